```python
import math
import jax, jax.numpy as jnp
from jax import lax
import numpy as np

D_MODEL = 2048
BATCH = 16
SEQ = 256
DEPTH = 4
DEC_BATCH = 4
DEC_SEQ = 1024
PAST_LEN = 512

GRID_W = 64
N_MIXERS = 4
N_LAYERS_A = (DEPTH + N_MIXERS - 1) // N_MIXERS
N_LAYERS_B = (DEPTH + N_MIXERS - 2) // N_MIXERS
N_LAYERS_C = (DEPTH + N_MIXERS - 3) // N_MIXERS
N_LAYERS_D = (DEPTH + N_MIXERS - 4) // N_MIXERS
D_FF = 4 * D_MODEL
N_MOD = 6
RMS_EPS = 1e-6
ROPE_THETA = 10000.0
Q_BLOCK = 128

A_HEADS = 16
A_HALF_DIM = D_MODEL // A_HEADS // 2
A_V_DIM = 2 * A_HALF_DIM

B_HEADS = 8
B_DK = D_MODEL // 2 // B_HEADS
B_DV = D_MODEL // B_HEADS
B_CHUNK = 64
B_IN = 2 * B_HEADS * B_DK + 2 * B_HEADS * B_DV + 4 * B_HEADS

C_HEADS = 16
C_HEAD_DIM = D_MODEL // C_HEADS
NA_ROWS_MAX = 8
NA_COLS = 16
NA_QCOLS = 16
NA_KCOLS = 32

D_HEADS = 32
D_KV_HEADS = 8
D_GROUP = D_HEADS // D_KV_HEADS
D_HEAD_DIM = D_MODEL // D_HEADS
WINDOW = 128

kernel_name = 'hybrid_diffusion_trunk_step'


def rmsnorm(x, g):
    xf = x.astype(jnp.float32)
    y = xf * lax.rsqrt(jnp.mean(xf * xf, axis=-1, keepdims=True) + RMS_EPS)
    return (y * g.astype(jnp.float32)).astype(x.dtype)


def adaln_in(x, g, shift, scale):
    return rmsnorm(x, g) * (1.0 + scale) + shift


def modulation(cvec, w, b):
    m = (jax.nn.silu(cvec) @ w + b)[:, None, :]
    return jnp.split(m, N_MOD, axis=-1)


def sqrelu_mlp(h, w1, w2):
    return jnp.square(jax.nn.relu(h @ w1)) @ w2


def rope_half(x, pos):
    half = x.shape[-1] // 2
    inv = ROPE_THETA ** (-jnp.arange(half, dtype=jnp.float32) / half)
    ang = pos.astype(jnp.float32)[:, None] * inv
    ang = ang.reshape((pos.shape[0],) + (1,) * (x.ndim - 3) + (half,))
    cos, sin = jnp.cos(ang).astype(x.dtype), jnp.sin(ang).astype(x.dtype)
    x1, x2 = x[..., :half], x[..., half:]
    return jnp.concatenate([x1 * cos - x2 * sin, x1 * sin + x2 * cos], axis=-1)


def axial_rope(x):
    t = jnp.arange(x.shape[1])
    d = x.shape[-1] // 2
    return jnp.concatenate([rope_half(x[..., :d], t // GRID_W), rope_half(x[..., d:], t % GRID_W)], axis=-1)


def map_query_blocks(fn, q):
    B, T = q.shape[:2]
    nb = T // Q_BLOCK
    qb = jnp.swapaxes(q.reshape((B, nb, Q_BLOCK) + q.shape[2:]), 0, 1)
    out = lax.map(fn, (qb, jnp.arange(nb)))
    return jnp.swapaxes(out, 0, 1).reshape((B, T) + out.shape[3:])


def softmax_attn_block(qb, k, v):
    s = jnp.einsum('bqhd,bkhd->bhqk', qb, k, preferred_element_type=jnp.float32) * (qb.shape[-1] ** -0.5)
    p = jax.nn.softmax(s, axis=-1)
    return jnp.einsum('bhqk,bkhd->bqhd', p.astype(v.dtype), v)


def diff_split(h, w_in):
    B, T, _ = h.shape
    q, k, v = jnp.split(h @ w_in, 3, axis=-1)
    return (q.reshape(B, T, A_HEADS, 2, A_HALF_DIM), k.reshape(B, T, A_HEADS, 2, A_HALF_DIM),
            v.reshape(B, T, A_HEADS, A_V_DIM))


def diff_lambda(lam_p, lam_init):
    lp = lam_p.astype(jnp.float32)
    return jnp.exp(jnp.sum(lp[0] * lp[1], -1)) - jnp.exp(jnp.sum(lp[2] * lp[3], -1)) + lam_init


def diff_core(qb, k, v, lam):
    s = jnp.einsum('bqhjd,bkhjd->bhjqk', qb, k, preferred_element_type=jnp.float32) * (A_HALF_DIM ** -0.5)
    p = jax.nn.softmax(s, axis=-1)
    a = p[:, :, 0] - lam[None, :, None, None] * p[:, :, 1]
    return jnp.einsum('bhqk,bkhd->bqhd', a.astype(v.dtype), v)


def diff_out(o, subln, lam_init, w_out):
    B, T = o.shape[:2]
    return (rmsnorm(o, subln) * (1.0 - lam_init)).reshape(B, T, -1) @ w_out


def diff_attn_context(h, w_in, w_out, lam_p, subln, lam_init):
    B, T, _ = h.shape
    q, k, v = diff_split(h, w_in)
    lam = diff_lambda(lam_p, lam_init)
    o = map_query_blocks(lambda a: diff_core(a[0], k, v, lam), q)
    return diff_out(o, subln, lam_init, w_out), k.reshape(B, T, A_HEADS, A_V_DIM), v


def diff_attn_latent(h, ck, cv, w_in, w_out, lam_p, subln, lam_init):
    B, T, _ = h.shape
    q, k, v = diff_split(h, w_in)
    q, k = axial_rope(q), axial_rope(k)
    kk = jnp.concatenate([ck.reshape(B, ck.shape[1], A_HEADS, 2, A_HALF_DIM).astype(k.dtype), k], axis=1)
    vv = jnp.concatenate([cv.astype(v.dtype), v], axis=1)
    lam = diff_lambda(lam_p, lam_init)
    o = map_query_blocks(lambda a: diff_core(a[0], kk, vv, lam), q)
    return diff_out(o, subln, lam_init, w_out)


def mlstm_chunked(q, k, v, ig, lf, C0, n0, m0):
    B, T = q.shape[:2]
    L = B_CHUNK
    nc = T // L

    def to_chunks(a):
        return jnp.swapaxes(a.reshape((B, nc, L) + a.shape[2:]), 0, 1)

    xs = (to_chunks(q), to_chunks(k), to_chunks(v), to_chunks(ig), to_chunks(lf))
    causal = jnp.tril(jnp.ones((L, L), dtype=bool))

    def step(carry, xc):
        C, n, m = carry
        qc, kc, vc, ic, fc = xc
        b = jnp.cumsum(fc, axis=1)
        dm = b[:, :, None, :] - b[:, None, :, :] + ic[:, None, :, :]
        dm = jnp.where(causal[None, :, :, None], dm, -jnp.inf)
        inter = b + m[:, None, :]
        m_t = jnp.maximum(inter, jnp.max(dm, axis=2))
        w = jnp.exp(dm - m_t[:, :, None, :])
        a_in = jnp.exp(inter - m_t)
        sw = jnp.einsum('bthd,bshd->btsh', qc, kc, preferred_element_type=jnp.float32) * w
        num = (jnp.einsum('btsh,bshv->bthv', sw, vc, preferred_element_type=jnp.float32)
               + a_in[..., None] * jnp.einsum('bhvd,bthd->bthv', C, qc, preferred_element_type=jnp.float32))
        den = jnp.sum(sw, axis=2) + a_in * jnp.einsum('bhd,bthd->bth', n, qc, preferred_element_type=jnp.float32)
        h = num / jnp.maximum(jnp.abs(den), jnp.exp(-m_t))[..., None]
        wl, al = w[:, -1], a_in[:, -1]
        C_new = al[..., None, None] * C + jnp.einsum('bsh,bshv,bshd->bhvd', wl, vc, kc, preferred_element_type=jnp.float32)
        n_new = al[..., None] * n + jnp.einsum('bsh,bshd->bhd', wl, kc, preferred_element_type=jnp.float32)
        return (C_new, n_new, m_t[:, -1]), h

    init = (C0.astype(jnp.float32), n0.astype(jnp.float32), m0.astype(jnp.float32))
    (C, n, m), hs = lax.scan(step, init, xs)
    hs = jnp.swapaxes(hs, 0, 1).reshape(B, T, B_HEADS, B_DV)
    return hs, C, n, m


def mlstm_proj(h, w_in, gate_bias):
    B, T, _ = h.shape
    qk, vd = B_HEADS * B_DK, B_HEADS * B_DV
    q, k, v, o, g = jnp.split(h @ w_in, [qk, 2 * qk, 2 * qk + vd, 2 * qk + 2 * vd], axis=-1)
    q = q.reshape(B, T, B_HEADS, B_DK) * (B_DK ** -0.5)
    k = k.reshape(B, T, B_HEADS, B_DK)
    v = v.reshape(B, T, B_HEADS, B_DV)
    g = g.astype(jnp.float32).reshape(B, T, 4, B_HEADS) + gate_bias.astype(jnp.float32)
    return q, k, v, o, g


def _rev(a, d):
    return a if d == 0 else jnp.flip(a, axis=1)


def mlstm_bidir(q, k, v, g, C0, n0, m0):
    hsum, Cs, ns, ms = 0.0, [], [], []
    for d in range(2):
        ig = g[:, :, 2 * d]
        lf = jax.nn.log_sigmoid(g[:, :, 2 * d + 1])
        hd, Cd, nd, md = mlstm_chunked(_rev(q, d), _rev(k, d), _rev(v, d), _rev(ig, d), _rev(lf, d),
                                       C0[:, d], n0[:, d], m0[:, d])
        hsum = hsum + _rev(hd, d)
        Cs.append(Cd); ns.append(nd); ms.append(md)
    return hsum, jnp.stack(Cs, axis=1), jnp.stack(ns, axis=1), jnp.stack(ms, axis=1)


def mlstm_out(hsum, o, norm_w, w_out):
    B, T = hsum.shape[:2]
    hn = rmsnorm(hsum.astype(o.dtype), norm_w.reshape(B_HEADS, B_DV))
    return (hn.reshape(B, T, -1) * jax.nn.sigmoid(o)) @ w_out


def mlstm_context(h, w_in, gate_bias, norm_w, w_out):
    B = h.shape[0]
    q, k, v, o, g = mlstm_proj(h, w_in, gate_bias)
    C0 = jnp.zeros((B, 2, B_HEADS, B_DV, B_DK), jnp.float32)
    n0 = jnp.zeros((B, 2, B_HEADS, B_DK), jnp.float32)
    m0 = jnp.zeros((B, 2, B_HEADS), jnp.float32)
    hsum, C, n, m = mlstm_bidir(q, k, v, g, C0, n0, m0)
    return mlstm_out(hsum, o, norm_w, w_out), C, n, m


def mlstm_latent(h, C0, n0, m0, w_in, gate_bias, norm_w, w_out):
    q, k, v, o, g = mlstm_proj(h, w_in, gate_bias)
    hsum, _, _, _ = mlstm_bidir(q, k, v, g, C0, n0, m0)
    return mlstm_out(hsum, o, norm_w, w_out)


def na_split(h, w_in):
    B, T, _ = h.shape
    q, k, v = jnp.split(h @ w_in, 3, axis=-1)
    return tuple(a.reshape(B, T, C_HEADS, C_HEAD_DIM) for a in (q, k, v))


def na_context(h, w_in, w_out):
    B, T, _ = h.shape
    q, k, v = na_split(h, w_in)
    o = map_query_blocks(lambda a: softmax_attn_block(a[0], k, v), q)
    return o.reshape(B, T, -1) @ w_out, k, v


def na_latent(h, ck, cv, w_in, w_out, rpb):
    B, T, _ = h.shape
    rows = T // GRID_W
    kh = min(NA_ROWS_MAX, rows)
    q, k, v = na_split(h, w_in)
    ck, cv = ck.astype(k.dtype), cv.astype(v.dtype)
    qg = q.reshape(B, rows, GRID_W, C_HEADS, C_HEAD_DIM)
    kg = k.reshape(B, rows, GRID_W, C_HEADS, C_HEAD_DIM)
    vg = v.reshape(B, rows, GRID_W, C_HEADS, C_HEAD_DIM)
    ncb = GRID_W // NA_QCOLS
    col0 = np.clip(np.arange(ncb) * NA_QCOLS - NA_COLS // 2, 0, GRID_W - NA_KCOLS)
    kcols = col0[:, None] + np.arange(NA_KCOLS)
    qcols = np.arange(ncb)[:, None] * NA_QCOLS + np.arange(NA_QCOLS)
    cstart = np.clip(qcols - NA_COLS // 2, 0, GRID_W - NA_COLS)
    col_ok = (kcols[:, None, :] >= cstart[..., None]) & (kcols[:, None, :] < cstart[..., None] + NA_COLS)
    dcol = np.clip(kcols[:, None, :] - qcols[..., None], -(NA_COLS - 1), NA_COLS - 1) + NA_COLS - 1
    bias_col = rpb.astype(jnp.float32)[:, :, dcol]
    n_nb = kh * NA_KCOLS
    scale = C_HEAD_DIM ** -0.5

    def row_fn(a):
        qr, r = a
        r0 = jnp.clip(r - kh // 2, 0, rows - kh)
        kr = lax.dynamic_slice_in_dim(kg, r0, kh, axis=1)[:, :, kcols]
        vr = lax.dynamic_slice_in_dim(vg, r0, kh, axis=1)[:, :, kcols]
        qb = qr.reshape(B, ncb, NA_QCOLS, C_HEADS, C_HEAD_DIM)
        s_nb = jnp.einsum('bjqhd,bijkhd->bhjqik', qb, kr, preferred_element_type=jnp.float32) * scale
        drow = r0 + jnp.arange(kh) - r + NA_ROWS_MAX - 1
        bias = jnp.transpose(jnp.take(bias_col, drow, axis=1), (0, 2, 3, 1, 4))
        s_nb = jnp.where(col_ok[:, :, None, :], s_nb + bias, -jnp.inf)
        s_ctx = jnp.einsum('bjqhd,bkhd->bhjqk', qb, ck, preferred_element_type=jnp.float32) * scale
        p = jax.nn.softmax(jnp.concatenate([s_nb.reshape(B, C_HEADS, ncb, NA_QCOLS, n_nb), s_ctx], axis=-1), axis=-1)
        p = p.astype(v.dtype)
        p_nb = p[..., :n_nb].reshape(B, C_HEADS, ncb, NA_QCOLS, kh, NA_KCOLS)
        o = (jnp.einsum('bhjqik,bijkhd->bjqhd', p_nb, vr)
             + jnp.einsum('bhjqk,bkhd->bjqhd', p[..., n_nb:], cv))
        return o.reshape(B, GRID_W, C_HEADS, C_HEAD_DIM)

    o = lax.map(row_fn, (jnp.swapaxes(qg, 0, 1), jnp.arange(rows)))
    return jnp.swapaxes(o, 0, 1).reshape(B, T, -1) @ w_out


def gqa_split(h, w_in):
    B, T, _ = h.shape
    q, k, v = jnp.split(h @ w_in, [D_HEADS * D_HEAD_DIM, (D_HEADS + D_KV_HEADS) * D_HEAD_DIM], axis=-1)
    return (q.reshape(B, T, D_KV_HEADS, D_GROUP, D_HEAD_DIM), k.reshape(B, T, D_KV_HEADS, D_HEAD_DIM),
            v.reshape(B, T, D_KV_HEADS, D_HEAD_DIM))


def sink_probs(s, sk):
    skb = sk[None, :, :, None, None]
    m = jnp.maximum(jnp.max(s, axis=-1, keepdims=True), skb)
    e = jnp.exp(s - m)
    return e / (jnp.sum(e, axis=-1, keepdims=True) + jnp.exp(skb - m))


def gqa_context(h, w_in, w_out, sink):
    B, T, _ = h.shape
    q, k, v = gqa_split(h, w_in)
    sk = sink.astype(jnp.float32).reshape(D_KV_HEADS, D_GROUP)

    def blk(a):
        s = jnp.einsum('bqhgd,bkhd->bhgqk', a[0], k, preferred_element_type=jnp.float32) * (D_HEAD_DIM ** -0.5)
        return jnp.einsum('bhgqk,bkhd->bqhgd', sink_probs(s, sk).astype(v.dtype), v)

    o = map_query_blocks(blk, q)
    return o.reshape(B, T, -1) @ w_out, k, v


def gqa_latent(h, ck, cv, w_in, w_out, sink):
    B, T, _ = h.shape
    q, k, v = gqa_split(h, w_in)
    q, k = axial_rope(q), axial_rope(k)
    ck, cv = ck.astype(k.dtype), cv.astype(v.dtype)
    pad = ((0, 0), (Q_BLOCK, Q_BLOCK), (0, 0), (0, 0))
    kp, vp = jnp.pad(k, pad), jnp.pad(v, pad)
    band = 3 * Q_BLOCK
    sk = sink.astype(jnp.float32).reshape(D_KV_HEADS, D_GROUP)
    scale = D_HEAD_DIM ** -0.5

    def blk(a):
        qb, bi = a
        kb = lax.dynamic_slice_in_dim(kp, bi * Q_BLOCK, band, axis=1)
        vb = lax.dynamic_slice_in_dim(vp, bi * Q_BLOCK, band, axis=1)
        qpos = bi * Q_BLOCK + jnp.arange(Q_BLOCK)
        kpos = (bi - 1) * Q_BLOCK + jnp.arange(band)
        valid = (kpos[None, :] >= 0) & (kpos[None, :] < T) & (jnp.abs(qpos[:, None] - kpos[None, :]) <= WINDOW)
        s_loc = jnp.einsum('bqhgd,bkhd->bhgqk', qb, kb, preferred_element_type=jnp.float32) * scale
        s_loc = jnp.where(valid, s_loc, -jnp.inf)
        s_ctx = jnp.einsum('bqhgd,bkhd->bhgqk', qb, ck, preferred_element_type=jnp.float32) * scale
        p = sink_probs(jnp.concatenate([s_loc, s_ctx], axis=-1), sk).astype(v.dtype)
        return (jnp.einsum('bhgqk,bkhd->bqhgd', p[..., :band], vb)
                + jnp.einsum('bhgqk,bkhd->bqhgd', p[..., band:], cv))

    o = map_query_blocks(blk, q)
    return o.reshape(B, T, -1) @ w_out


def setup_inputs(seed: int = 0) -> dict:
    key = jax.random.key(seed)
    ks = jax.random.split(key, 40)
    D = D_MODEL

    def nrm(i, shape, s=1.0):
        return s * jax.random.normal(ks[i], shape, jnp.float32)

    return {
        'x_prompt': nrm(0, (BATCH, SEQ, D)),
        'x_sample': nrm(1, (DEC_BATCH, DEC_SEQ, D)),
        'cache_a_k': nrm(2, (DEC_BATCH, N_LAYERS_A, PAST_LEN, A_HEADS, A_V_DIM)),
        'cache_a_v': nrm(3, (DEC_BATCH, N_LAYERS_A, PAST_LEN, A_HEADS, A_V_DIM)),
        'state_b_C': nrm(4, (DEC_BATCH, N_LAYERS_B, 2, B_HEADS, B_DV, B_DK)),
        'state_b_n': nrm(5, (DEC_BATCH, N_LAYERS_B, 2, B_HEADS, B_DK)),
        'state_b_m': nrm(6, (DEC_BATCH, N_LAYERS_B, 2, B_HEADS)),
        'cache_c_k': nrm(7, (DEC_BATCH, N_LAYERS_C, PAST_LEN, C_HEADS, C_HEAD_DIM)),
        'cache_c_v': nrm(8, (DEC_BATCH, N_LAYERS_C, PAST_LEN, C_HEADS, C_HEAD_DIM)),
        'cache_d_k': nrm(9, (DEC_BATCH, N_LAYERS_D, PAST_LEN, D_KV_HEADS, D_HEAD_DIM)),
        'cache_d_v': nrm(10, (DEC_BATCH, N_LAYERS_D, PAST_LEN, D_KV_HEADS, D_HEAD_DIM)),
        'c': nrm(11, (DEC_BATCH, D)),
        'c_ctx': nrm(12, (D,)),
        'w_mod': nrm(13, (DEPTH, D, N_MOD * D), 0.5 * D ** -0.5),
        'b_mod': nrm(14, (DEPTH, N_MOD * D), 0.01),
        'g_norm': 1.0 + nrm(15, (DEPTH, 4, D), 0.05),
        'w_ff1': nrm(16, (DEPTH, D, D_FF), D ** -0.5),
        'w_ff2': nrm(17, (DEPTH, D_FF, D), D_FF ** -0.5),
        'a_w_in': nrm(18, (N_LAYERS_A, D, 3 * A_HEADS * A_V_DIM), D ** -0.5),
        'a_w_out': nrm(19, (N_LAYERS_A, A_HEADS * A_V_DIM, D), (A_HEADS * A_V_DIM) ** -0.5),
        'a_lambda': nrm(20, (N_LAYERS_A, 4, A_HEADS, A_HALF_DIM), 0.1),
        'a_subln': 1.0 + nrm(21, (N_LAYERS_A, A_V_DIM), 0.05),
        'b_w_in': nrm(22, (N_LAYERS_B, D, B_IN), D ** -0.5),
        'b_gate_bias': nrm(23, (N_LAYERS_B, 4, B_HEADS), 0.1) + jnp.array([0.0, 3.0, 0.0, 3.0], jnp.float32)[None, :, None],
        'b_w_out': nrm(24, (N_LAYERS_B, B_HEADS * B_DV, D), (B_HEADS * B_DV) ** -0.5),
        'b_norm': 1.0 + nrm(25, (N_LAYERS_B, B_HEADS * B_DV), 0.05),
        'c_w_in': nrm(26, (N_LAYERS_C, D, 3 * C_HEADS * C_HEAD_DIM), D ** -0.5),
        'c_w_out': nrm(27, (N_LAYERS_C, C_HEADS * C_HEAD_DIM, D), (C_HEADS * C_HEAD_DIM) ** -0.5),
        'c_rpb': nrm(28, (N_LAYERS_C, C_HEADS, 2 * NA_ROWS_MAX - 1, 2 * NA_COLS - 1), 0.5),
        'd_w_in': nrm(29, (N_LAYERS_D, D, (D_HEADS + 2 * D_KV_HEADS) * D_HEAD_DIM), D ** -0.5),
        'd_w_out': nrm(30, (N_LAYERS_D, D_HEADS * D_HEAD_DIM, D), (D_HEADS * D_HEAD_DIM) ** -0.5),
        'd_sink': nrm(31, (N_LAYERS_D, D_HEADS)),
    }


def reference(x_prompt, x_sample, cache_a_k, cache_a_v, state_b_C, state_b_n, state_b_m,
              cache_c_k, cache_c_v, cache_d_k, cache_d_v, c, c_ctx, w_mod, b_mod, g_norm,
              w_ff1, w_ff2, a_w_in, a_w_out, a_lambda, a_subln, b_w_in, b_gate_bias, b_w_out,
              b_norm, c_w_in, c_w_out, c_rpb, d_w_in, d_w_out, d_sink):
    xp, xs = x_prompt, x_sample
    a_k, a_v, b_C, b_n, b_m, c_k, c_v, d_k, d_v = [], [], [], [], [], [], [], [], []
    for i in range(DEPTH):
        kind, j = i % N_MIXERS, i // N_MIXERS
        mod_p = modulation(c_ctx[None, :], w_mod[i], b_mod[i])
        mod_s = modulation(c, w_mod[i], b_mod[i])
        hp = adaln_in(xp, g_norm[i, 0], mod_p[0], mod_p[1])
        hs = adaln_in(xs, g_norm[i, 0], mod_s[0], mod_s[1])
        if kind == 0:
            lam_init = 0.8 - 0.6 * math.exp(-0.3 * i)
            yp, kc, vc = diff_attn_context(hp, a_w_in[j], a_w_out[j], a_lambda[j], a_subln[j], lam_init)
            ys = diff_attn_latent(hs, cache_a_k[:, j], cache_a_v[:, j], a_w_in[j], a_w_out[j],
                                  a_lambda[j], a_subln[j], lam_init)
            a_k.append(kc); a_v.append(vc)
        elif kind == 1:
            yp, Cc, nc_, mc = mlstm_context(hp, b_w_in[j], b_gate_bias[j], b_norm[j], b_w_out[j])
            ys = mlstm_latent(hs, state_b_C[:, j], state_b_n[:, j], state_b_m[:, j],
                              b_w_in[j], b_gate_bias[j], b_norm[j], b_w_out[j])
            b_C.append(Cc); b_n.append(nc_); b_m.append(mc)
        elif kind == 2:
            yp, kc, vc = na_context(hp, c_w_in[j], c_w_out[j])
            ys = na_latent(hs, cache_c_k[:, j], cache_c_v[:, j], c_w_in[j], c_w_out[j], c_rpb[j])
            c_k.append(kc); c_v.append(vc)
        else:
            yp, kc, vc = gqa_context(hp, d_w_in[j], d_w_out[j], d_sink[j])
            ys = gqa_latent(hs, cache_d_k[:, j], cache_d_v[:, j], d_w_in[j], d_w_out[j], d_sink[j])
            d_k.append(kc); d_v.append(vc)
        xp = xp + mod_p[2] * rmsnorm(yp, g_norm[i, 1])
        xs = xs + mod_s[2] * rmsnorm(ys, g_norm[i, 1])
        hp = adaln_in(xp, g_norm[i, 2], mod_p[3], mod_p[4])
        hs = adaln_in(xs, g_norm[i, 2], mod_s[3], mod_s[4])
        xp = xp + mod_p[5] * rmsnorm(sqrelu_mlp(hp, w_ff1[i], w_ff2[i]), g_norm[i, 3])
        xs = xs + mod_s[5] * rmsnorm(sqrelu_mlp(hs, w_ff1[i], w_ff2[i]), g_norm[i, 3])
    return (xp, xs, jnp.stack(a_k, axis=1), jnp.stack(a_v, axis=1), jnp.stack(b_C, axis=1),
            jnp.stack(b_n, axis=1), jnp.stack(b_m, axis=1), jnp.stack(c_k, axis=1),
            jnp.stack(c_v, axis=1), jnp.stack(d_k, axis=1), jnp.stack(d_v, axis=1))
```

```python
import functools
import math

import numpy as np
import jax
import jax.numpy as jnp
from jax import lax
from jax.experimental import pallas as pl
from jax.experimental.pallas import tpu as pltpu

D_MODEL = 2048
BATCH = 16
SEQ = 256
DEPTH = 4
DEC_BATCH = 4
DEC_SEQ = 1024
PAST_LEN = 512
GRID_W = 64
N_MIXERS = 4
D_FF = 4 * D_MODEL
N_MOD = 6
RMS_EPS = 1e-6
ROPE_THETA = 10000.0

A_HEADS = 16
A_HALF_DIM = 64
A_V_DIM = 128
B_HEADS = 8
B_DK = 128
B_DV = 256
B_QK = B_HEADS * B_DK
B_VD = B_HEADS * B_DV
B_MAIN = 2 * B_QK + 2 * B_VD
C_HEADS = 16
C_HEAD_DIM = 128
NA_ROWS = 8
NA_COLS = 16
GRID_H = DEC_SEQ // GRID_W
D_HEADS = 32
D_KV_HEADS = 8
D_GROUP = 4
D_HEAD_DIM = 64
WINDOW = 128

N_PROMPT = BATCH * SEQ
N_SAMPLE = DEC_BATCH * DEC_SEQ
N_TOK = N_PROMPT + N_SAMPLE
N_COND = 8

LANES = 128
HALF = LANES // 2
NEG = -1e30
MIB = 1024 * 1024
F32 = jnp.float32
BF16 = jnp.bfloat16


def _params(sem, vmem_mib):
    return pltpu.CompilerParams(dimension_semantics=sem, vmem_limit_bytes=vmem_mib * MIB)


def _dot(a, b):
    return jnp.dot(a, b, preferred_element_type=F32)


def _dot_nt(a, b):
    return lax.dot_general(a, b, (((1,), (1,)), ((), ())), preferred_element_type=F32)


def _dot_tn(a, b):
    return lax.dot_general(a, b, (((0,), (0,)), ((), ())), preferred_element_type=F32)


def _rms(x, g):
    return x * lax.rsqrt(jnp.mean(x * x, axis=-1, keepdims=True) + RMS_EPS) * g


def _cond_index(row0):
    return jnp.where(row0 < N_PROMPT, 0, 1 + (row0 - N_PROMPT) // DEC_SEQ)


def _mod_kernel(c_ref, w_ref, b_ref, o_ref):
    c = c_ref[...]
    s = (c / (1.0 + jnp.exp(-c))).astype(BF16)
    o_ref[...] = _dot(s, w_ref[...].astype(BF16)) + b_ref[...]


def _modulation(cond, w_mod, b_mod):
    tn = 1024
    n = N_MOD * D_MODEL
    return pl.pallas_call(
        _mod_kernel,
        out_shape=jax.ShapeDtypeStruct((DEPTH, N_COND, n), F32),
        grid=(DEPTH, n // tn),
        in_specs=[
            pl.BlockSpec((N_COND, D_MODEL), lambda l, j: (0, 0)),
            pl.BlockSpec((None, D_MODEL, tn), lambda l, j: (l, 0, j)),
            pl.BlockSpec((None, 1, tn), lambda l, j: (l, 0, j)),
        ],
        out_specs=pl.BlockSpec((None, N_COND, tn), lambda l, j: (l, 0, j)),
        compiler_params=_params(("arbitrary", "arbitrary"), 40),
        name="modulation",
    )(cond, w_mod, b_mod.reshape(DEPTH, 1, n))


ROW_TILE = 256


def _mod_spec(layer, which):
    return pl.BlockSpec((None, None, None, 1, D_MODEL),
                        lambda i: (layer, _cond_index(i * ROW_TILE), which, 0, 0))


def _gain_spec(layer, which):
    return pl.BlockSpec((None, None, 1, D_MODEL), lambda i: (layer, which, 0, 0))


def _adaln_kernel(x_ref, g_ref, shift_ref, scale_ref, h_ref):
    h = _rms(x_ref[...], g_ref[...]) * (1.0 + scale_ref[...]) + shift_ref[...]
    h_ref[...] = h.astype(BF16)


def _adaln(x, gains, mod, layer):
    row = pl.BlockSpec((ROW_TILE, D_MODEL), lambda i: (i, 0))
    return pl.pallas_call(
        _adaln_kernel,
        out_shape=jax.ShapeDtypeStruct((N_TOK, D_MODEL), BF16),
        grid=(N_TOK // ROW_TILE,),
        in_specs=[row, _gain_spec(layer, 0), _mod_spec(layer, 0), _mod_spec(layer, 1)],
        out_specs=row,
        compiler_params=_params(("arbitrary",), 32),
        name="adaln",
    )(x, gains, mod, mod)


def _residual_adaln_kernel(x_ref, y_ref, gate_ref, gpost_ref, gpre_ref, shift_ref, scale_ref, xo_ref, h_ref):
    x = x_ref[...] + gate_ref[...] * _rms(y_ref[...], gpost_ref[...])
    xo_ref[...] = x
    h = _rms(x, gpre_ref[...]) * (1.0 + scale_ref[...]) + shift_ref[...]
    h_ref[...] = h.astype(BF16)


def _residual_kernel(x_ref, y_ref, gate_ref, gpost_ref, xo_ref):
    xo_ref[...] = x_ref[...] + gate_ref[...] * _rms(y_ref[...], gpost_ref[...])


def _residual(x, y, gains, mod, layer, gate_idx, post_idx, nxt):
    row = pl.BlockSpec((ROW_TILE, D_MODEL), lambda i: (i, 0))
    in_specs = [row, row, _mod_spec(layer, gate_idx), _gain_spec(layer, post_idx)]
    args = [x, y, mod, gains]
    x_shape = jax.ShapeDtypeStruct((N_TOK, D_MODEL), F32)
    if nxt is None:
        return pl.pallas_call(
            _residual_kernel, out_shape=x_shape, grid=(N_TOK // ROW_TILE,),
            in_specs=in_specs, out_specs=row,
            compiler_params=_params(("arbitrary",), 32), name="residual",
        )(*args), None
    nl, ng, nshift, nscale = nxt
    in_specs += [_gain_spec(nl, ng), _mod_spec(nl, nshift), _mod_spec(nl, nscale)]
    args += [gains, mod, mod]
    return pl.pallas_call(
        _residual_adaln_kernel,
        out_shape=(x_shape, jax.ShapeDtypeStruct((N_TOK, D_MODEL), BF16)),
        grid=(N_TOK // ROW_TILE,),
        in_specs=in_specs, out_specs=(row, row),
        compiler_params=_params(("arbitrary",), 40), name="residual_adaln",
    )(*args)


def _proj_kernel(a_ref, w_ref, o_ref, wbf_ref):
    @pl.when(pl.program_id(1) == 0)
    def _():
        wbf_ref[...] = w_ref[...].astype(BF16)

    o_ref[...] = _dot(a_ref[...], wbf_ref[...]).astype(o_ref.dtype)


def _project(a, w, out_dtype=F32):
    k, n = w.shape
    tm = 1024
    tn = min(n, 1024)
    return pl.pallas_call(
        _proj_kernel,
        out_shape=jax.ShapeDtypeStruct((N_TOK, n), out_dtype),
        grid=(n // tn, N_TOK // tm),
        in_specs=[pl.BlockSpec((tm, k), lambda j, i: (i, 0)),
                  pl.BlockSpec((k, tn), lambda j, i: (0, j))],
        out_specs=pl.BlockSpec((tm, tn), lambda j, i: (i, j)),
        scratch_shapes=[pltpu.VMEM((k, tn), BF16)],
        compiler_params=_params(("arbitrary", "arbitrary"), 48),
        name="project",
    )(a, w)


def _mlp_kernel(h_ref, w1_ref, w2_ref, o_ref):
    u = jnp.maximum(_dot(h_ref[...], w1_ref[...]), 0.0)
    part = _dot((u * u).astype(BF16), w2_ref[...])

    @pl.when(pl.program_id(1) == 0)
    def _():
        o_ref[...] = part

    @pl.when(pl.program_id(1) != 0)
    def _():
        o_ref[...] += part


def _mlp(h, w1, w2):
    tm, tf = 1024, 512
    return pl.pallas_call(
        _mlp_kernel,
        out_shape=jax.ShapeDtypeStruct((N_TOK, D_MODEL), F32),
        grid=(N_TOK // tm, D_FF // tf),
        in_specs=[pl.BlockSpec((tm, D_MODEL), lambda i, f: (i, 0)),
                  pl.BlockSpec((D_MODEL, tf), lambda i, f: (0, f)),
                  pl.BlockSpec((tf, D_MODEL), lambda i, f: (f, 0))],
        out_specs=pl.BlockSpec((tm, D_MODEL), lambda i, f: (i, 0)),
        compiler_params=_params(("arbitrary", "arbitrary"), 56),
        name="mlp",
    )(h, w1, w2)


def _rope_tables():
    t = jnp.arange(DEC_SEQ)
    lane = np.arange(LANES)
    f = lane % 32
    first = f < 16
    inv = ROPE_THETA ** (-jnp.arange(16, dtype=F32) / 16)
    pos = jnp.where((lane % 64 < 32)[None, :], (t // GRID_W)[:, None], (t % GRID_W)[:, None]).astype(F32)
    ang = pos * inv[f % 16][None, :]
    cos, sin = jnp.cos(ang), jnp.sin(ang)
    sin_next = jnp.where(first[None, :], -sin, 0.0)
    sin_prev = jnp.where(first[None, :], 0.0, sin)
    return cos, sin_next, sin_prev


def _rope(x, cos, sin_next, sin_prev):
    return (x * cos + pltpu.roll(x, LANES - 16, axis=1) * sin_next
            + pltpu.roll(x, 16, axis=1) * sin_prev)


def _lane_lo(shape):
    return lax.broadcasted_iota(jnp.int32, shape, 1) < HALF


def _diff_lambda(lam_ref, lam_init):
    lp = lam_ref[...]
    a = jnp.sum(lp[0] * lp[1], axis=-1, keepdims=True)
    b = jnp.sum(lp[2] * lp[3], axis=-1, keepdims=True)
    return jnp.exp(a) - jnp.exp(b) + lam_init


def _diff_finish(o, subln_ref, lam_init):
    return (_rms(o, subln_ref[...]) * (1.0 - lam_init)).astype(BF16)


def _a_ctx_kernel(lam_ref, subln_ref, q_ref, k_ref, v_ref, o_ref, *, lam_init):
    q = q_ref[...]
    k = k_ref[...].astype(BF16)
    lo = _lane_lo(q.shape)
    scale = A_HALF_DIM ** -0.5
    probs = []
    for half in (0, 1):
        qh = jnp.where(lo if half == 0 else ~lo, q, 0.0).astype(BF16)
        s = _dot_nt(qh, k) * scale
        e = jnp.exp(s - jnp.max(s, axis=-1, keepdims=True))
        probs.append(e * (1.0 / jnp.sum(e, axis=-1, keepdims=True)))
    lam = _diff_lambda(lam_ref, lam_init)
    a = (probs[0] - lam * probs[1]).astype(BF16)
    o_ref[...] = _diff_finish(_dot(a, v_ref[...].astype(BF16)), subln_ref, lam_init)


def _a_lat_kernel(lam_ref, subln_ref, cq_ref, nq_ref, pq_ref, ck_ref, nk_ref, pk_ref,
                  q_ref, k_ref, v_ref, pk_cache_ref, pv_cache_ref, o_ref, *, lam_init):
    q = _rope(q_ref[...], cq_ref[...], nq_ref[...], pq_ref[...])
    k = _rope(k_ref[...], ck_ref[...], nk_ref[...], pk_ref[...]).astype(BF16)
    kc = pk_cache_ref[...].astype(BF16)
    lo = _lane_lo(q.shape)
    scale = A_HALF_DIM ** -0.5
    probs = []
    for half in (0, 1):
        qh = jnp.where(lo if half == 0 else ~lo, q, 0.0).astype(BF16)
        s_c = _dot_nt(qh, kc) * scale
        s_l = _dot_nt(qh, k) * scale
        m = jnp.maximum(jnp.max(s_c, axis=-1, keepdims=True), jnp.max(s_l, axis=-1, keepdims=True))
        e_c, e_l = jnp.exp(s_c - m), jnp.exp(s_l - m)
        r = 1.0 / (jnp.sum(e_c, axis=-1, keepdims=True) + jnp.sum(e_l, axis=-1, keepdims=True))
        probs.append((e_c * r, e_l * r))
    lam = _diff_lambda(lam_ref, lam_init)
    a_c = (probs[0][0] - lam * probs[1][0]).astype(BF16)
    a_l = (probs[0][1] - lam * probs[1][1]).astype(BF16)
    o = _dot(a_c, pv_cache_ref[...].astype(BF16)) + _dot(a_l, v_ref[...].astype(BF16))
    o_ref[...] = _diff_finish(o, subln_ref, lam_init)


def _mixer_a(qkv, cache_k, cache_v, lam_p, subln, lam_init, rope):
    nh = A_HEADS
    lam4 = lam_p.reshape(4, nh, 1, A_HALF_DIM)
    sub2 = subln.reshape(1, A_V_DIM)
    lam_spec2 = pl.BlockSpec((4, None, 1, A_HALF_DIM), lambda b, h: (0, h, 0, 0))
    sub_spec2 = pl.BlockSpec((1, A_V_DIM), lambda b, h: (0, 0))
    o_p = pl.pallas_call(
        functools.partial(_a_ctx_kernel, lam_init=lam_init),
        out_shape=jax.ShapeDtypeStruct((N_PROMPT, nh * A_V_DIM), BF16),
        grid=(BATCH, nh),
        in_specs=[lam_spec2, sub_spec2,
                  pl.BlockSpec((SEQ, LANES), lambda b, h: (b, h)),
                  pl.BlockSpec((SEQ, LANES), lambda b, h: (b, nh + h)),
                  pl.BlockSpec((SEQ, LANES), lambda b, h: (b, 2 * nh + h))],
        out_specs=pl.BlockSpec((SEQ, LANES), lambda b, h: (b, h)),
        compiler_params=_params(("arbitrary", "arbitrary"), 32),
        name="diff_attn_ctx",
    )(lam4, sub2, qkv, qkv, qkv)

    tq = 256
    nqb = DEC_SEQ // tq
    row0 = N_PROMPT // DEC_SEQ
    cos, s_next, s_prev = rope
    tab_q = pl.BlockSpec((tq, LANES), lambda b, h, i: (i, 0))
    tab_k = pl.BlockSpec((DEC_SEQ, LANES), lambda b, h, i: (0, 0))
    past = pl.BlockSpec((PAST_LEN, LANES), lambda b, h, i: (b, h))
    o_s = pl.pallas_call(
        functools.partial(_a_lat_kernel, lam_init=lam_init),
        out_shape=jax.ShapeDtypeStruct((N_SAMPLE, nh * A_V_DIM), BF16),
        grid=(DEC_BATCH, nh, nqb),
        in_specs=[pl.BlockSpec((4, None, 1, A_HALF_DIM), lambda b, h, i: (0, h, 0, 0)),
                  pl.BlockSpec((1, A_V_DIM), lambda b, h, i: (0, 0)),
                  tab_q, tab_q, tab_q, tab_k, tab_k, tab_k,
                  pl.BlockSpec((tq, LANES), lambda b, h, i: ((row0 + b) * nqb + i, h)),
                  pl.BlockSpec((DEC_SEQ, LANES), lambda b, h, i: (row0 + b, nh + h)),
                  pl.BlockSpec((DEC_SEQ, LANES), lambda b, h, i: (row0 + b, 2 * nh + h)),
                  past, past],
        out_specs=pl.BlockSpec((tq, LANES), lambda b, h, i: (b * nqb + i, h)),
        compiler_params=_params(("arbitrary", "arbitrary", "arbitrary"), 40),
        name="diff_attn_lat",
    )(lam4, sub2, cos, s_next, s_prev, cos, s_next, s_prev, qkv, qkv, qkv,
      cache_k.reshape(DEC_BATCH * PAST_LEN, nh * A_V_DIM), cache_v.reshape(DEC_BATCH * PAST_LEN, nh * A_V_DIM))
    return o_p, o_s


B_CHUNK = 256


def _log_sigmoid(x):
    return jnp.minimum(x, 0.0) - jnp.log(1.0 + jnp.exp(-jnp.abs(x)))


def _mlstm_kernel(*refs, n_chunks, has_init, emit_state):
    it = iter(refs)
    q_ref, k_ref, v_ref, og_ref, gcol_ref, grow_ref, bcol_ref, brow_ref, nw_ref = (next(it) for _ in range(9))
    if has_init:
        c0_ref, n0_ref, m0_ref = next(it), next(it), next(it)
    h_ref = next(it)
    if emit_state:
        c_out, n_out, m_out = next(it), next(it), next(it)
    hs_ref = next(it)

    L = B_CHUNK
    ti = lax.broadcasted_iota(jnp.int32, (L, L), 0)
    si = lax.broadcasted_iota(jnp.int32, (L, L), 1)
    gcol = gcol_ref[...] + brow_ref[...]
    grow = grow_ref[...] + bcol_ref[...]
    qscale = B_DK ** -0.5

    for d in (0, 1):
        valid = (si <= ti) if d == 0 else (si >= ti)
        valid_t = (ti <= si) if d == 0 else (ti >= si)
        last = L - 1 if d == 0 else 0
        if has_init:
            C, n, m = c0_ref[d], n0_ref[d], m0_ref[d][:, :1]
        else:
            C, n, m = jnp.zeros((B_DV, B_DK), F32), jnp.zeros((1, B_DK), F32), jnp.zeros((1, 1), F32)
        for ci in range(n_chunks):
            c = ci if d == 0 else n_chunks - 1 - ci
            rows = slice(c * L, (c + 1) * L)
            q = q_ref[rows, :] * qscale
            k = k_ref[rows, :]
            v = v_ref[rows, :]
            qb, kb = q.astype(BF16), k.astype(BF16)
            ig_col = gcol[rows, 2 * d:2 * d + 1]
            lf_col = _log_sigmoid(gcol[rows, 2 * d + 1:2 * d + 2])
            ig_row = grow[2 * d:2 * d + 1, rows]
            lf_row = _log_sigmoid(grow[2 * d + 1:2 * d + 2, rows])
            b_col = jnp.sum(jnp.where(valid, lf_row, 0.0), axis=1, keepdims=True)
            b_row = jnp.sum(jnp.where(valid_t, lf_col, 0.0), axis=0, keepdims=True)
            dm = jnp.where(valid, b_col - b_row + ig_row, NEG)
            inter = b_col + m
            m_t = jnp.maximum(inter, jnp.max(dm, axis=1, keepdims=True))
            w = jnp.exp(dm - m_t)
            a_in = jnp.exp(inter - m_t)
            sw = _dot_nt(qb, kb) * w
            num = _dot(sw.astype(BF16), v.astype(BF16)) + a_in * _dot_nt(qb, C.astype(BF16))
            den = jnp.sum(sw, axis=1, keepdims=True) + a_in * jnp.sum(q * n, axis=1, keepdims=True)
            h = num / jnp.maximum(jnp.abs(den), jnp.exp(-m_t))
            if d == 0:
                hs_ref[rows, :] = h
            else:
                hs_ref[rows, :] += h
            if emit_state or ci + 1 < n_chunks:
                m_last = m_t[last:last + 1, :]
                al = a_in[last:last + 1, :]
                wl = jnp.exp(b_col[last:last + 1, :] - b_col + ig_col - m_last)
                C = al * C + _dot_tn((v * wl).astype(BF16), kb)
                n = al * n + jnp.sum(wl * k, axis=0, keepdims=True)
                m = m_last
        if emit_state:
            c_out[d] = C
            n_out[d] = n
            m_out[d] = jnp.broadcast_to(m, (1, LANES))

    og = og_ref[...]
    h_ref[...] = (_rms(hs_ref[...], nw_ref[...]) * (1.0 / (1.0 + jnp.exp(-og)))).astype(BF16)


def _mlstm_call(z, gcol, grow, bias, norm_w, seq, batch, row0, init):
    nh = B_HEADS
    has_init = init is not None
    emit_state = not has_init
    bcol = bias.reshape(2, 2, nh).transpose(2, 0, 1).reshape(nh, 4, 1)
    brow = bcol.reshape(nh, 1, 4)
    qk_tiles = B_QK // B_DK
    vd0 = 2 * B_QK // B_DV
    in_specs = [
        pl.BlockSpec((seq, B_DK), lambda b, h: (row0 + b, h)),
        pl.BlockSpec((seq, B_DK), lambda b, h: (row0 + b, qk_tiles + h)),
        pl.BlockSpec((seq, B_DV), lambda b, h: (row0 + b, vd0 + h)),
        pl.BlockSpec((seq, B_DV), lambda b, h: (row0 + b, vd0 + nh + h)),
        pl.BlockSpec((None, seq, 4), lambda b, h: (h, row0 + b, 0)),
        pl.BlockSpec((None, 4, seq), lambda b, h: (h, 0, row0 + b)),
        pl.BlockSpec((None, 4, 1), lambda b, h: (h, 0, 0)),
        pl.BlockSpec((None, 1, 4), lambda b, h: (h, 0, 0)),
        pl.BlockSpec((1, B_DV), lambda b, h: (0, h)),
    ]
    args = [z, z, z, z, gcol, grow, bcol, brow, norm_w.reshape(1, B_VD)]
    state_c = pl.BlockSpec((None, 2, None, B_DV, B_DK), lambda b, h: (b, 0, h, 0, 0))
    state_v = pl.BlockSpec((None, 2, None, 1, B_DK), lambda b, h: (b, 0, h, 0, 0))
    if has_init:
        c0, n0, m0 = init
        in_specs += [state_c, state_v, state_v]
        args += [c0, n0.reshape(batch, 2, nh, 1, B_DK),
                 jnp.broadcast_to(m0[..., None, None], (batch, 2, nh, 1, B_DK))]
    h_shape = jax.ShapeDtypeStruct((batch * seq, B_VD), BF16)
    h_spec = pl.BlockSpec((seq, B_DV), lambda b, h: (b, h))
    if emit_state:
        out_shape = (h_shape,
                     jax.ShapeDtypeStruct((batch, 2, nh, B_DV, B_DK), F32),
                     jax.ShapeDtypeStruct((batch, 2, nh, 1, B_DK), F32),
                     jax.ShapeDtypeStruct((batch, 2, nh, 1, B_DK), F32))
        out_specs = (h_spec, state_c, state_v, state_v)
    else:
        out_shape, out_specs = h_shape, h_spec
    return pl.pallas_call(
        functools.partial(_mlstm_kernel, n_chunks=seq // B_CHUNK, has_init=has_init, emit_state=emit_state),
        out_shape=out_shape,
        grid=(batch, nh),
        in_specs=in_specs, out_specs=out_specs,
        scratch_shapes=[pltpu.VMEM((seq, B_DV), F32)],
        compiler_params=_params(("arbitrary", "arbitrary"), 48),
        name="mlstm_ctx" if emit_state else "mlstm_lat",
    )(*args)


def _mixer_b(z, gates, state_c, state_n, state_m, bias, norm_w):
    nh = B_HEADS
    g = gates[:, :4 * nh].reshape(N_TOK, 4, nh)
    gcol = g.transpose(2, 0, 1)
    grow = g.transpose(2, 1, 0)
    o_p, c_new, n_new, m_new = _mlstm_call(z, gcol, grow, bias, norm_w, SEQ, BATCH, 0, None)
    o_s = _mlstm_call(z, gcol, grow, bias, norm_w, DEC_SEQ, DEC_BATCH, N_PROMPT // DEC_SEQ,
                      (state_c, state_n, state_m))
    return o_p, o_s, c_new, n_new.reshape(BATCH, 2, nh, B_DK), m_new[:, :, :, 0, 0]


def _c_ctx_kernel(q_ref, k_ref, v_ref, o_ref):
    s = _dot_nt(q_ref[...].astype(BF16), k_ref[...].astype(BF16)) * (C_HEAD_DIM ** -0.5)
    e = jnp.exp(s - jnp.max(s, axis=-1, keepdims=True))
    p = e * (1.0 / jnp.sum(e, axis=-1, keepdims=True))
    o_ref[...] = _dot(p.astype(BF16), v_ref[...].astype(BF16)).astype(BF16)


def _na_row_start(r):
    return min(max(r - NA_ROWS // 2, 0), GRID_H - NA_ROWS)


def _c_lat_kernel(bias_ref, q_ref, k_ref, v_ref, pk_ref, pv_ref, o_ref):
    scale = C_HEAD_DIM ** -0.5
    kc = pk_ref[...].astype(BF16)
    vc = pv_ref[...].astype(BF16)
    n_win = NA_ROWS * GRID_W
    for r in range(GRID_H):
        r0 = _na_row_start(r)
        q = q_ref[r * GRID_W:(r + 1) * GRID_W, :].astype(BF16)
        kw = k_ref[r0 * GRID_W:r0 * GRID_W + n_win, :].astype(BF16)
        vw = v_ref[r0 * GRID_W:r0 * GRID_W + n_win, :].astype(BF16)
        s_n = _dot_nt(q, kw) * scale + bias_ref[r - r0]
        s_c = _dot_nt(q, kc) * scale
        m = jnp.maximum(jnp.max(s_n, axis=-1, keepdims=True), jnp.max(s_c, axis=-1, keepdims=True))
        e_n, e_c = jnp.exp(s_n - m), jnp.exp(s_c - m)
        rcp = 1.0 / (jnp.sum(e_n, axis=-1, keepdims=True) + jnp.sum(e_c, axis=-1, keepdims=True))
        o = _dot((e_n * rcp).astype(BF16), vw) + _dot((e_c * rcp).astype(BF16), vc)
        o_ref[r * GRID_W:(r + 1) * GRID_W, :] = o.astype(BF16)


def _na_bias(rpb):
    cq = np.arange(GRID_W)[:, None]
    kc = np.arange(GRID_W)[None, :]
    cstart = np.clip(cq - NA_COLS // 2, 0, GRID_W - NA_COLS)
    col_ok = (kc >= cstart) & (kc < cstart + NA_COLS)
    dcol = np.clip(kc - cq, -(NA_COLS - 1), NA_COLS - 1) + NA_COLS - 1
    off = np.arange(NA_ROWS)[:, None]
    drow = np.arange(NA_ROWS)[None, :] - off + NA_ROWS - 1
    tab = rpb.astype(F32)[:, drow[:, :, None, None], dcol[None, None, :, :]]
    tab = jnp.where(col_ok[None, None, None], tab, NEG)
    return tab.transpose(0, 1, 3, 2, 4).reshape(C_HEADS, NA_ROWS, GRID_W, NA_ROWS * GRID_W)


def _mixer_c(qkv, cache_k, cache_v, rpb):
    nh = C_HEADS
    o_p = pl.pallas_call(
        _c_ctx_kernel,
        out_shape=jax.ShapeDtypeStruct((N_PROMPT, D_MODEL), BF16),
        grid=(BATCH, nh),
        in_specs=[pl.BlockSpec((SEQ, LANES), lambda b, h: (b, h)),
                  pl.BlockSpec((SEQ, LANES), lambda b, h: (b, nh + h)),
                  pl.BlockSpec((SEQ, LANES), lambda b, h: (b, 2 * nh + h))],
        out_specs=pl.BlockSpec((SEQ, LANES), lambda b, h: (b, h)),
        compiler_params=_params(("arbitrary", "arbitrary"), 32),
        name="na_ctx",
    )(qkv, qkv, qkv)

    row0 = N_PROMPT // DEC_SEQ
    past = pl.BlockSpec((PAST_LEN, LANES), lambda h, b: (b, h))
    o_s = pl.pallas_call(
        _c_lat_kernel,
        out_shape=jax.ShapeDtypeStruct((N_SAMPLE, D_MODEL), BF16),
        grid=(nh, DEC_BATCH),
        in_specs=[pl.BlockSpec((None, NA_ROWS, GRID_W, NA_ROWS * GRID_W), lambda h, b: (h, 0, 0, 0)),
                  pl.BlockSpec((DEC_SEQ, LANES), lambda h, b: (row0 + b, h)),
                  pl.BlockSpec((DEC_SEQ, LANES), lambda h, b: (row0 + b, nh + h)),
                  pl.BlockSpec((DEC_SEQ, LANES), lambda h, b: (row0 + b, 2 * nh + h)),
                  past, past],
        out_specs=pl.BlockSpec((DEC_SEQ, LANES), lambda h, b: (b, h)),
        compiler_params=_params(("arbitrary", "arbitrary"), 40),
        name="na_lat",
    )(_na_bias(rpb), qkv, qkv, qkv,
      cache_k.reshape(DEC_BATCH * PAST_LEN, D_MODEL), cache_v.reshape(DEC_BATCH * PAST_LEN, D_MODEL))
    return o_p, o_s


def _gqa_core(q, keys, vals, masks, sink_ref, pair, o_ref):
    tq = q.shape[0]
    lo = _lane_lo((tq, LANES))
    scale = D_HEAD_DIM ** -0.5
    pieces = {}
    for kv in (0, 1):
        keep = lo if kv == 0 else ~lo
        stack, sinks = [], []
        for tile in (2 * kv, 2 * kv + 1):
            qt = q[:, tile * LANES:(tile + 1) * LANES]
            for e in (0, 1):
                qe = qt if e == kv else pltpu.roll(qt, HALF, axis=1)
                stack.append(jnp.where(keep, qe, 0.0))
                sk = sink_ref[pair * 2 * D_GROUP + tile * 2 + e]
                sinks.append(jnp.full((tq, 1), sk, F32))
        qs = jnp.concatenate(stack, axis=0).astype(BF16)
        sk = jnp.concatenate(sinks, axis=0)
        scores = []
        for kt, mk in zip(keys, masks):
            s = _dot_nt(qs, kt) * scale
            if mk is not None:
                s = s + jnp.concatenate([mk] * D_GROUP, axis=0)
            scores.append(s)
        m = sk
        for s in scores:
            m = jnp.maximum(m, jnp.max(s, axis=-1, keepdims=True))
        es = [jnp.exp(s - m) for s in scores]
        den = jnp.exp(sk - m)
        for e_ in es:
            den = den + jnp.sum(e_, axis=-1, keepdims=True)
        rcp = 1.0 / den
        o = None
        for e_, vt in zip(es, vals):
            part = _dot((e_ * rcp).astype(BF16), vt)
            o = part if o is None else o + part
        for i, tile in enumerate((2 * kv, 2 * kv + 1)):
            for e in (0, 1):
                piece = o[(2 * i + e) * tq:(2 * i + e + 1) * tq, :]
                pieces[(tile, e)] = piece if e == kv else pltpu.roll(piece, HALF, axis=1)
    for tile in range(4):
        o_ref[:, tile * LANES:(tile + 1) * LANES] = jnp.where(lo, pieces[(tile, 0)], pieces[(tile, 1)]).astype(BF16)


def _d_ctx_kernel(sink_ref, q_ref, k_ref, v_ref, o_ref):
    _gqa_core(q_ref[...], [k_ref[...].astype(BF16)], [v_ref[...].astype(BF16)], [None],
              sink_ref, pl.program_id(1), o_ref)


D_QBLOCK = 128
D_BAND = 3 * D_QBLOCK


def _d_lat_kernel(sink_ref, cq_ref, nq_ref, pq_ref, ck_ref, nk_ref, pk_ref,
                  q_ref, k_ref, v_ref, pkc_ref, pvc_ref, o_ref):
    bi = pl.program_id(2)
    start = pl.multiple_of(jnp.clip((bi - 1) * D_QBLOCK, 0, DEC_SEQ - D_BAND), D_QBLOCK)
    band = pl.ds(start, D_BAND)
    cq, nq, pq = cq_ref[...], nq_ref[...], pq_ref[...]
    q = jnp.concatenate([_rope(q_ref[:, t * LANES:(t + 1) * LANES], cq, nq, pq) for t in range(4)], axis=1)
    k = _rope(k_ref[band, :], ck_ref[band, :], nk_ref[band, :], pk_ref[band, :]).astype(BF16)
    v = v_ref[band, :].astype(BF16)
    qpos = bi * D_QBLOCK + lax.broadcasted_iota(jnp.int32, (D_QBLOCK, D_BAND), 0)
    kpos = start + lax.broadcasted_iota(jnp.int32, (D_QBLOCK, D_BAND), 1)
    mask = jnp.where(jnp.abs(qpos - kpos) <= WINDOW, 0.0, NEG)
    _gqa_core(q, [k, pkc_ref[...].astype(BF16)], [v, pvc_ref[...].astype(BF16)], [mask, None],
              sink_ref, pl.program_id(1), o_ref)


def _mixer_d(qkv, cache_k, cache_v, sink, rope):
    n_pairs = D_KV_HEADS // 2
    qw = 2 * D_GROUP * D_HEAD_DIM
    k0 = D_HEADS * D_HEAD_DIM // LANES
    v0 = k0 + n_pairs
    smem = pl.BlockSpec(memory_space=pltpu.SMEM)
    sink = sink.astype(F32)
    o_p = pl.pallas_call(
        _d_ctx_kernel,
        out_shape=jax.ShapeDtypeStruct((N_PROMPT, D_MODEL), BF16),
        grid=(BATCH, n_pairs),
        in_specs=[smem,
                  pl.BlockSpec((SEQ, qw), lambda b, p: (b, p)),
                  pl.BlockSpec((SEQ, LANES), lambda b, p: (b, k0 + p)),
                  pl.BlockSpec((SEQ, LANES), lambda b, p: (b, v0 + p))],
        out_specs=pl.BlockSpec((SEQ, qw), lambda b, p: (b, p)),
        compiler_params=_params(("arbitrary", "arbitrary"), 40),
        name="gqa_ctx",
    )(sink, qkv, qkv, qkv)

    nqb = DEC_SEQ // D_QBLOCK
    row0 = N_PROMPT // DEC_SEQ
    cos, s_next, s_prev = rope
    tab_q = pl.BlockSpec((D_QBLOCK, LANES), lambda b, p, i: (i, 0))
    tab_k = pl.BlockSpec((DEC_SEQ, LANES), lambda b, p, i: (0, 0))
    past = pl.BlockSpec((PAST_LEN, LANES), lambda b, p, i: (b, p))
    kvw = D_KV_HEADS * D_HEAD_DIM
    o_s = pl.pallas_call(
        _d_lat_kernel,
        out_shape=jax.ShapeDtypeStruct((N_SAMPLE, D_MODEL), BF16),
        grid=(DEC_BATCH, n_pairs, nqb),
        in_specs=[smem, tab_q, tab_q, tab_q, tab_k, tab_k, tab_k,
                  pl.BlockSpec((D_QBLOCK, qw), lambda b, p, i: ((row0 + b) * nqb + i, p)),
                  pl.BlockSpec((DEC_SEQ, LANES), lambda b, p, i: (row0 + b, k0 + p)),
                  pl.BlockSpec((DEC_SEQ, LANES), lambda b, p, i: (row0 + b, v0 + p)),
                  past, past],
        out_specs=pl.BlockSpec((D_QBLOCK, qw), lambda b, p, i: (b * nqb + i, p)),
        compiler_params=_params(("arbitrary", "arbitrary", "arbitrary"), 40),
        name="gqa_lat",
    )(sink, cos, s_next, s_prev, cos, s_next, s_prev, qkv, qkv, qkv,
      cache_k.reshape(DEC_BATCH * PAST_LEN, kvw), cache_v.reshape(DEC_BATCH * PAST_LEN, kvw))
    return o_p, o_s


def kernel(x_prompt, x_sample, cache_a_k, cache_a_v, state_b_C, state_b_n, state_b_m, cache_c_k, cache_c_v,
           cache_d_k, cache_d_v, c, c_ctx, w_mod, b_mod, g_norm, w_ff1, w_ff2, a_w_in, a_w_out, a_lambda,
           a_subln, b_w_in, b_gate_bias, b_w_out, b_norm, c_w_in, c_w_out, c_rpb, d_w_in, d_w_out, d_sink):
    x = jnp.concatenate([x_prompt.reshape(N_PROMPT, D_MODEL), x_sample.reshape(N_SAMPLE, D_MODEL)], axis=0)
    cond = jnp.concatenate([c_ctx[None, :], c, jnp.zeros((N_COND - 1 - DEC_BATCH, D_MODEL), F32)], axis=0)
    mod = _modulation(cond, w_mod, b_mod).reshape(DEPTH, N_COND, N_MOD, 1, D_MODEL)
    gains = g_norm.reshape(DEPTH, 4, 1, D_MODEL)
    rope = _rope_tables()
    new = {}

    h = _adaln(x, gains, mod, 0)
    for i in range(DEPTH):
        kind, j = i % N_MIXERS, i // N_MIXERS
        if kind == 0:
            lam_init = 0.8 - 0.6 * math.exp(-0.3 * i)
            qkv = _project(h, a_w_in[j])
            o_p, o_s = _mixer_a(qkv, cache_a_k[:, j], cache_a_v[:, j], a_lambda[j], a_subln[j], lam_init, rope)
            w_out = a_w_out[j]
            kv = qkv[:N_PROMPT].reshape(BATCH, SEQ, 3, A_HEADS, A_V_DIM)
            new.setdefault("a_k", []).append(kv[:, :, 1])
            new.setdefault("a_v", []).append(kv[:, :, 2])
        elif kind == 1:
            w_in = b_w_in[j]
            z = _project(h, w_in[:, :B_MAIN])
            w_gate = jnp.pad(w_in[:, B_MAIN:], ((0, 0), (0, LANES - 4 * B_HEADS)))
            gates = _project(h, w_gate)
            o_p, o_s, c_new, n_new, m_new = _mixer_b(z, gates, state_b_C[:, j], state_b_n[:, j], state_b_m[:, j],
                                                     b_gate_bias[j], b_norm[j])
            w_out = b_w_out[j]
            new.setdefault("b_C", []).append(c_new)
            new.setdefault("b_n", []).append(n_new)
            new.setdefault("b_m", []).append(m_new)
        elif kind == 2:
            qkv = _project(h, c_w_in[j])
            o_p, o_s = _mixer_c(qkv, cache_c_k[:, j], cache_c_v[:, j], c_rpb[j])
            w_out = c_w_out[j]
            kv = qkv[:N_PROMPT].reshape(BATCH, SEQ, 3, C_HEADS, C_HEAD_DIM)
            new.setdefault("c_k", []).append(kv[:, :, 1])
            new.setdefault("c_v", []).append(kv[:, :, 2])
        else:
            qkv = _project(h, d_w_in[j])
            o_p, o_s = _mixer_d(qkv, cache_d_k[:, j], cache_d_v[:, j], d_sink[j], rope)
            w_out = d_w_out[j]
            nq = D_HEADS * D_HEAD_DIM
            nkv = D_KV_HEADS * D_HEAD_DIM
            new.setdefault("d_k", []).append(qkv[:N_PROMPT, nq:nq + nkv].reshape(BATCH, SEQ, D_KV_HEADS, D_HEAD_DIM))
            new.setdefault("d_v", []).append(qkv[:N_PROMPT, nq + nkv:].reshape(BATCH, SEQ, D_KV_HEADS, D_HEAD_DIM))
        y = _project(jnp.concatenate([o_p, o_s], axis=0), w_out)
        x, h = _residual(x, y, gains, mod, i, 2, 1, (i, 2, 3, 4))
        y = _mlp(h, w_ff1[i].astype(BF16), w_ff2[i].astype(BF16))
        x, h = _residual(x, y, gains, mod, i, 5, 3, (i + 1, 0, 0, 1) if i + 1 < DEPTH else None)

    stack = lambda name: jnp.stack(new[name], axis=1)
    return (x[:N_PROMPT].reshape(BATCH, SEQ, D_MODEL), x[N_PROMPT:].reshape(DEC_BATCH, DEC_SEQ, D_MODEL),
            stack("a_k"), stack("a_v"), stack("b_C"), stack("b_n"), stack("b_m"),
            stack("c_k"), stack("c_v"), stack("d_k"), stack("d_v"))
```

```python
import functools
import math

import numpy as np
import jax
import jax.numpy as jnp
from jax import lax
from jax.experimental import pallas as pl
from jax.experimental.pallas import tpu as pltpu

D_MODEL = 2048
BATCH = 16
SEQ = 256
DEPTH = 4
DEC_BATCH = 4
DEC_SEQ = 1024
PAST_LEN = 512
GRID_W = 64
N_MIXERS = 4
D_FF = 4 * D_MODEL
N_MOD = 6
RMS_EPS = 1e-6
ROPE_THETA = 10000.0

A_HEADS = 16
A_HALF_DIM = 64
A_V_DIM = 128
B_HEADS = 8
B_DK = 128
B_DV = 256
B_QK = B_HEADS * B_DK
B_VD = B_HEADS * B_DV
B_MAIN = 2 * B_QK + 2 * B_VD
C_HEADS = 16
C_HEAD_DIM = 128
NA_ROWS = 8
NA_COLS = 16
GRID_H = DEC_SEQ // GRID_W
D_HEADS = 32
D_KV_HEADS = 8
D_GROUP = 4
D_HEAD_DIM = 64
WINDOW = 128

N_PROMPT = BATCH * SEQ
N_SAMPLE = DEC_BATCH * DEC_SEQ
N_TOK = N_PROMPT + N_SAMPLE
N_COND = 8
SAMPLE_ROW0 = N_PROMPT // DEC_SEQ

LANES = 128
HALF = LANES // 2
NEG = -1e30
MIB = 1024 * 1024
F32 = jnp.float32
BF16 = jnp.bfloat16


def _params(sem, vmem_mib):
    return pltpu.CompilerParams(dimension_semantics=sem, vmem_limit_bytes=vmem_mib * MIB)


def _dot(a, b):
    return jnp.dot(a, b, preferred_element_type=F32)


def _dot_nt(a, b):
    return lax.dot_general(a, b, (((1,), (1,)), ((), ())), preferred_element_type=F32)


def _dot_tn(a, b):
    return lax.dot_general(a, b, (((0,), (0,)), ((), ())), preferred_element_type=F32)


def _rms(x, g):
    return x * lax.rsqrt(jnp.mean(x * x, axis=-1, keepdims=True) + RMS_EPS) * g


def _cond_index(row0):
    return jnp.where(row0 < N_PROMPT, 0, 1 + (row0 - N_PROMPT) // DEC_SEQ)


def _mod_kernel(c_ref, w_ref, b_ref, o_ref):
    c = c_ref[...]
    s = (c / (1.0 + jnp.exp(-c))).astype(BF16)
    o_ref[...] = _dot(s, w_ref[...].astype(BF16)) + b_ref[...]


def _modulation(cond, w_mod, b_mod):
    tn = 1024
    n = N_MOD * D_MODEL
    return pl.pallas_call(
        _mod_kernel,
        out_shape=jax.ShapeDtypeStruct((DEPTH, N_COND, n), F32),
        grid=(DEPTH, n // tn),
        in_specs=[
            pl.BlockSpec((N_COND, D_MODEL), lambda l, j: (0, 0)),
            pl.BlockSpec((None, D_MODEL, tn), lambda l, j: (l, 0, j)),
            pl.BlockSpec((None, 1, tn), lambda l, j: (l, 0, j)),
        ],
        out_specs=pl.BlockSpec((None, N_COND, tn), lambda l, j: (l, 0, j)),
        compiler_params=_params(("arbitrary", "arbitrary"), 40),
        name="modulation",
    )(cond, w_mod, b_mod.reshape(DEPTH, 1, n))


ROW_TILE = 256


def _mod_spec(layer, which, tile0=0):
    return pl.BlockSpec((None, None, None, 1, D_MODEL),
                        lambda i: (layer, _cond_index((i + tile0) * ROW_TILE), which, 0, 0))


def _gain_spec(layer, which):
    return pl.BlockSpec((None, None, 1, D_MODEL), lambda i: (layer, which, 0, 0))


def _adaln_kernel(x_ref, g_ref, shift_ref, scale_ref, h_ref):
    h = _rms(x_ref[...], g_ref[...]) * (1.0 + scale_ref[...]) + shift_ref[...]
    h_ref[...] = h.astype(BF16)


def _adaln(x, gains, mod, layer):
    row = pl.BlockSpec((ROW_TILE, D_MODEL), lambda i: (i, 0))
    return pl.pallas_call(
        _adaln_kernel,
        out_shape=jax.ShapeDtypeStruct((N_TOK, D_MODEL), BF16),
        grid=(N_TOK // ROW_TILE,),
        in_specs=[row, _gain_spec(layer, 0), _mod_spec(layer, 0), _mod_spec(layer, 1)],
        out_specs=row,
        compiler_params=_params(("arbitrary",), 32),
        name="adaln",
    )(x, gains, mod, mod)


def _residual_adaln_kernel(x_ref, y_ref, gate_ref, gpost_ref, gpre_ref, shift_ref, scale_ref, xo_ref, h_ref):
    x = x_ref[...] + gate_ref[...] * _rms(y_ref[...], gpost_ref[...])
    xo_ref[...] = x
    h = _rms(x, gpre_ref[...]) * (1.0 + scale_ref[...]) + shift_ref[...]
    h_ref[...] = h.astype(BF16)


def _residual_kernel(x_ref, y_ref, gate_ref, gpost_ref, xo_ref):
    xo_ref[...] = x_ref[...] + gate_ref[...] * _rms(y_ref[...], gpost_ref[...])


def _residual(x, y, gains, mod, layer, gate_idx, post_idx, nxt):
    row = pl.BlockSpec((ROW_TILE, D_MODEL), lambda i: (i, 0))
    if nxt is None:
        outs = []
        for tile0, n_rows in ((0, N_PROMPT), (N_PROMPT // ROW_TILE, N_SAMPLE)):
            src = pl.BlockSpec((ROW_TILE, D_MODEL), lambda i, t=tile0: (i + t, 0))
            outs.append(pl.pallas_call(
                _residual_kernel, out_shape=jax.ShapeDtypeStruct((n_rows, D_MODEL), F32),
                grid=(n_rows // ROW_TILE,),
                in_specs=[src, src, _mod_spec(layer, gate_idx, tile0), _gain_spec(layer, post_idx)],
                out_specs=row,
                compiler_params=_params(("arbitrary",), 32), name="residual",
            )(x, y, mod, gains))
        return outs, None
    nl, ng, nshift, nscale = nxt
    return pl.pallas_call(
        _residual_adaln_kernel,
        out_shape=(jax.ShapeDtypeStruct((N_TOK, D_MODEL), F32), jax.ShapeDtypeStruct((N_TOK, D_MODEL), BF16)),
        grid=(N_TOK // ROW_TILE,),
        in_specs=[row, row, _mod_spec(layer, gate_idx), _gain_spec(layer, post_idx),
                  _gain_spec(nl, ng), _mod_spec(nl, nshift), _mod_spec(nl, nscale)],
        out_specs=(row, row),
        compiler_params=_params(("arbitrary",), 40), name="residual_adaln",
    )(x, y, mod, gains, gains, mod, mod)


PROJ_TM = 1024
PROMPT_TILES = N_PROMPT // PROJ_TM


def _proj_kernel(a_ref, w_ref, o_ref, wbf_ref):
    @pl.when(pl.program_id(1) == 0)
    def _():
        wbf_ref[...] = w_ref[...].astype(BF16)

    o_ref[...] = _dot(a_ref[...], wbf_ref[...]).astype(o_ref.dtype)


def _proj2_kernel(ap_ref, as_ref, w_ref, o_ref, wbf_ref):
    i = pl.program_id(1)

    @pl.when(i == 0)
    def _():
        wbf_ref[...] = w_ref[...].astype(BF16)

    @pl.when(i < PROMPT_TILES)
    def _():
        o_ref[...] = _dot(ap_ref[...], wbf_ref[...]).astype(o_ref.dtype)

    @pl.when(i >= PROMPT_TILES)
    def _():
        o_ref[...] = _dot(as_ref[...], wbf_ref[...]).astype(o_ref.dtype)


def _project(a, w, n=None, out_dtype=F32):
    k = w.shape[0]
    n = w.shape[1] if n is None else n
    tm = PROJ_TM
    tn = min(n, 1024)
    w_spec = pl.BlockSpec((k, tn), lambda j, i: (0, j))
    if isinstance(a, tuple):
        body, args = _proj2_kernel, (*a, w)
        last_p = PROMPT_TILES - 1
        a_specs = [pl.BlockSpec((tm, k), lambda j, i: (jnp.minimum(i, last_p), 0)),
                   pl.BlockSpec((tm, k), lambda j, i: (jnp.maximum(i - PROMPT_TILES, 0), 0))]
    else:
        body, args = _proj_kernel, (a, w)
        a_specs = [pl.BlockSpec((tm, k), lambda j, i: (i, 0))]
    return pl.pallas_call(
        body,
        out_shape=jax.ShapeDtypeStruct((N_TOK, n), out_dtype),
        grid=(n // tn, N_TOK // tm),
        in_specs=a_specs + [w_spec],
        out_specs=pl.BlockSpec((tm, tn), lambda j, i: (i, j)),
        scratch_shapes=[pltpu.VMEM((k, tn), BF16)],
        compiler_params=_params(("arbitrary", "arbitrary"), 48),
        name="project",
    )(*args)


def _mlp_kernel(h_ref, w1_ref, w2_ref, o_ref):
    @pl.when(pl.program_id(1) == 0)
    def _():
        o_ref[...] = jnp.zeros_like(o_ref)

    u = jnp.maximum(_dot(h_ref[...], w1_ref[...].astype(BF16)), 0.0)
    o_ref[...] += _dot((u * u).astype(BF16), w2_ref[...].astype(BF16))


def _mlp(h, w1, w2):
    tm, tf = 1024, 512
    return pl.pallas_call(
        _mlp_kernel,
        out_shape=jax.ShapeDtypeStruct((N_TOK, D_MODEL), F32),
        grid=(N_TOK // tm, D_FF // tf),
        in_specs=[pl.BlockSpec((tm, D_MODEL), lambda i, f: (i, 0)),
                  pl.BlockSpec((D_MODEL, tf), lambda i, f: (0, f)),
                  pl.BlockSpec((tf, D_MODEL), lambda i, f: (f, 0))],
        out_specs=pl.BlockSpec((tm, D_MODEL), lambda i, f: (i, 0)),
        compiler_params=_params(("arbitrary", "arbitrary"), 56),
        name="mlp",
    )(h, w1, w2)


def _rope_tables():
    t = jnp.arange(DEC_SEQ)
    lane = np.arange(LANES)
    f = lane % 32
    first = f < 16
    inv = ROPE_THETA ** (-jnp.arange(16, dtype=F32) / 16)
    pos = jnp.where((lane % 64 < 32)[None, :], (t // GRID_W)[:, None], (t % GRID_W)[:, None]).astype(F32)
    ang = pos * inv[f % 16][None, :]
    cos, sin = jnp.cos(ang), jnp.sin(ang)
    sin_next = jnp.where(first[None, :], -sin, 0.0)
    sin_prev = jnp.where(first[None, :], 0.0, sin)
    return cos, sin_next, sin_prev


def _rope(x, cos, sin_next, sin_prev):
    return (x * cos + pltpu.roll(x, LANES - 16, axis=1) * sin_next
            + pltpu.roll(x, 16, axis=1) * sin_prev)


def _lane_lo(shape):
    return lax.broadcasted_iota(jnp.int32, shape, 1) < HALF


def _with_ones(v):
    return jnp.concatenate([v.astype(BF16), jnp.ones(v.shape, BF16)], axis=1)


def _attend(scores, vals_ones, sink=None):
    m = None
    for s in scores:
        mi = jnp.max(s, axis=-1, keepdims=True)
        m = mi if m is None else jnp.maximum(m, mi)
    if sink is not None:
        m = jnp.maximum(m, sink)
    acc = None
    for s, v in zip(scores, vals_ones):
        part = _dot(jnp.exp(s - m).astype(BF16), v)
        acc = part if acc is None else acc + part
    den = acc[:, LANES:]
    if sink is not None:
        den = den + jnp.exp(sink - m)
    return acc[:, :LANES] / den


CTX_HEADS = 4


A_SCALE = A_HALF_DIM ** -0.5


def _diff_lambda(lp, lam_init):
    a = jnp.sum(lp[0] * lp[1], axis=-1, keepdims=True)
    b = jnp.sum(lp[2] * lp[3], axis=-1, keepdims=True)
    return jnp.exp(a) - jnp.exp(b) + lam_init


def _diff_finish(o, subln_ref, lam_init):
    return (_rms(o, subln_ref[...]) * (1.0 - lam_init)).astype(BF16)


def _a_ctx_kernel(lam_ref, subln_ref, q_ref, k_ref, v_ref, o_ref, ko_ref, vo_ref, *, lam_init):
    lo = _lane_lo((SEQ, LANES))
    for hh in range(CTX_HEADS):
        cols = slice(hh * LANES, (hh + 1) * LANES)
        q = q_ref[:, cols] * A_SCALE
        k = k_ref[:, cols]
        v = v_ref[:, cols]
        ko_ref[:, cols] = k
        vo_ref[:, cols] = v
        kb, v1 = k.astype(BF16), _with_ones(v)
        o1 = _attend([_dot_nt(jnp.where(lo, q, 0.0).astype(BF16), kb)], [v1])
        o2 = _attend([_dot_nt(jnp.where(lo, 0.0, q).astype(BF16), kb)], [v1])
        lam = _diff_lambda(lam_ref[:, hh], lam_init)
        o_ref[:, cols] = _diff_finish(o1 - lam * o2, subln_ref, lam_init)


A_LAT_TQ = 256


def _a_lat_kernel(lam_ref, subln_ref, cos_ref, nxt_ref, prv_ref, q_ref, k_ref, v_ref, pk_ref, pv_ref, o_ref,
                  kb_ref, v1_ref, pkb_ref, pv1_ref, *, lam_init):
    kb_ref[...] = _rope(k_ref[...], cos_ref[...], nxt_ref[...], prv_ref[...]).astype(BF16)
    v1_ref[...] = _with_ones(v_ref[...])
    pkb_ref[...] = pk_ref[...].astype(BF16)
    pv1_ref[...] = _with_ones(pv_ref[...])
    lam = _diff_lambda(lam_ref[...], lam_init)
    lo = _lane_lo((A_LAT_TQ, LANES))

    def block(i, carry):
        rows = pl.ds(pl.multiple_of(i * A_LAT_TQ, A_LAT_TQ), A_LAT_TQ)
        q = _rope(q_ref[rows, :], cos_ref[rows, :], nxt_ref[rows, :], prv_ref[rows, :]) * A_SCALE
        outs = []
        for keep in (lo, ~lo):
            qh = jnp.where(keep, q, 0.0).astype(BF16)
            outs.append(_attend([_dot_nt(qh, pkb_ref[...]), _dot_nt(qh, kb_ref[...])], [pv1_ref[...], v1_ref[...]]))
        o_ref[rows, :] = _diff_finish(outs[0] - lam * outs[1], subln_ref, lam_init)
        return carry

    lax.fori_loop(0, DEC_SEQ // A_LAT_TQ, block, 0)


def _mixer_a(qkv, cache_k, cache_v, lam_p, subln, lam_init, rope):
    nh = A_HEADS
    lam4 = lam_p.reshape(4, nh, 1, A_HALF_DIM)
    sub2 = subln.reshape(1, A_V_DIM)
    cw = CTX_HEADS * LANES
    nt = nh // CTX_HEADS
    ctx_out = pl.BlockSpec((SEQ, cw), lambda b, h: (b, h))
    kv_shape = jax.ShapeDtypeStruct((N_PROMPT, nh * A_V_DIM), F32)
    o_p, k_new, v_new = pl.pallas_call(
        functools.partial(_a_ctx_kernel, lam_init=lam_init),
        out_shape=(jax.ShapeDtypeStruct((N_PROMPT, nh * A_V_DIM), BF16), kv_shape, kv_shape),
        grid=(BATCH, nt),
        in_specs=[pl.BlockSpec((4, CTX_HEADS, 1, A_HALF_DIM), lambda b, h: (0, h, 0, 0)),
                  pl.BlockSpec((1, A_V_DIM), lambda b, h: (0, 0)),
                  pl.BlockSpec((SEQ, cw), lambda b, h: (b, h)),
                  pl.BlockSpec((SEQ, cw), lambda b, h: (b, nt + h)),
                  pl.BlockSpec((SEQ, cw), lambda b, h: (b, 2 * nt + h))],
        out_specs=(ctx_out, ctx_out, ctx_out),
        compiler_params=_params(("arbitrary", "arbitrary"), 32),
        name="diff_attn_ctx",
    )(lam4, sub2, qkv, qkv, qkv)

    table = pl.BlockSpec((DEC_SEQ, LANES), lambda b, h: (0, 0))
    past = pl.BlockSpec((PAST_LEN, LANES), lambda b, h: (b, h))
    o_s = pl.pallas_call(
        functools.partial(_a_lat_kernel, lam_init=lam_init),
        out_shape=jax.ShapeDtypeStruct((N_SAMPLE, nh * A_V_DIM), BF16),
        grid=(DEC_BATCH, nh),
        in_specs=[pl.BlockSpec((4, None, 1, A_HALF_DIM), lambda b, h: (0, h, 0, 0)),
                  pl.BlockSpec((1, A_V_DIM), lambda b, h: (0, 0)),
                  table, table, table,
                  pl.BlockSpec((DEC_SEQ, LANES), lambda b, h: (SAMPLE_ROW0 + b, h)),
                  pl.BlockSpec((DEC_SEQ, LANES), lambda b, h: (SAMPLE_ROW0 + b, nh + h)),
                  pl.BlockSpec((DEC_SEQ, LANES), lambda b, h: (SAMPLE_ROW0 + b, 2 * nh + h)),
                  past, past],
        out_specs=pl.BlockSpec((DEC_SEQ, LANES), lambda b, h: (b, h)),
        scratch_shapes=[pltpu.VMEM((DEC_SEQ, LANES), BF16), pltpu.VMEM((DEC_SEQ, 2 * LANES), BF16),
                        pltpu.VMEM((PAST_LEN, LANES), BF16), pltpu.VMEM((PAST_LEN, 2 * LANES), BF16)],
        compiler_params=_params(("arbitrary", "arbitrary"), 40),
        name="diff_attn_lat",
    )(lam4, sub2, *rope, qkv, qkv, qkv,
      cache_k.reshape(DEC_BATCH * PAST_LEN, nh * A_V_DIM), cache_v.reshape(DEC_BATCH * PAST_LEN, nh * A_V_DIM))
    return o_p, o_s, k_new, v_new


B_CHUNK = 256


def _log_sigmoid(x):
    return jnp.minimum(x, 0.0) - jnp.log(1.0 + jnp.exp(-jnp.abs(x)))


def _mlstm_kernel(*refs, n_chunks, has_init, emit_state):
    it = iter(refs)
    q_ref, k_ref, v_ref, og_ref, gcol_ref, grow_ref, bcol_ref, brow_ref, nw_ref = (next(it) for _ in range(9))
    if has_init:
        c0_ref, n0_ref, m0_ref = next(it), next(it), next(it)
    h_ref = next(it)
    if emit_state:
        c_out, n_out, m_out = next(it), next(it), next(it)
    hs_ref = next(it)

    L = B_CHUNK
    ti = lax.broadcasted_iota(jnp.int32, (L, L), 0)
    si = lax.broadcasted_iota(jnp.int32, (L, L), 1)
    gcol = gcol_ref[...] + brow_ref[...]
    grow = grow_ref[...] + bcol_ref[...]
    qscale = B_DK ** -0.5

    for d in (0, 1):
        valid = (si <= ti) if d == 0 else (si >= ti)
        valid_t = (ti <= si) if d == 0 else (ti >= si)
        last = L - 1 if d == 0 else 0
        if has_init:
            C, n, m = c0_ref[d], n0_ref[d], m0_ref[d][:, :1]
        else:
            C, n, m = jnp.zeros((B_DV, B_DK), F32), jnp.zeros((1, B_DK), F32), jnp.zeros((1, 1), F32)
        for ci in range(n_chunks):
            c = ci if d == 0 else n_chunks - 1 - ci
            rows = slice(c * L, (c + 1) * L)
            q = q_ref[rows, :] * qscale
            k = k_ref[rows, :]
            v = v_ref[rows, :]
            qb, kb = q.astype(BF16), k.astype(BF16)
            ig_col = gcol[rows, 2 * d:2 * d + 1]
            lf_col = _log_sigmoid(gcol[rows, 2 * d + 1:2 * d + 2])
            ig_row = grow[2 * d:2 * d + 1, rows]
            lf_row = _log_sigmoid(grow[2 * d + 1:2 * d + 2, rows])
            b_col = jnp.sum(jnp.where(valid, lf_row, 0.0), axis=1, keepdims=True)
            b_row = jnp.sum(jnp.where(valid_t, lf_col, 0.0), axis=0, keepdims=True)
            dm = jnp.where(valid, b_col - b_row + ig_row, NEG)
            inter = b_col + m
            m_t = jnp.maximum(inter, jnp.max(dm, axis=1, keepdims=True))
            w = jnp.exp(dm - m_t)
            a_in = jnp.exp(inter - m_t)
            sw = _dot_nt(qb, kb) * w
            num = _dot(sw.astype(BF16), v.astype(BF16)) + a_in * _dot_nt(qb, C.astype(BF16))
            den = jnp.sum(sw, axis=1, keepdims=True) + a_in * jnp.sum(q * n, axis=1, keepdims=True)
            h = num / jnp.maximum(jnp.abs(den), jnp.exp(-m_t))
            if d == 0:
                hs_ref[rows, :] = h
            else:
                hs_ref[rows, :] += h
            if emit_state or ci + 1 < n_chunks:
                m_last = m_t[last:last + 1, :]
                al = a_in[last:last + 1, :]
                wl = jnp.exp(b_col[last:last + 1, :] - b_col + ig_col - m_last)
                C = al * C + _dot_tn((v * wl).astype(BF16), kb)
                n = al * n + jnp.sum(wl * k, axis=0, keepdims=True)
                m = m_last
        if emit_state:
            c_out[d] = C
            n_out[d] = n
            m_out[d] = jnp.broadcast_to(m, (1, LANES))

    og = og_ref[...]
    h_ref[...] = (_rms(hs_ref[...], nw_ref[...]) * (1.0 / (1.0 + jnp.exp(-og)))).astype(BF16)


def _mlstm_call(z, gcol, grow, bias, norm_w, seq, batch, row0, init):
    nh = B_HEADS
    has_init = init is not None
    emit_state = not has_init
    bcol = bias.reshape(2, 2, nh).transpose(2, 0, 1).reshape(nh, 4, 1)
    brow = bcol.reshape(nh, 1, 4)
    qk_tiles = B_QK // B_DK
    vd0 = 2 * B_QK // B_DV
    in_specs = [
        pl.BlockSpec((seq, B_DK), lambda b, h: (row0 + b, h)),
        pl.BlockSpec((seq, B_DK), lambda b, h: (row0 + b, qk_tiles + h)),
        pl.BlockSpec((seq, B_DV), lambda b, h: (row0 + b, vd0 + h)),
        pl.BlockSpec((seq, B_DV), lambda b, h: (row0 + b, vd0 + nh + h)),
        pl.BlockSpec((None, seq, 4), lambda b, h: (h, row0 + b, 0)),
        pl.BlockSpec((None, 4, seq), lambda b, h: (h, 0, row0 + b)),
        pl.BlockSpec((None, 4, 1), lambda b, h: (h, 0, 0)),
        pl.BlockSpec((None, 1, 4), lambda b, h: (h, 0, 0)),
        pl.BlockSpec((1, B_DV), lambda b, h: (0, h)),
    ]
    args = [z, z, z, z, gcol, grow, bcol, brow, norm_w.reshape(1, B_VD)]
    state_c = pl.BlockSpec((None, 2, None, B_DV, B_DK), lambda b, h: (b, 0, h, 0, 0))
    state_v = pl.BlockSpec((None, 2, None, 1, B_DK), lambda b, h: (b, 0, h, 0, 0))
    if has_init:
        c0, n0, m0 = init
        in_specs += [state_c, state_v, state_v]
        args += [c0, n0.reshape(batch, 2, nh, 1, B_DK),
                 jnp.broadcast_to(m0[..., None, None], (batch, 2, nh, 1, B_DK))]
    h_shape = jax.ShapeDtypeStruct((batch * seq, B_VD), BF16)
    h_spec = pl.BlockSpec((seq, B_DV), lambda b, h: (b, h))
    if emit_state:
        out_shape = (h_shape,
                     jax.ShapeDtypeStruct((batch, 2, nh, B_DV, B_DK), F32),
                     jax.ShapeDtypeStruct((batch, 2, nh, 1, B_DK), F32),
                     jax.ShapeDtypeStruct((batch, 2, nh, 1, B_DK), F32))
        out_specs = (h_spec, state_c, state_v, state_v)
    else:
        out_shape, out_specs = h_shape, h_spec
    return pl.pallas_call(
        functools.partial(_mlstm_kernel, n_chunks=seq // B_CHUNK, has_init=has_init, emit_state=emit_state),
        out_shape=out_shape,
        grid=(batch, nh),
        in_specs=in_specs, out_specs=out_specs,
        scratch_shapes=[pltpu.VMEM((seq, B_DV), F32)],
        compiler_params=_params(("arbitrary", "arbitrary"), 48),
        name="mlstm_ctx" if emit_state else "mlstm_lat",
    )(*args)


def _mixer_b(z, gates, state_c, state_n, state_m, bias, norm_w):
    nh = B_HEADS
    g = gates[:, :4 * nh].reshape(N_TOK, 4, nh)
    gcol = g.transpose(2, 0, 1)
    grow = g.transpose(2, 1, 0)
    o_p, c_new, n_new, m_new = _mlstm_call(z, gcol, grow, bias, norm_w, SEQ, BATCH, 0, None)
    o_s = _mlstm_call(z, gcol, grow, bias, norm_w, DEC_SEQ, DEC_BATCH, SAMPLE_ROW0,
                      (state_c, state_n, state_m))
    return o_p, o_s, c_new, n_new.reshape(BATCH, 2, nh, B_DK), m_new[:, :, :, 0, 0]


C_SCALE = C_HEAD_DIM ** -0.5


def _c_ctx_kernel(q_ref, k_ref, v_ref, o_ref, ko_ref, vo_ref):
    for hh in range(CTX_HEADS):
        cols = slice(hh * LANES, (hh + 1) * LANES)
        k = k_ref[:, cols]
        v = v_ref[:, cols]
        ko_ref[:, cols] = k
        vo_ref[:, cols] = v
        s = _dot_nt(q_ref[:, cols].astype(BF16), k.astype(BF16)) * C_SCALE
        o_ref[:, cols] = _attend([s], [_with_ones(v)]).astype(BF16)


def _na_row_start(r):
    return min(max(r - NA_ROWS // 2, 0), GRID_H - NA_ROWS)


def _na_row_groups():
    groups = []
    for r in range(GRID_H):
        if groups and _na_row_start(groups[-1][0]) == _na_row_start(r):
            groups[-1].append(r)
        else:
            groups.append([r])
    return groups


def _c_lat_kernel(bias_ref, q_ref, k_ref, v_ref, pk_ref, pv_ref, o_ref, kb_ref, v1_ref, sc_ref, ec_ref, acc_ref):
    n_win = NA_ROWS * GRID_W
    kb_ref[...] = k_ref[...].astype(BF16)
    v1_ref[...] = _with_ones(v_ref[...])
    sc_ref[...] = _dot_nt(q_ref[...].astype(BF16), pk_ref[...].astype(BF16)) * C_SCALE
    for rows_g in _na_row_groups():
        r0 = _na_row_start(rows_g[0])
        rows = slice(rows_g[0] * GRID_W, (rows_g[-1] + 1) * GRID_W)
        win = slice(r0 * GRID_W, r0 * GRID_W + n_win)
        strips = []
        for r in rows_g:
            strip = NA_ROWS - 1 - (r - r0)
            even = strip - strip % 2
            strips.append(bias_ref[strip % 2, :, even * GRID_W:even * GRID_W + n_win])
        bias = strips[0] if len(strips) == 1 else jnp.concatenate(strips, axis=0)
        s_n = _dot_nt(q_ref[rows, :].astype(BF16), kb_ref[win, :]) * C_SCALE + bias
        s_c = sc_ref[rows, :]
        m = jnp.maximum(jnp.max(s_n, axis=-1, keepdims=True), jnp.max(s_c, axis=-1, keepdims=True))
        ec_ref[rows, :] = jnp.exp(s_c - m).astype(BF16)
        acc_ref[rows, :] = _dot(jnp.exp(s_n - m).astype(BF16), v1_ref[win, :])
    acc = acc_ref[...] + _dot(ec_ref[...], _with_ones(pv_ref[...]))
    o_ref[...] = (acc[:, :LANES] / acc[:, LANES:]).astype(BF16)


def _na_bias(rpb):
    n_drow, n_dcol = 2 * NA_ROWS - 1, 2 * NA_COLS - 1
    cq = np.arange(GRID_W)[:, None]
    kc = np.arange(GRID_W)[None, :]
    cstart = np.clip(cq - NA_COLS // 2, 0, GRID_W - NA_COLS)
    col_ok = (kc >= cstart) & (kc < cstart + NA_COLS)
    dcol = np.clip(kc - cq, -(NA_COLS - 1), NA_COLS - 1) + NA_COLS - 1
    onehot = (dcol[:, :, None] == np.arange(n_dcol)).astype(np.float32)
    strips = jnp.einsum('hab,qkb->hqak', rpb.astype(F32), onehot, precision=lax.Precision.HIGHEST)
    strips = jnp.where(col_ok[None, :, None, :], strips, NEG).reshape(C_HEADS, GRID_W, n_drow * GRID_W)
    strips = jnp.pad(strips, ((0, 0), (0, 0), (0, 2 * GRID_W)))
    return jnp.stack([strips[:, :, :(n_drow + 1) * GRID_W], strips[:, :, GRID_W:]], axis=1)


def _mixer_c(qkv, cache_k, cache_v, rpb):
    nh = C_HEADS
    cw = CTX_HEADS * LANES
    nt = nh // CTX_HEADS
    ctx_out = pl.BlockSpec((SEQ, cw), lambda b, h: (b, h))
    kv_shape = jax.ShapeDtypeStruct((N_PROMPT, D_MODEL), F32)
    o_p, k_new, v_new = pl.pallas_call(
        _c_ctx_kernel,
        out_shape=(jax.ShapeDtypeStruct((N_PROMPT, D_MODEL), BF16), kv_shape, kv_shape),
        grid=(BATCH, nt),
        in_specs=[pl.BlockSpec((SEQ, cw), lambda b, h: (b, h)),
                  pl.BlockSpec((SEQ, cw), lambda b, h: (b, nt + h)),
                  pl.BlockSpec((SEQ, cw), lambda b, h: (b, 2 * nt + h))],
        out_specs=(ctx_out, ctx_out, ctx_out),
        compiler_params=_params(("arbitrary", "arbitrary"), 32),
        name="na_ctx",
    )(qkv, qkv, qkv)

    past = pl.BlockSpec((PAST_LEN, LANES), lambda h, b: (b, h))
    o_s = pl.pallas_call(
        _c_lat_kernel,
        out_shape=jax.ShapeDtypeStruct((N_SAMPLE, D_MODEL), BF16),
        grid=(nh, DEC_BATCH),
        in_specs=[pl.BlockSpec((None, 2, GRID_W, 2 * NA_ROWS * GRID_W), lambda h, b: (h, 0, 0, 0)),
                  pl.BlockSpec((DEC_SEQ, LANES), lambda h, b: (SAMPLE_ROW0 + b, h)),
                  pl.BlockSpec((DEC_SEQ, LANES), lambda h, b: (SAMPLE_ROW0 + b, nh + h)),
                  pl.BlockSpec((DEC_SEQ, LANES), lambda h, b: (SAMPLE_ROW0 + b, 2 * nh + h)),
                  past, past],
        out_specs=pl.BlockSpec((DEC_SEQ, LANES), lambda h, b: (b, h)),
        scratch_shapes=[pltpu.VMEM((DEC_SEQ, LANES), BF16), pltpu.VMEM((DEC_SEQ, 2 * LANES), BF16),
                        pltpu.VMEM((DEC_SEQ, PAST_LEN), F32), pltpu.VMEM((DEC_SEQ, PAST_LEN), BF16),
                        pltpu.VMEM((DEC_SEQ, 2 * LANES), F32)],
        compiler_params=_params(("arbitrary", "arbitrary"), 40),
        name="na_lat",
    )(_na_bias(rpb), qkv, qkv, qkv,
      cache_k.reshape(DEC_BATCH * PAST_LEN, D_MODEL), cache_v.reshape(DEC_BATCH * PAST_LEN, D_MODEL))
    return o_p, o_s, k_new, v_new


D_SCALE = D_HEAD_DIM ** -0.5
D_QBLOCK = 128
D_BAND = 3 * D_QBLOCK


def _d_ctx_kernel(sink_ref, q_ref, k_ref, v_ref, o_ref, ko_ref, vo_ref):
    pair = pl.program_id(1)
    k = k_ref[...]
    v = v_ref[...]
    ko_ref[...] = k
    vo_ref[...] = v
    kb, v1 = k.astype(BF16), _with_ones(v)
    lo = _lane_lo((SEQ, LANES))
    for tile in range(4):
        kv = tile // 2
        keep = lo if kv == 0 else ~lo
        qt = q_ref[:, tile * LANES:(tile + 1) * LANES] * D_SCALE
        halves = []
        for e in (0, 1):
            qe = qt if e == kv else pltpu.roll(qt, HALF, axis=1)
            s = _dot_nt(jnp.where(keep, qe, 0.0).astype(BF16), kb)
            o = _attend([s], [v1], sink=sink_ref[pair * 2 * D_GROUP + tile * 2 + e])
            halves.append(o if e == kv else pltpu.roll(o, HALF, axis=1))
        o_ref[:, tile * LANES:(tile + 1) * LANES] = jnp.where(lo, halves[0], halves[1]).astype(BF16)


def _d_lat_kernel(sink_ref, cos_ref, nxt_ref, prv_ref, q_ref, k_ref, v_ref, pk_ref, pv_ref, o_ref,
                  kb_ref, v1_ref, pkb_ref, pv1_ref):
    pair = pl.program_id(1)
    kb_ref[...] = _rope(k_ref[...], cos_ref[...], nxt_ref[...], prv_ref[...]).astype(BF16)
    v1_ref[...] = _with_ones(v_ref[...])
    pkb_ref[...] = pk_ref[...].astype(BF16)
    pv1_ref[...] = _with_ones(pv_ref[...])
    tq = D_QBLOCK
    lo = _lane_lo((tq, LANES))

    def block(bi, carry):
        rows = pl.ds(pl.multiple_of(bi * tq, tq), tq)
        start = pl.multiple_of(jnp.clip((bi - 1) * tq, 0, DEC_SEQ - D_BAND), tq)
        band = pl.ds(start, D_BAND)
        cos, nxt, prv = cos_ref[rows, :], nxt_ref[rows, :], prv_ref[rows, :]
        qpos = bi * tq + lax.broadcasted_iota(jnp.int32, (tq, D_BAND), 0)
        kpos = start + lax.broadcasted_iota(jnp.int32, (tq, D_BAND), 1)
        mask = jnp.where(jnp.abs(qpos - kpos) <= WINDOW, 0.0, NEG)
        mask4 = jnp.concatenate([mask] * D_GROUP, axis=0)
        tiles = [_rope(q_ref[rows, t * LANES:(t + 1) * LANES], cos, nxt, prv) * D_SCALE for t in range(4)]
        outs = {}
        for kv in (0, 1):
            keep = lo if kv == 0 else ~lo
            stack, sinks = [], []
            for t in (2 * kv, 2 * kv + 1):
                for e in (0, 1):
                    qe = tiles[t] if e == kv else pltpu.roll(tiles[t], HALF, axis=1)
                    stack.append(jnp.where(keep, qe, 0.0))
                    sinks.append(jnp.full((tq, 1), sink_ref[pair * 2 * D_GROUP + t * 2 + e], F32))
            qs = jnp.concatenate(stack, axis=0).astype(BF16)
            o = _attend([_dot_nt(qs, kb_ref[band, :]) + mask4, _dot_nt(qs, pkb_ref[...])],
                        [v1_ref[band, :], pv1_ref[...]], sink=jnp.concatenate(sinks, axis=0))
            for i, t in enumerate((2 * kv, 2 * kv + 1)):
                for e in (0, 1):
                    piece = o[(2 * i + e) * tq:(2 * i + e + 1) * tq, :]
                    outs[(t, e)] = piece if e == kv else pltpu.roll(piece, HALF, axis=1)
        for t in range(4):
            o_ref[rows, t * LANES:(t + 1) * LANES] = jnp.where(lo, outs[(t, 0)], outs[(t, 1)]).astype(BF16)
        return carry

    lax.fori_loop(0, DEC_SEQ // tq, block, 0)


def _mixer_d(qkv, cache_k, cache_v, sink, rope):
    n_pairs = D_KV_HEADS // 2
    qw = 2 * D_GROUP * D_HEAD_DIM
    k0 = D_HEADS * D_HEAD_DIM // LANES
    v0 = k0 + n_pairs
    kvw = D_KV_HEADS * D_HEAD_DIM
    smem = pl.BlockSpec(memory_space=pltpu.SMEM)
    sink = sink.astype(F32)
    kv_out = pl.BlockSpec((SEQ, LANES), lambda b, p: (b, p))
    kv_shape = jax.ShapeDtypeStruct((N_PROMPT, kvw), F32)
    o_p, k_new, v_new = pl.pallas_call(
        _d_ctx_kernel,
        out_shape=(jax.ShapeDtypeStruct((N_PROMPT, D_MODEL), BF16), kv_shape, kv_shape),
        grid=(BATCH, n_pairs),
        in_specs=[smem,
                  pl.BlockSpec((SEQ, qw), lambda b, p: (b, p)),
                  pl.BlockSpec((SEQ, LANES), lambda b, p: (b, k0 + p)),
                  pl.BlockSpec((SEQ, LANES), lambda b, p: (b, v0 + p))],
        out_specs=(pl.BlockSpec((SEQ, qw), lambda b, p: (b, p)), kv_out, kv_out),
        compiler_params=_params(("arbitrary", "arbitrary"), 40),
        name="gqa_ctx",
    )(sink, qkv, qkv, qkv)

    table = pl.BlockSpec((DEC_SEQ, LANES), lambda b, p: (0, 0))
    past = pl.BlockSpec((PAST_LEN, LANES), lambda b, p: (b, p))
    o_s = pl.pallas_call(
        _d_lat_kernel,
        out_shape=jax.ShapeDtypeStruct((N_SAMPLE, D_MODEL), BF16),
        grid=(DEC_BATCH, n_pairs),
        in_specs=[smem, table, table, table,
                  pl.BlockSpec((DEC_SEQ, qw), lambda b, p: (SAMPLE_ROW0 + b, p)),
                  pl.BlockSpec((DEC_SEQ, LANES), lambda b, p: (SAMPLE_ROW0 + b, k0 + p)),
                  pl.BlockSpec((DEC_SEQ, LANES), lambda b, p: (SAMPLE_ROW0 + b, v0 + p)),
                  past, past],
        out_specs=pl.BlockSpec((DEC_SEQ, qw), lambda b, p: (b, p)),
        scratch_shapes=[pltpu.VMEM((DEC_SEQ, LANES), BF16), pltpu.VMEM((DEC_SEQ, 2 * LANES), BF16),
                        pltpu.VMEM((PAST_LEN, LANES), BF16), pltpu.VMEM((PAST_LEN, 2 * LANES), BF16)],
        compiler_params=_params(("arbitrary", "arbitrary"), 40),
        name="gqa_lat",
    )(sink, *rope, qkv, qkv, qkv,
      cache_k.reshape(DEC_BATCH * PAST_LEN, kvw), cache_v.reshape(DEC_BATCH * PAST_LEN, kvw))
    return o_p, o_s, k_new, v_new


def kernel(x_prompt, x_sample, cache_a_k, cache_a_v, state_b_C, state_b_n, state_b_m, cache_c_k, cache_c_v,
           cache_d_k, cache_d_v, c, c_ctx, w_mod, b_mod, g_norm, w_ff1, w_ff2, a_w_in, a_w_out, a_lambda,
           a_subln, b_w_in, b_gate_bias, b_w_out, b_norm, c_w_in, c_w_out, c_rpb, d_w_in, d_w_out, d_sink):
    x = jnp.concatenate([x_prompt.reshape(N_PROMPT, D_MODEL), x_sample.reshape(N_SAMPLE, D_MODEL)], axis=0)
    cond = jnp.concatenate([c_ctx[None, :], c, jnp.zeros((N_COND - 1 - DEC_BATCH, D_MODEL), F32)], axis=0)
    mod = _modulation(cond, w_mod, b_mod).reshape(DEPTH, N_COND, N_MOD, 1, D_MODEL)
    gains = g_norm.reshape(DEPTH, 4, 1, D_MODEL)
    rope = _rope_tables()
    new = {name: [] for name in ("a_k", "a_v", "b_C", "b_n", "b_m", "c_k", "c_v", "d_k", "d_v")}

    h = _adaln(x, gains, mod, 0)
    for i in range(DEPTH):
        kind, j = i % N_MIXERS, i // N_MIXERS
        if kind == 0:
            lam_init = 0.8 - 0.6 * math.exp(-0.3 * i)
            qkv = _project(h, a_w_in[j])
            o_p, o_s, k_new, v_new = _mixer_a(qkv, cache_a_k[:, j], cache_a_v[:, j], a_lambda[j], a_subln[j],
                                              lam_init, rope)
            w_out = a_w_out[j]
            new["a_k"].append(k_new.reshape(BATCH, SEQ, A_HEADS, A_V_DIM))
            new["a_v"].append(v_new.reshape(BATCH, SEQ, A_HEADS, A_V_DIM))
        elif kind == 1:
            w_in = b_w_in[j]
            z = _project(h, w_in, n=B_MAIN)
            w_gate = jnp.pad(w_in[:, B_MAIN:], ((0, 0), (0, LANES - 4 * B_HEADS)))
            gates = _project(h, w_gate)
            o_p, o_s, c_new, n_new, m_new = _mixer_b(z, gates, state_b_C[:, j], state_b_n[:, j], state_b_m[:, j],
                                                     b_gate_bias[j], b_norm[j])
            w_out = b_w_out[j]
            new["b_C"].append(c_new)
            new["b_n"].append(n_new)
            new["b_m"].append(m_new)
        elif kind == 2:
            qkv = _project(h, c_w_in[j])
            o_p, o_s, k_new, v_new = _mixer_c(qkv, cache_c_k[:, j], cache_c_v[:, j], c_rpb[j])
            w_out = c_w_out[j]
            new["c_k"].append(k_new.reshape(BATCH, SEQ, C_HEADS, C_HEAD_DIM))
            new["c_v"].append(v_new.reshape(BATCH, SEQ, C_HEADS, C_HEAD_DIM))
        else:
            qkv = _project(h, d_w_in[j])
            o_p, o_s, k_new, v_new = _mixer_d(qkv, cache_d_k[:, j], cache_d_v[:, j], d_sink[j], rope)
            w_out = d_w_out[j]
            new["d_k"].append(k_new.reshape(BATCH, SEQ, D_KV_HEADS, D_HEAD_DIM))
            new["d_v"].append(v_new.reshape(BATCH, SEQ, D_KV_HEADS, D_HEAD_DIM))
        y = _project((o_p, o_s), w_out)
        x, h = _residual(x, y, gains, mod, i, 2, 1, (i, 2, 3, 4))
        y = _mlp(h, w_ff1[i], w_ff2[i])
        x, h = _residual(x, y, gains, mod, i, 5, 3, (i + 1, 0, 0, 1) if i + 1 < DEPTH else None)

    x_p, x_s = x
    stack = lambda name: jnp.stack(new[name], axis=1)
    return (x_p.reshape(BATCH, SEQ, D_MODEL), x_s.reshape(DEC_BATCH, DEC_SEQ, D_MODEL),
            stack("a_k"), stack("a_v"), stack("b_C"), stack("b_n"), stack("b_m"),
            stack("c_k"), stack("c_v"), stack("d_k"), stack("d_v"))
```

```python
import functools
import math

import numpy as np
import jax
import jax.numpy as jnp
from jax import lax
from jax.experimental import pallas as pl
from jax.experimental.pallas import tpu as pltpu

D_MODEL = 2048
BATCH = 16
SEQ = 256
DEPTH = 4
DEC_BATCH = 4
DEC_SEQ = 1024
PAST_LEN = 512
GRID_W = 64
N_MIXERS = 4
D_FF = 4 * D_MODEL
N_MOD = 6
RMS_EPS = 1e-6
ROPE_THETA = 10000.0

A_HEADS = 16
A_HALF_DIM = 64
A_V_DIM = 128
B_HEADS = 8
B_DK = 128
B_DV = 256
B_QK = B_HEADS * B_DK
B_VD = B_HEADS * B_DV
B_MAIN = 2 * B_QK + 2 * B_VD
C_HEADS = 16
C_HEAD_DIM = 128
NA_ROWS = 8
NA_COLS = 16
GRID_H = DEC_SEQ // GRID_W
D_HEADS = 32
D_KV_HEADS = 8
D_GROUP = 4
D_HEAD_DIM = 64
WINDOW = 128

N_PROMPT = BATCH * SEQ
N_SAMPLE = DEC_BATCH * DEC_SEQ
N_TOK = N_PROMPT + N_SAMPLE
N_COND = 8
SAMPLE_ROW0 = N_PROMPT // DEC_SEQ

LANES = 128
HALF = LANES // 2
NEG = -1e30
MIB = 1024 * 1024
F32 = jnp.float32
BF16 = jnp.bfloat16


def _params(sem, vmem_mib):
    return pltpu.CompilerParams(dimension_semantics=sem, vmem_limit_bytes=vmem_mib * MIB)


def _dot(a, b):
    return jnp.dot(a, b, preferred_element_type=F32)


def _dot_nt(a, b):
    return lax.dot_general(a, b, (((1,), (1,)), ((), ())), preferred_element_type=F32)


def _dot_tn(a, b):
    return lax.dot_general(a, b, (((0,), (0,)), ((), ())), preferred_element_type=F32)


def _rms(x, g):
    return x * lax.rsqrt(jnp.mean(x * x, axis=-1, keepdims=True) + RMS_EPS) * g


def _cond_index(row0):
    return jnp.where(row0 < N_PROMPT, 0, 1 + (row0 - N_PROMPT) // DEC_SEQ)


def _mod_kernel(c_ref, w_ref, b_ref, o_ref):
    c = c_ref[...]
    s = (c / (1.0 + jnp.exp(-c))).astype(BF16)
    o_ref[...] = _dot(s, w_ref[...].astype(BF16)) + b_ref[...]


def _modulation(cond, w_mod, b_mod):
    tn = 1024
    n = N_MOD * D_MODEL
    return pl.pallas_call(
        _mod_kernel,
        out_shape=jax.ShapeDtypeStruct((DEPTH, N_COND, n), F32),
        grid=(DEPTH, n // tn),
        in_specs=[
            pl.BlockSpec((N_COND, D_MODEL), lambda l, j: (0, 0)),
            pl.BlockSpec((None, D_MODEL, tn), lambda l, j: (l, 0, j)),
            pl.BlockSpec((None, 1, tn), lambda l, j: (l, 0, j)),
        ],
        out_specs=pl.BlockSpec((None, N_COND, tn), lambda l, j: (l, 0, j)),
        compiler_params=_params(("arbitrary", "arbitrary"), 40),
        name="modulation",
    )(cond, w_mod, b_mod.reshape(DEPTH, 1, n))


ROW_TILE = 512


def _mod_spec(layer, which, tile0=0):
    return pl.BlockSpec((None, None, None, 1, D_MODEL),
                        lambda i: (layer, _cond_index((i + tile0) * ROW_TILE), which, 0, 0))


def _gain_spec(layer, which):
    return pl.BlockSpec((None, None, 1, D_MODEL), lambda i: (layer, which, 0, 0))


def _adaln_kernel(x_ref, g_ref, shift_ref, scale_ref, h_ref):
    h = _rms(x_ref[...], g_ref[...]) * (1.0 + scale_ref[...]) + shift_ref[...]
    h_ref[...] = h.astype(BF16)


def _adaln(x, gains, mod, layer):
    row = pl.BlockSpec((ROW_TILE, D_MODEL), lambda i: (i, 0))
    return pl.pallas_call(
        _adaln_kernel,
        out_shape=jax.ShapeDtypeStruct((N_TOK, D_MODEL), BF16),
        grid=(N_TOK // ROW_TILE,),
        in_specs=[row, _gain_spec(layer, 0), _mod_spec(layer, 0), _mod_spec(layer, 1)],
        out_specs=row,
        compiler_params=_params(("arbitrary",), 32),
        name="adaln",
    )(x, gains, mod, mod)


def _residual_adaln_kernel(x_ref, y_ref, gate_ref, gpost_ref, gpre_ref, shift_ref, scale_ref, xo_ref, h_ref):
    x = x_ref[...] + gate_ref[...] * _rms(y_ref[...], gpost_ref[...])
    xo_ref[...] = x
    h = _rms(x, gpre_ref[...]) * (1.0 + scale_ref[...]) + shift_ref[...]
    h_ref[...] = h.astype(BF16)


def _residual_kernel(x_ref, y_ref, gate_ref, gpost_ref, xo_ref):
    xo_ref[...] = x_ref[...] + gate_ref[...] * _rms(y_ref[...], gpost_ref[...])


def _residual(x, y, gains, mod, layer, gate_idx, post_idx, nxt):
    row = pl.BlockSpec((ROW_TILE, D_MODEL), lambda i: (i, 0))
    if nxt is None:
        outs = []
        for tile0, n_rows in ((0, N_PROMPT), (N_PROMPT // ROW_TILE, N_SAMPLE)):
            src = pl.BlockSpec((ROW_TILE, D_MODEL), lambda i, t=tile0: (i + t, 0))
            outs.append(pl.pallas_call(
                _residual_kernel, out_shape=jax.ShapeDtypeStruct((n_rows, D_MODEL), F32),
                grid=(n_rows // ROW_TILE,),
                in_specs=[src, src, _mod_spec(layer, gate_idx, tile0), _gain_spec(layer, post_idx)],
                out_specs=row,
                compiler_params=_params(("arbitrary",), 32), name="residual",
            )(x, y, mod, gains))
        return outs, None
    nl, ng, nshift, nscale = nxt
    return pl.pallas_call(
        _residual_adaln_kernel,
        out_shape=(jax.ShapeDtypeStruct((N_TOK, D_MODEL), F32), jax.ShapeDtypeStruct((N_TOK, D_MODEL), BF16)),
        grid=(N_TOK // ROW_TILE,),
        in_specs=[row, row, _mod_spec(layer, gate_idx), _gain_spec(layer, post_idx),
                  _gain_spec(nl, ng), _mod_spec(nl, nshift), _mod_spec(nl, nscale)],
        out_specs=(row, row),
        compiler_params=_params(("arbitrary",), 40), name="residual_adaln",
    )(x, y, mod, gains, gains, mod, mod)


PROJ_TM = 1024


def _proj_kernel(a_ref, w_ref, o_ref, wbf_ref):
    @pl.when(pl.program_id(1) == 0)
    def _():
        wbf_ref[...] = w_ref[...].astype(BF16)

    o_ref[...] = _dot(a_ref[...], wbf_ref[...]).astype(o_ref.dtype)


def _project(a, w, n=None, out_dtype=F32):
    k = w.shape[0]
    n = w.shape[1] if n is None else n
    tm = PROJ_TM
    tn = min(n, 1024)
    return pl.pallas_call(
        _proj_kernel,
        out_shape=jax.ShapeDtypeStruct((N_TOK, n), out_dtype),
        grid=(n // tn, N_TOK // tm),
        in_specs=[pl.BlockSpec((tm, k), lambda j, i: (i, 0)),
                  pl.BlockSpec((k, tn), lambda j, i: (0, j))],
        out_specs=pl.BlockSpec((tm, tn), lambda j, i: (i, j)),
        scratch_shapes=[pltpu.VMEM((k, tn), BF16)],
        compiler_params=_params(("arbitrary", "arbitrary"), 48),
        name="project",
    )(a, w)


OUT_TM = 512


def _out_proj_kernel(ap_ref, as_ref, w_ref, x_ref, gate_ref, gpost_ref, gpre_ref, shift_ref, scale_ref,
                     xo_ref, h_ref):
    def finish(a):
        x = x_ref[...] + gate_ref[...] * _rms(_dot(a, w_ref[...]), gpost_ref[...])
        xo_ref[...] = x
        h_ref[...] = (_rms(x, gpre_ref[...]) * (1.0 + scale_ref[...]) + shift_ref[...]).astype(BF16)

    @pl.when(pl.program_id(0) < N_PROMPT // OUT_TM)
    def _():
        finish(ap_ref[...])

    @pl.when(pl.program_id(0) >= N_PROMPT // OUT_TM)
    def _():
        finish(as_ref[...])


def _out_project(o_p, o_s, w_out, x, gains, mod, layer):
    tm = OUT_TM
    p_tiles = N_PROMPT // tm
    row = pl.BlockSpec((tm, D_MODEL), lambda i: (i, 0))

    def mod_spec(which):
        return pl.BlockSpec((None, None, None, 1, D_MODEL), lambda i: (layer, _cond_index(i * tm), which, 0, 0))

    return pl.pallas_call(
        _out_proj_kernel,
        out_shape=(jax.ShapeDtypeStruct((N_TOK, D_MODEL), F32), jax.ShapeDtypeStruct((N_TOK, D_MODEL), BF16)),
        grid=(N_TOK // tm,),
        in_specs=[pl.BlockSpec((tm, D_MODEL), lambda i: (jnp.minimum(i, p_tiles - 1), 0)),
                  pl.BlockSpec((tm, D_MODEL), lambda i: (jnp.maximum(i - p_tiles, 0), 0)),
                  pl.BlockSpec((D_MODEL, D_MODEL), lambda i: (0, 0)),
                  row, mod_spec(2), _gain_spec(layer, 1), _gain_spec(layer, 2), mod_spec(3), mod_spec(4)],
        out_specs=(row, row),
        compiler_params=_params(("arbitrary",), 56),
        name="out_project",
    )(o_p, o_s, w_out.astype(BF16), x, mod, gains, gains, mod, mod)


def _mlp_kernel(h_ref, w1_ref, w2_ref, o_ref):
    @pl.when(pl.program_id(1) == 0)
    def _():
        o_ref[...] = jnp.zeros_like(o_ref)

    u = jnp.maximum(_dot(h_ref[...], w1_ref[...].astype(BF16)), 0.0)
    o_ref[...] += _dot((u * u).astype(BF16), w2_ref[...].astype(BF16))


def _mlp(h, w1, w2, layer):
    tm, tf = 1024, 512
    return pl.pallas_call(
        _mlp_kernel,
        out_shape=jax.ShapeDtypeStruct((N_TOK, D_MODEL), F32),
        grid=(N_TOK // tm, D_FF // tf),
        in_specs=[pl.BlockSpec((tm, D_MODEL), lambda i, f: (i, 0)),
                  pl.BlockSpec((None, D_MODEL, tf), lambda i, f: (layer, 0, f)),
                  pl.BlockSpec((None, tf, D_MODEL), lambda i, f: (layer, f, 0))],
        out_specs=pl.BlockSpec((tm, D_MODEL), lambda i, f: (i, 0)),
        compiler_params=_params(("arbitrary", "arbitrary"), 56),
        name="mlp",
    )(h, w1, w2)


def _rope_tables():
    t = jnp.arange(DEC_SEQ)
    lane = np.arange(LANES)
    f = lane % 32
    first = f < 16
    inv = ROPE_THETA ** (-jnp.arange(16, dtype=F32) / 16)
    pos = jnp.where((lane % 64 < 32)[None, :], (t // GRID_W)[:, None], (t % GRID_W)[:, None]).astype(F32)
    ang = pos * inv[f % 16][None, :]
    cos, sin = jnp.cos(ang), jnp.sin(ang)
    sin_next = jnp.where(first[None, :], -sin, 0.0)
    sin_prev = jnp.where(first[None, :], 0.0, sin)
    return cos, sin_next, sin_prev


def _rope(x, cos, sin_next, sin_prev):
    return (x * cos + pltpu.roll(x, LANES - 16, axis=1) * sin_next
            + pltpu.roll(x, 16, axis=1) * sin_prev)


def _lane_lo(shape):
    return lax.broadcasted_iota(jnp.int32, shape, 1) < HALF


def _with_ones(v):
    return jnp.concatenate([v.astype(BF16), jnp.ones(v.shape, BF16)], axis=1)


def _attend(scores, vals_ones, sink=None):
    m = None
    for s in scores:
        mi = jnp.max(s, axis=-1, keepdims=True)
        m = mi if m is None else jnp.maximum(m, mi)
    if sink is not None:
        m = jnp.maximum(m, sink)
    acc = None
    for s, v in zip(scores, vals_ones):
        part = _dot(jnp.exp(s - m).astype(BF16), v)
        acc = part if acc is None else acc + part
    den = acc[:, LANES:]
    if sink is not None:
        den = den + jnp.exp(sink - m)
    return acc[:, :LANES] / den


CTX_HEADS = 4


A_SCALE = A_HALF_DIM ** -0.5


def _diff_lambda(lp, lam_init):
    a = jnp.sum(lp[0] * lp[1], axis=-1, keepdims=True)
    b = jnp.sum(lp[2] * lp[3], axis=-1, keepdims=True)
    return jnp.exp(a) - jnp.exp(b) + lam_init


def _diff_finish(o, subln_ref, lam_init):
    return (_rms(o, subln_ref[...]) * (1.0 - lam_init)).astype(BF16)


def _a_ctx_kernel(lam_ref, subln_ref, q_ref, k_ref, v_ref, o_ref, ko_ref, vo_ref, *, lam_init):
    lo = _lane_lo((SEQ, LANES))
    for hh in range(CTX_HEADS):
        cols = slice(hh * LANES, (hh + 1) * LANES)
        q = q_ref[:, cols] * A_SCALE
        k = k_ref[:, cols]
        v = v_ref[:, cols]
        ko_ref[:, cols] = k
        vo_ref[:, cols] = v
        kb, v1 = k.astype(BF16), _with_ones(v)
        o1 = _attend([_dot_nt(jnp.where(lo, q, 0.0).astype(BF16), kb)], [v1])
        o2 = _attend([_dot_nt(jnp.where(lo, 0.0, q).astype(BF16), kb)], [v1])
        lam = _diff_lambda(lam_ref[:, hh], lam_init)
        o_ref[:, cols] = _diff_finish(o1 - lam * o2, subln_ref, lam_init)


A_LAT_TQ = 256


def _a_lat_kernel(lam_ref, subln_ref, cos_ref, nxt_ref, prv_ref, q_ref, k_ref, v_ref, pk_ref, pv_ref, o_ref,
                  kb_ref, v1_ref, e_ref, *, lam_init):
    kb_ref[:PAST_LEN, :] = pk_ref[...].astype(BF16)
    kb_ref[PAST_LEN:, :] = _rope(k_ref[...], cos_ref[...], nxt_ref[...], prv_ref[...]).astype(BF16)
    v1_ref[:PAST_LEN, :] = _with_ones(pv_ref[...])
    v1_ref[PAST_LEN:, :] = _with_ones(v_ref[...])
    lam = _diff_lambda(lam_ref[...], lam_init)
    lo = _lane_lo((A_LAT_TQ, LANES))
    n_blocks = DEC_SEQ // A_LAT_TQ

    def weights(i):
        rows = slice(i * A_LAT_TQ, (i + 1) * A_LAT_TQ)
        q = _rope(q_ref[rows, :], cos_ref[rows, :], nxt_ref[rows, :], prv_ref[rows, :]) * A_SCALE
        for half, keep in enumerate((lo, ~lo)):
            s = _dot_nt(jnp.where(keep, q, 0.0).astype(BF16), kb_ref[...])
            e_ref[i % 2, half] = jnp.exp(s - jnp.max(s, axis=-1, keepdims=True)).astype(BF16)

    def values(i):
        rows = slice(i * A_LAT_TQ, (i + 1) * A_LAT_TQ)
        outs = []
        for half in (0, 1):
            acc = _dot(e_ref[i % 2, half], v1_ref[...])
            outs.append(acc[:, :LANES] / acc[:, LANES:])
        o_ref[rows, :] = _diff_finish(outs[0] - lam * outs[1], subln_ref, lam_init)

    weights(0)
    for i in range(1, n_blocks):
        values(i - 1)
        weights(i)
    values(n_blocks - 1)


def _mixer_a(qkv, cache_k, cache_v, lam_p, subln, lam_init, rope):
    nh = A_HEADS
    lam4 = lam_p.reshape(4, nh, 1, A_HALF_DIM)
    sub2 = subln.reshape(1, A_V_DIM)
    cw = CTX_HEADS * LANES
    nt = nh // CTX_HEADS
    ctx_out = pl.BlockSpec((SEQ, cw), lambda b, h: (b, h))
    kv_shape = jax.ShapeDtypeStruct((N_PROMPT, nh * A_V_DIM), F32)
    o_p, k_new, v_new = pl.pallas_call(
        functools.partial(_a_ctx_kernel, lam_init=lam_init),
        out_shape=(jax.ShapeDtypeStruct((N_PROMPT, nh * A_V_DIM), BF16), kv_shape, kv_shape),
        grid=(BATCH, nt),
        in_specs=[pl.BlockSpec((4, CTX_HEADS, 1, A_HALF_DIM), lambda b, h: (0, h, 0, 0)),
                  pl.BlockSpec((1, A_V_DIM), lambda b, h: (0, 0)),
                  pl.BlockSpec((SEQ, cw), lambda b, h: (b, h)),
                  pl.BlockSpec((SEQ, cw), lambda b, h: (b, nt + h)),
                  pl.BlockSpec((SEQ, cw), lambda b, h: (b, 2 * nt + h))],
        out_specs=(ctx_out, ctx_out, ctx_out),
        compiler_params=_params(("arbitrary", "arbitrary"), 32),
        name="diff_attn_ctx",
    )(lam4, sub2, qkv, qkv, qkv)

    table = pl.BlockSpec((DEC_SEQ, LANES), lambda b, h: (0, 0))
    past = pl.BlockSpec((PAST_LEN, LANES), lambda b, h: (b, h))
    o_s = pl.pallas_call(
        functools.partial(_a_lat_kernel, lam_init=lam_init),
        out_shape=jax.ShapeDtypeStruct((N_SAMPLE, nh * A_V_DIM), BF16),
        grid=(DEC_BATCH, nh),
        in_specs=[pl.BlockSpec((4, None, 1, A_HALF_DIM), lambda b, h: (0, h, 0, 0)),
                  pl.BlockSpec((1, A_V_DIM), lambda b, h: (0, 0)),
                  table, table, table,
                  pl.BlockSpec((DEC_SEQ, LANES), lambda b, h: (SAMPLE_ROW0 + b, h)),
                  pl.BlockSpec((DEC_SEQ, LANES), lambda b, h: (SAMPLE_ROW0 + b, nh + h)),
                  pl.BlockSpec((DEC_SEQ, LANES), lambda b, h: (SAMPLE_ROW0 + b, 2 * nh + h)),
                  past, past],
        out_specs=pl.BlockSpec((DEC_SEQ, LANES), lambda b, h: (b, h)),
        scratch_shapes=[pltpu.VMEM((PAST_LEN + DEC_SEQ, LANES), BF16),
                        pltpu.VMEM((PAST_LEN + DEC_SEQ, 2 * LANES), BF16),
                        pltpu.VMEM((2, 2, A_LAT_TQ, PAST_LEN + DEC_SEQ), BF16)],
        compiler_params=_params(("arbitrary", "arbitrary"), 40),
        name="diff_attn_lat",
    )(lam4, sub2, *rope, qkv, qkv, qkv,
      cache_k.reshape(DEC_BATCH * PAST_LEN, nh * A_V_DIM), cache_v.reshape(DEC_BATCH * PAST_LEN, nh * A_V_DIM))
    return o_p, o_s, k_new, v_new


B_CHUNK = 256


def _log_sigmoid(x):
    return jnp.minimum(x, 0.0) - jnp.log(1.0 + jnp.exp(-jnp.abs(x)))


def _mlstm_kernel(*refs, n_chunks, has_init, emit_state):
    it = iter(refs)
    q_ref, k_ref, v_ref, og_ref, gcol_ref, grow_ref, bcol_ref, brow_ref, nw_ref = (next(it) for _ in range(9))
    if has_init:
        c0_ref, n0_ref, m0_ref = next(it), next(it), next(it)
    h_ref = next(it)
    if emit_state:
        c_out, n_out, m_out = next(it), next(it), next(it)
    hs_ref = next(it)

    L = B_CHUNK
    ti = lax.broadcasted_iota(jnp.int32, (L, L), 0)
    si = lax.broadcasted_iota(jnp.int32, (L, L), 1)
    gcol = gcol_ref[...] + brow_ref[...]
    grow = grow_ref[...] + bcol_ref[...]
    qscale = B_DK ** -0.5

    for d in (0, 1):
        valid = (si <= ti) if d == 0 else (si >= ti)
        valid_t = (ti <= si) if d == 0 else (ti >= si)
        last = L - 1 if d == 0 else 0
        if has_init:
            C, n, m = c0_ref[d], n0_ref[d], m0_ref[d][:, :1]
        else:
            C, n, m = jnp.zeros((B_DV, B_DK), F32), jnp.zeros((1, B_DK), F32), jnp.zeros((1, 1), F32)
        for ci in range(n_chunks):
            c = ci if d == 0 else n_chunks - 1 - ci
            rows = slice(c * L, (c + 1) * L)
            q = q_ref[rows, :] * qscale
            k = k_ref[rows, :]
            v = v_ref[rows, :]
            qb, kb = q.astype(BF16), k.astype(BF16)
            ig_col = gcol[rows, 2 * d:2 * d + 1]
            lf_col = _log_sigmoid(gcol[rows, 2 * d + 1:2 * d + 2])
            ig_row = grow[2 * d:2 * d + 1, rows]
            lf_row = _log_sigmoid(grow[2 * d + 1:2 * d + 2, rows])
            b_col = jnp.sum(jnp.where(valid, lf_row, 0.0), axis=1, keepdims=True)
            b_row = jnp.sum(jnp.where(valid_t, lf_col, 0.0), axis=0, keepdims=True)
            dm = jnp.where(valid, b_col - b_row + ig_row, NEG)
            inter = b_col + m
            m_t = jnp.maximum(inter, jnp.max(dm, axis=1, keepdims=True))
            w = jnp.exp(dm - m_t)
            a_in = jnp.exp(inter - m_t)
            sw = _dot_nt(qb, kb) * w
            num = _dot(sw.astype(BF16), v.astype(BF16)) + a_in * _dot_nt(qb, C.astype(BF16))
            den = jnp.sum(sw, axis=1, keepdims=True) + a_in * jnp.sum(q * n, axis=1, keepdims=True)
            h = num / jnp.maximum(jnp.abs(den), jnp.exp(-m_t))
            if d == 0:
                hs_ref[rows, :] = h
            else:
                hs_ref[rows, :] += h
            if emit_state or ci + 1 < n_chunks:
                m_last = m_t[last:last + 1, :]
                al = a_in[last:last + 1, :]
                wl = jnp.exp(b_col[last:last + 1, :] - b_col + ig_col - m_last)
                C = al * C + _dot_tn((v * wl).astype(BF16), kb)
                n = al * n + jnp.sum(wl * k, axis=0, keepdims=True)
                m = m_last
        if emit_state:
            c_out[d] = C
            n_out[d] = n
            m_out[d] = jnp.broadcast_to(m, (1, LANES))

    og = og_ref[...]
    h_ref[...] = (_rms(hs_ref[...], nw_ref[...]) * (1.0 / (1.0 + jnp.exp(-og)))).astype(BF16)


def _mlstm_call(z, gcol, grow, bias, norm_w, seq, batch, row0, init):
    nh = B_HEADS
    has_init = init is not None
    emit_state = not has_init
    bcol = bias.reshape(2, 2, nh).transpose(2, 0, 1).reshape(nh, 4, 1)
    brow = bcol.reshape(nh, 1, 4)
    qk_tiles = B_QK // B_DK
    vd0 = 2 * B_QK // B_DV
    in_specs = [
        pl.BlockSpec((seq, B_DK), lambda b, h: (row0 + b, h)),
        pl.BlockSpec((seq, B_DK), lambda b, h: (row0 + b, qk_tiles + h)),
        pl.BlockSpec((seq, B_DV), lambda b, h: (row0 + b, vd0 + h)),
        pl.BlockSpec((seq, B_DV), lambda b, h: (row0 + b, vd0 + nh + h)),
        pl.BlockSpec((None, seq, 4), lambda b, h: (h, row0 + b, 0)),
        pl.BlockSpec((None, 4, seq), lambda b, h: (h, 0, row0 + b)),
        pl.BlockSpec((None, 4, 1), lambda b, h: (h, 0, 0)),
        pl.BlockSpec((None, 1, 4), lambda b, h: (h, 0, 0)),
        pl.BlockSpec((1, B_DV), lambda b, h: (0, h)),
    ]
    args = [z, z, z, z, gcol, grow, bcol, brow, norm_w.reshape(1, B_VD)]
    state_c = pl.BlockSpec((None, 2, None, B_DV, B_DK), lambda b, h: (b, 0, h, 0, 0))
    state_v = pl.BlockSpec((None, 2, None, 1, B_DK), lambda b, h: (b, 0, h, 0, 0))
    if has_init:
        c0, n0, m0 = init
        in_specs += [state_c, state_v, state_v]
        args += [c0, n0.reshape(batch, 2, nh, 1, B_DK),
                 jnp.broadcast_to(m0[..., None, None], (batch, 2, nh, 1, B_DK))]
    h_shape = jax.ShapeDtypeStruct((batch * seq, B_VD), BF16)
    h_spec = pl.BlockSpec((seq, B_DV), lambda b, h: (b, h))
    if emit_state:
        out_shape = (h_shape,
                     jax.ShapeDtypeStruct((batch, 2, nh, B_DV, B_DK), F32),
                     jax.ShapeDtypeStruct((batch, 2, nh, 1, B_DK), F32),
                     jax.ShapeDtypeStruct((batch, 2, nh, 1, B_DK), F32))
        out_specs = (h_spec, state_c, state_v, state_v)
    else:
        out_shape, out_specs = h_shape, h_spec
    return pl.pallas_call(
        functools.partial(_mlstm_kernel, n_chunks=seq // B_CHUNK, has_init=has_init, emit_state=emit_state),
        out_shape=out_shape,
        grid=(batch, nh),
        in_specs=in_specs, out_specs=out_specs,
        scratch_shapes=[pltpu.VMEM((seq, B_DV), F32)],
        compiler_params=_params(("arbitrary", "arbitrary"), 48),
        name="mlstm_ctx" if emit_state else "mlstm_lat",
    )(*args)


def _mixer_b(z, gates, state_c, state_n, state_m, bias, norm_w):
    nh = B_HEADS
    g = gates[:, :4 * nh].reshape(N_TOK, 4, nh)
    gcol = g.transpose(2, 0, 1)
    grow = g.transpose(2, 1, 0)
    o_p, c_new, n_new, m_new = _mlstm_call(z, gcol, grow, bias, norm_w, SEQ, BATCH, 0, None)
    o_s = _mlstm_call(z, gcol, grow, bias, norm_w, DEC_SEQ, DEC_BATCH, SAMPLE_ROW0,
                      (state_c, state_n, state_m))
    return o_p, o_s, c_new, n_new.reshape(BATCH, 2, nh, B_DK), m_new[:, :, :, 0, 0]


C_SCALE = C_HEAD_DIM ** -0.5


def _c_ctx_kernel(q_ref, k_ref, v_ref, o_ref, ko_ref, vo_ref):
    for hh in range(CTX_HEADS):
        cols = slice(hh * LANES, (hh + 1) * LANES)
        k = k_ref[:, cols]
        v = v_ref[:, cols]
        ko_ref[:, cols] = k
        vo_ref[:, cols] = v
        s = _dot_nt(q_ref[:, cols].astype(BF16), k.astype(BF16)) * C_SCALE
        o_ref[:, cols] = _attend([s], [_with_ones(v)]).astype(BF16)


def _na_row_start(r):
    return min(max(r - NA_ROWS // 2, 0), GRID_H - NA_ROWS)


def _na_row_groups():
    groups = []
    for r in range(GRID_H):
        if groups and _na_row_start(groups[-1][0]) == _na_row_start(r):
            groups[-1].append(r)
        else:
            groups.append([r])
    return groups


def _c_lat_kernel(bias_ref, q_ref, k_ref, v_ref, pk_ref, pv_ref, o_ref, kb_ref, v1_ref, sc_ref, ec_ref, acc_ref):
    n_win = NA_ROWS * GRID_W
    kb_ref[...] = k_ref[...].astype(BF16)
    v1_ref[...] = _with_ones(v_ref[...])
    sc_ref[...] = _dot_nt(q_ref[...].astype(BF16), pk_ref[...].astype(BF16)) * C_SCALE
    for rows_g in _na_row_groups():
        r0 = _na_row_start(rows_g[0])
        rows = slice(rows_g[0] * GRID_W, (rows_g[-1] + 1) * GRID_W)
        win = slice(r0 * GRID_W, r0 * GRID_W + n_win)
        strips = []
        for r in rows_g:
            strip = NA_ROWS - 1 - (r - r0)
            even = strip - strip % 2
            strips.append(bias_ref[strip % 2, :, even * GRID_W:even * GRID_W + n_win])
        bias = strips[0] if len(strips) == 1 else jnp.concatenate(strips, axis=0)
        s_n = _dot_nt(q_ref[rows, :].astype(BF16), kb_ref[win, :]) * C_SCALE + bias
        s_c = sc_ref[rows, :]
        m = jnp.maximum(jnp.max(s_n, axis=-1, keepdims=True), jnp.max(s_c, axis=-1, keepdims=True))
        ec_ref[rows, :] = jnp.exp(s_c - m).astype(BF16)
        acc_ref[rows, :] = _dot(jnp.exp(s_n - m).astype(BF16), v1_ref[win, :])
    acc = acc_ref[...] + _dot(ec_ref[...], _with_ones(pv_ref[...]))
    o_ref[...] = (acc[:, :LANES] / acc[:, LANES:]).astype(BF16)


def _na_bias(rpb):
    n_drow, n_dcol = 2 * NA_ROWS - 1, 2 * NA_COLS - 1
    cq = np.arange(GRID_W)[:, None]
    kc = np.arange(GRID_W)[None, :]
    cstart = np.clip(cq - NA_COLS // 2, 0, GRID_W - NA_COLS)
    col_ok = (kc >= cstart) & (kc < cstart + NA_COLS)
    dcol = np.clip(kc - cq, -(NA_COLS - 1), NA_COLS - 1) + NA_COLS - 1
    onehot = (dcol[:, :, None] == np.arange(n_dcol)).astype(np.float32)
    strips = jnp.einsum('hab,qkb->hqak', rpb.astype(F32), onehot, precision=lax.Precision.HIGHEST)
    strips = jnp.where(col_ok[None, :, None, :], strips, NEG).reshape(C_HEADS, GRID_W, n_drow * GRID_W)
    strips = jnp.pad(strips, ((0, 0), (0, 0), (0, 2 * GRID_W)))
    return jnp.stack([strips[:, :, :(n_drow + 1) * GRID_W], strips[:, :, GRID_W:]], axis=1)


def _mixer_c(qkv, cache_k, cache_v, rpb):
    nh = C_HEADS
    cw = CTX_HEADS * LANES
    nt = nh // CTX_HEADS
    ctx_out = pl.BlockSpec((SEQ, cw), lambda b, h: (b, h))
    kv_shape = jax.ShapeDtypeStruct((N_PROMPT, D_MODEL), F32)
    o_p, k_new, v_new = pl.pallas_call(
        _c_ctx_kernel,
        out_shape=(jax.ShapeDtypeStruct((N_PROMPT, D_MODEL), BF16), kv_shape, kv_shape),
        grid=(BATCH, nt),
        in_specs=[pl.BlockSpec((SEQ, cw), lambda b, h: (b, h)),
                  pl.BlockSpec((SEQ, cw), lambda b, h: (b, nt + h)),
                  pl.BlockSpec((SEQ, cw), lambda b, h: (b, 2 * nt + h))],
        out_specs=(ctx_out, ctx_out, ctx_out),
        compiler_params=_params(("arbitrary", "arbitrary"), 32),
        name="na_ctx",
    )(qkv, qkv, qkv)

    past = pl.BlockSpec((PAST_LEN, LANES), lambda h, b: (b, h))
    o_s = pl.pallas_call(
        _c_lat_kernel,
        out_shape=jax.ShapeDtypeStruct((N_SAMPLE, D_MODEL), BF16),
        grid=(nh, DEC_BATCH),
        in_specs=[pl.BlockSpec((None, 2, GRID_W, 2 * NA_ROWS * GRID_W), lambda h, b: (h, 0, 0, 0)),
                  pl.BlockSpec((DEC_SEQ, LANES), lambda h, b: (SAMPLE_ROW0 + b, h)),
                  pl.BlockSpec((DEC_SEQ, LANES), lambda h, b: (SAMPLE_ROW0 + b, nh + h)),
                  pl.BlockSpec((DEC_SEQ, LANES), lambda h, b: (SAMPLE_ROW0 + b, 2 * nh + h)),
                  past, past],
        out_specs=pl.BlockSpec((DEC_SEQ, LANES), lambda h, b: (b, h)),
        scratch_shapes=[pltpu.VMEM((DEC_SEQ, LANES), BF16), pltpu.VMEM((DEC_SEQ, 2 * LANES), BF16),
                        pltpu.VMEM((DEC_SEQ, PAST_LEN), F32), pltpu.VMEM((DEC_SEQ, PAST_LEN), BF16),
                        pltpu.VMEM((DEC_SEQ, 2 * LANES), F32)],
        compiler_params=_params(("arbitrary", "arbitrary"), 40),
        name="na_lat",
    )(_na_bias(rpb), qkv, qkv, qkv,
      cache_k.reshape(DEC_BATCH * PAST_LEN, D_MODEL), cache_v.reshape(DEC_BATCH * PAST_LEN, D_MODEL))
    return o_p, o_s, k_new, v_new


D_SCALE = D_HEAD_DIM ** -0.5
D_QBLOCK = 128
D_BAND = 3 * D_QBLOCK


def _d_ctx_kernel(sink_ref, q_ref, k_ref, v_ref, o_ref, ko_ref, vo_ref):
    pair = pl.program_id(1)
    k = k_ref[...]
    v = v_ref[...]
    ko_ref[...] = k
    vo_ref[...] = v
    kb, v1 = k.astype(BF16), _with_ones(v)
    lo = _lane_lo((SEQ, LANES))
    for tile in range(4):
        kv = tile // 2
        keep = lo if kv == 0 else ~lo
        qt = q_ref[:, tile * LANES:(tile + 1) * LANES] * D_SCALE
        halves = []
        for e in (0, 1):
            qe = qt if e == kv else pltpu.roll(qt, HALF, axis=1)
            s = _dot_nt(jnp.where(keep, qe, 0.0).astype(BF16), kb)
            o = _attend([s], [v1], sink=sink_ref[pair * 2 * D_GROUP + tile * 2 + e])
            halves.append(o if e == kv else pltpu.roll(o, HALF, axis=1))
        o_ref[:, tile * LANES:(tile + 1) * LANES] = jnp.where(lo, halves[0], halves[1]).astype(BF16)


def _d_lat_kernel(sink_ref, cos_ref, nxt_ref, prv_ref, q_ref, k_ref, v_ref, pk_ref, pv_ref, o_ref,
                  kb_ref, v1_ref, pkb_ref, pv1_ref, e_ref, es_ref):
    pair = pl.program_id(1)
    kb_ref[...] = _rope(k_ref[...], cos_ref[...], nxt_ref[...], prv_ref[...]).astype(BF16)
    v1_ref[...] = _with_ones(v_ref[...])
    pkb_ref[...] = pk_ref[...].astype(BF16)
    pv1_ref[...] = _with_ones(pv_ref[...])
    tq = D_QBLOCK
    lo = _lane_lo((tq, LANES))
    n_blocks = DEC_SEQ // tq

    def band_of(bi):
        start = min(max((bi - 1) * tq, 0), DEC_SEQ - D_BAND)
        return start, slice(start, start + D_BAND)

    def weights(bi):
        rows = slice(bi * tq, (bi + 1) * tq)
        start, band = band_of(bi)
        cos, nxt, prv = cos_ref[rows, :], nxt_ref[rows, :], prv_ref[rows, :]
        qpos = bi * tq + lax.broadcasted_iota(jnp.int32, (tq, D_BAND), 0)
        kpos = start + lax.broadcasted_iota(jnp.int32, (tq, D_BAND), 1)
        mask = jnp.where(jnp.abs(qpos - kpos) <= WINDOW, 0.0, NEG)
        mask4 = jnp.concatenate([mask] * D_GROUP, axis=0)
        tiles = [_rope(q_ref[rows, t * LANES:(t + 1) * LANES], cos, nxt, prv) * D_SCALE for t in range(4)]
        for kv in (0, 1):
            keep = lo if kv == 0 else ~lo
            stack, sinks = [], []
            for t in (2 * kv, 2 * kv + 1):
                for e in (0, 1):
                    qe = tiles[t] if e == kv else pltpu.roll(tiles[t], HALF, axis=1)
                    stack.append(jnp.where(keep, qe, 0.0))
                    sinks.append(jnp.full((tq, 1), sink_ref[pair * 2 * D_GROUP + t * 2 + e], F32))
            qs = jnp.concatenate(stack, axis=0).astype(BF16)
            sink = jnp.concatenate(sinks, axis=0)
            s_band = _dot_nt(qs, kb_ref[band, :]) + mask4
            s_past = _dot_nt(qs, pkb_ref[...])
            m = jnp.maximum(jnp.maximum(jnp.max(s_band, axis=-1, keepdims=True),
                                        jnp.max(s_past, axis=-1, keepdims=True)), sink)
            e_ref[bi % 2, kv, :, :D_BAND] = jnp.exp(s_band - m).astype(BF16)
            e_ref[bi % 2, kv, :, D_BAND:] = jnp.exp(s_past - m).astype(BF16)
            es_ref[bi % 2, kv] = jnp.exp(sink - m)

    def values(bi):
        rows = slice(bi * tq, (bi + 1) * tq)
        _, band = band_of(bi)
        outs = {}
        for kv in (0, 1):
            acc = (_dot(e_ref[bi % 2, kv, :, :D_BAND], v1_ref[band, :])
                   + _dot(e_ref[bi % 2, kv, :, D_BAND:], pv1_ref[...]))
            o = acc[:, :LANES] / (acc[:, LANES:] + es_ref[bi % 2, kv])
            for i, t in enumerate((2 * kv, 2 * kv + 1)):
                for e in (0, 1):
                    piece = o[(2 * i + e) * tq:(2 * i + e + 1) * tq, :]
                    outs[(t, e)] = piece if e == kv else pltpu.roll(piece, HALF, axis=1)
        for t in range(4):
            o_ref[rows, t * LANES:(t + 1) * LANES] = jnp.where(lo, outs[(t, 0)], outs[(t, 1)]).astype(BF16)

    weights(0)
    for bi in range(1, n_blocks):
        values(bi - 1)
        weights(bi)
    values(n_blocks - 1)


def _mixer_d(qkv, cache_k, cache_v, sink, rope):
    n_pairs = D_KV_HEADS // 2
    qw = 2 * D_GROUP * D_HEAD_DIM
    k0 = D_HEADS * D_HEAD_DIM // LANES
    v0 = k0 + n_pairs
    kvw = D_KV_HEADS * D_HEAD_DIM
    smem = pl.BlockSpec(memory_space=pltpu.SMEM)
    sink = sink.astype(F32)
    kv_out = pl.BlockSpec((SEQ, LANES), lambda b, p: (b, p))
    kv_shape = jax.ShapeDtypeStruct((N_PROMPT, kvw), F32)
    o_p, k_new, v_new = pl.pallas_call(
        _d_ctx_kernel,
        out_shape=(jax.ShapeDtypeStruct((N_PROMPT, D_MODEL), BF16), kv_shape, kv_shape),
        grid=(BATCH, n_pairs),
        in_specs=[smem,
                  pl.BlockSpec((SEQ, qw), lambda b, p: (b, p)),
                  pl.BlockSpec((SEQ, LANES), lambda b, p: (b, k0 + p)),
                  pl.BlockSpec((SEQ, LANES), lambda b, p: (b, v0 + p))],
        out_specs=(pl.BlockSpec((SEQ, qw), lambda b, p: (b, p)), kv_out, kv_out),
        compiler_params=_params(("arbitrary", "arbitrary"), 40),
        name="gqa_ctx",
    )(sink, qkv, qkv, qkv)

    table = pl.BlockSpec((DEC_SEQ, LANES), lambda b, p: (0, 0))
    past = pl.BlockSpec((PAST_LEN, LANES), lambda b, p: (b, p))
    o_s = pl.pallas_call(
        _d_lat_kernel,
        out_shape=jax.ShapeDtypeStruct((N_SAMPLE, D_MODEL), BF16),
        grid=(DEC_BATCH, n_pairs),
        in_specs=[smem, table, table, table,
                  pl.BlockSpec((DEC_SEQ, qw), lambda b, p: (SAMPLE_ROW0 + b, p)),
                  pl.BlockSpec((DEC_SEQ, LANES), lambda b, p: (SAMPLE_ROW0 + b, k0 + p)),
                  pl.BlockSpec((DEC_SEQ, LANES), lambda b, p: (SAMPLE_ROW0 + b, v0 + p)),
                  past, past],
        out_specs=pl.BlockSpec((DEC_SEQ, qw), lambda b, p: (b, p)),
        scratch_shapes=[pltpu.VMEM((DEC_SEQ, LANES), BF16), pltpu.VMEM((DEC_SEQ, 2 * LANES), BF16),
                        pltpu.VMEM((PAST_LEN, LANES), BF16), pltpu.VMEM((PAST_LEN, 2 * LANES), BF16),
                        pltpu.VMEM((2, 2, D_GROUP * D_QBLOCK, D_BAND + PAST_LEN), BF16),
                        pltpu.VMEM((2, 2, D_GROUP * D_QBLOCK, 1), F32)],
        compiler_params=_params(("arbitrary", "arbitrary"), 40),
        name="gqa_lat",
    )(sink, *rope, qkv, qkv, qkv,
      cache_k.reshape(DEC_BATCH * PAST_LEN, kvw), cache_v.reshape(DEC_BATCH * PAST_LEN, kvw))
    return o_p, o_s, k_new, v_new


def kernel(x_prompt, x_sample, cache_a_k, cache_a_v, state_b_C, state_b_n, state_b_m, cache_c_k, cache_c_v,
           cache_d_k, cache_d_v, c, c_ctx, w_mod, b_mod, g_norm, w_ff1, w_ff2, a_w_in, a_w_out, a_lambda,
           a_subln, b_w_in, b_gate_bias, b_w_out, b_norm, c_w_in, c_w_out, c_rpb, d_w_in, d_w_out, d_sink):
    x = jnp.concatenate([x_prompt.reshape(N_PROMPT, D_MODEL), x_sample.reshape(N_SAMPLE, D_MODEL)], axis=0)
    cond = jnp.concatenate([c_ctx[None, :], c, jnp.zeros((N_COND - 1 - DEC_BATCH, D_MODEL), F32)], axis=0)
    mod = _modulation(cond, w_mod, b_mod).reshape(DEPTH, N_COND, N_MOD, 1, D_MODEL)
    gains = g_norm.reshape(DEPTH, 4, 1, D_MODEL)
    rope = _rope_tables()
    new = {name: [] for name in ("a_k", "a_v", "b_C", "b_n", "b_m", "c_k", "c_v", "d_k", "d_v")}

    h = _adaln(x, gains, mod, 0)
    for i in range(DEPTH):
        kind, j = i % N_MIXERS, i // N_MIXERS
        if kind == 0:
            lam_init = 0.8 - 0.6 * math.exp(-0.3 * i)
            qkv = _project(h, a_w_in[j])
            o_p, o_s, k_new, v_new = _mixer_a(qkv, cache_a_k[:, j], cache_a_v[:, j], a_lambda[j], a_subln[j],
                                              lam_init, rope)
            w_out = a_w_out[j]
            new["a_k"].append(k_new.reshape(BATCH, SEQ, A_HEADS, A_V_DIM))
            new["a_v"].append(v_new.reshape(BATCH, SEQ, A_HEADS, A_V_DIM))
        elif kind == 1:
            w_in = b_w_in[j]
            z = _project(h, w_in, n=B_MAIN)
            w_gate = jnp.pad(w_in[:, B_MAIN:], ((0, 0), (0, LANES - 4 * B_HEADS)))
            gates = _project(h, w_gate)
            o_p, o_s, c_new, n_new, m_new = _mixer_b(z, gates, state_b_C[:, j], state_b_n[:, j], state_b_m[:, j],
                                                     b_gate_bias[j], b_norm[j])
            w_out = b_w_out[j]
            new["b_C"].append(c_new)
            new["b_n"].append(n_new)
            new["b_m"].append(m_new)
        elif kind == 2:
            qkv = _project(h, c_w_in[j])
            o_p, o_s, k_new, v_new = _mixer_c(qkv, cache_c_k[:, j], cache_c_v[:, j], c_rpb[j])
            w_out = c_w_out[j]
            new["c_k"].append(k_new.reshape(BATCH, SEQ, C_HEADS, C_HEAD_DIM))
            new["c_v"].append(v_new.reshape(BATCH, SEQ, C_HEADS, C_HEAD_DIM))
        else:
            qkv = _project(h, d_w_in[j])
            o_p, o_s, k_new, v_new = _mixer_d(qkv, cache_d_k[:, j], cache_d_v[:, j], d_sink[j], rope)
            w_out = d_w_out[j]
            new["d_k"].append(k_new.reshape(BATCH, SEQ, D_KV_HEADS, D_HEAD_DIM))
            new["d_v"].append(v_new.reshape(BATCH, SEQ, D_KV_HEADS, D_HEAD_DIM))
        x, h = _out_project(o_p, o_s, w_out, x, gains, mod, i)
        y = _mlp(h, w_ff1, w_ff2, i)
        x, h = _residual(x, y, gains, mod, i, 5, 3, (i + 1, 0, 0, 1) if i + 1 < DEPTH else None)

    x_p, x_s = x
    stack = lambda name: jnp.stack(new[name], axis=1)
    return (x_p.reshape(BATCH, SEQ, D_MODEL), x_s.reshape(DEC_BATCH, DEC_SEQ, D_MODEL),
            stack("a_k"), stack("a_v"), stack("b_C"), stack("b_n"), stack("b_m"),
            stack("c_k"), stack("c_v"), stack("d_k"), stack("d_v"))
```

```python
import functools
import math

import numpy as np
import jax
import jax.numpy as jnp
from jax import lax
from jax.experimental import pallas as pl
from jax.experimental.pallas import tpu as pltpu

D_MODEL = 2048
BATCH = 16
SEQ = 256
DEPTH = 4
DEC_BATCH = 4
DEC_SEQ = 1024
PAST_LEN = 512
GRID_W = 64
N_MIXERS = 4
D_FF = 4 * D_MODEL
N_MOD = 6
RMS_EPS = 1e-6
ROPE_THETA = 10000.0

A_HEADS = 16
A_HALF_DIM = 64
A_V_DIM = 128
B_HEADS = 8
B_DK = 128
B_DV = 256
B_QK = B_HEADS * B_DK
B_VD = B_HEADS * B_DV
B_MAIN = 2 * B_QK + 2 * B_VD
C_HEADS = 16
C_HEAD_DIM = 128
NA_ROWS = 8
NA_COLS = 16
GRID_H = DEC_SEQ // GRID_W
D_HEADS = 32
D_KV_HEADS = 8
D_GROUP = 4
D_HEAD_DIM = 64
WINDOW = 128

N_PROMPT = BATCH * SEQ
N_SAMPLE = DEC_BATCH * DEC_SEQ
N_TOK = N_PROMPT + N_SAMPLE
N_COND = 8
SAMPLE_ROW0 = N_PROMPT // DEC_SEQ

LANES = 128
HALF = LANES // 2
NEG = -1e30
MIB = 1024 * 1024
F32 = jnp.float32
BF16 = jnp.bfloat16


def _params(sem, vmem_mib):
    return pltpu.CompilerParams(dimension_semantics=sem, vmem_limit_bytes=vmem_mib * MIB)


def _dot(a, b):
    return jnp.dot(a, b, preferred_element_type=F32)


def _dot_nt(a, b):
    return lax.dot_general(a, b, (((1,), (1,)), ((), ())), preferred_element_type=F32)


def _dot_tn(a, b):
    return lax.dot_general(a, b, (((0,), (0,)), ((), ())), preferred_element_type=F32)


def _rms(x, g):
    return x * lax.rsqrt(jnp.mean(x * x, axis=-1, keepdims=True) + RMS_EPS) * g


def _cond_index(row0):
    return jnp.where(row0 < N_PROMPT, 0, 1 + (row0 - N_PROMPT) // DEC_SEQ)


def _mod_kernel(c_ref, w_ref, b_ref, o_ref):
    c = c_ref[...]
    s = (c / (1.0 + jnp.exp(-c))).astype(BF16)
    o_ref[...] = _dot(s, w_ref[...].astype(BF16)) + b_ref[...]


def _modulation(cond, w_mod, b_mod):
    tn = 1024
    n = N_MOD * D_MODEL
    return pl.pallas_call(
        _mod_kernel,
        out_shape=jax.ShapeDtypeStruct((DEPTH, N_COND, n), F32),
        grid=(DEPTH, n // tn),
        in_specs=[
            pl.BlockSpec((N_COND, D_MODEL), lambda l, j: (0, 0)),
            pl.BlockSpec((None, D_MODEL, tn), lambda l, j: (l, 0, j)),
            pl.BlockSpec((None, 1, tn), lambda l, j: (l, 0, j)),
        ],
        out_specs=pl.BlockSpec((None, N_COND, tn), lambda l, j: (l, 0, j)),
        compiler_params=_params(("arbitrary", "arbitrary"), 40),
        name="modulation",
    )(cond, w_mod, b_mod.reshape(DEPTH, 1, n))


ROW_TILE = 512


def _mod_spec(layer, which, tile0=0):
    return pl.BlockSpec((None, None, None, 1, D_MODEL),
                        lambda i: (layer, _cond_index((i + tile0) * ROW_TILE), which, 0, 0))


def _gain_spec(layer, which):
    return pl.BlockSpec((None, None, 1, D_MODEL), lambda i: (layer, which, 0, 0))


def _adaln_kernel(xp_ref, xs_ref, g_ref, shift_ref, scale_ref, x_ref, h_ref):
    def emit(x):
        x_ref[...] = x
        h_ref[...] = (_rms(x, g_ref[...]) * (1.0 + scale_ref[...]) + shift_ref[...]).astype(BF16)

    @pl.when(pl.program_id(0) < N_PROMPT // ROW_TILE)
    def _():
        emit(xp_ref[...])

    @pl.when(pl.program_id(0) >= N_PROMPT // ROW_TILE)
    def _():
        emit(xs_ref[...])


def _adaln(x_prompt, x_sample, gains, mod, layer):
    p_tiles = N_PROMPT // ROW_TILE
    row = pl.BlockSpec((ROW_TILE, D_MODEL), lambda i: (i, 0))
    return pl.pallas_call(
        _adaln_kernel,
        out_shape=(jax.ShapeDtypeStruct((N_TOK, D_MODEL), F32), jax.ShapeDtypeStruct((N_TOK, D_MODEL), BF16)),
        grid=(N_TOK // ROW_TILE,),
        in_specs=[pl.BlockSpec((ROW_TILE, D_MODEL), lambda i: (jnp.minimum(i, p_tiles - 1), 0)),
                  pl.BlockSpec((ROW_TILE, D_MODEL), lambda i: (jnp.maximum(i - p_tiles, 0), 0)),
                  _gain_spec(layer, 0), _mod_spec(layer, 0), _mod_spec(layer, 1)],
        out_specs=(row, row),
        compiler_params=_params(("arbitrary",), 40),
        name="adaln",
    )(x_prompt.reshape(N_PROMPT, D_MODEL), x_sample.reshape(N_SAMPLE, D_MODEL), gains, mod, mod)


def _residual_adaln_kernel(x_ref, y_ref, gate_ref, gpost_ref, gpre_ref, shift_ref, scale_ref, xo_ref, h_ref):
    x = x_ref[...] + gate_ref[...] * _rms(y_ref[...], gpost_ref[...])
    xo_ref[...] = x
    h = _rms(x, gpre_ref[...]) * (1.0 + scale_ref[...]) + shift_ref[...]
    h_ref[...] = h.astype(BF16)


def _residual_kernel(x_ref, y_ref, gate_ref, gpost_ref, xo_ref):
    xo_ref[...] = x_ref[...] + gate_ref[...] * _rms(y_ref[...], gpost_ref[...])


def _residual(x, y, gains, mod, layer, gate_idx, post_idx, nxt):
    row = pl.BlockSpec((ROW_TILE, D_MODEL), lambda i: (i, 0))
    if nxt is None:
        outs = []
        for tile0, n_rows in ((0, N_PROMPT), (N_PROMPT // ROW_TILE, N_SAMPLE)):
            src = pl.BlockSpec((ROW_TILE, D_MODEL), lambda i, t=tile0: (i + t, 0))
            outs.append(pl.pallas_call(
                _residual_kernel, out_shape=jax.ShapeDtypeStruct((n_rows, D_MODEL), F32),
                grid=(n_rows // ROW_TILE,),
                in_specs=[src, src, _mod_spec(layer, gate_idx, tile0), _gain_spec(layer, post_idx)],
                out_specs=row,
                compiler_params=_params(("arbitrary",), 32), name="residual",
            )(x, y, mod, gains))
        return outs, None
    nl, ng, nshift, nscale = nxt
    return pl.pallas_call(
        _residual_adaln_kernel,
        out_shape=(jax.ShapeDtypeStruct((N_TOK, D_MODEL), F32), jax.ShapeDtypeStruct((N_TOK, D_MODEL), BF16)),
        grid=(N_TOK // ROW_TILE,),
        in_specs=[row, row, _mod_spec(layer, gate_idx), _gain_spec(layer, post_idx),
                  _gain_spec(nl, ng), _mod_spec(nl, nshift), _mod_spec(nl, nscale)],
        out_specs=(row, row),
        compiler_params=_params(("arbitrary",), 40), name="residual_adaln",
    )(x, y, mod, gains, gains, mod, mod)


PROJ_TM = 1024


def _proj_kernel(a_ref, w_ref, o_ref, wbf_ref):
    @pl.when(pl.program_id(1) == 0)
    def _():
        wbf_ref[...] = w_ref[...].astype(BF16)

    o_ref[...] = _dot(a_ref[...], wbf_ref[...]).astype(o_ref.dtype)


def _project(a, w, n=None, out_dtype=F32):
    k = w.shape[0]
    n = w.shape[1] if n is None else n
    tm = PROJ_TM
    tn = min(n, 1024)
    return pl.pallas_call(
        _proj_kernel,
        out_shape=jax.ShapeDtypeStruct((N_TOK, n), out_dtype),
        grid=(n // tn, N_TOK // tm),
        in_specs=[pl.BlockSpec((tm, k), lambda j, i: (i, 0)),
                  pl.BlockSpec((k, tn), lambda j, i: (0, j))],
        out_specs=pl.BlockSpec((tm, tn), lambda j, i: (i, j)),
        scratch_shapes=[pltpu.VMEM((k, tn), BF16)],
        compiler_params=_params(("arbitrary", "arbitrary"), 48),
        name="project",
    )(a, w)


OUT_TM = 512


def _out_proj_kernel(ap_ref, as_ref, w_ref, x_ref, gate_ref, gpost_ref, gpre_ref, shift_ref, scale_ref,
                     xo_ref, h_ref):
    def finish(a):
        x = x_ref[...] + gate_ref[...] * _rms(_dot(a, w_ref[...]), gpost_ref[...])
        xo_ref[...] = x
        h_ref[...] = (_rms(x, gpre_ref[...]) * (1.0 + scale_ref[...]) + shift_ref[...]).astype(BF16)

    @pl.when(pl.program_id(0) < N_PROMPT // OUT_TM)
    def _():
        finish(ap_ref[...])

    @pl.when(pl.program_id(0) >= N_PROMPT // OUT_TM)
    def _():
        finish(as_ref[...])


def _out_project(o_p, o_s, w_out, x, gains, mod, layer):
    tm = OUT_TM
    p_tiles = N_PROMPT // tm
    row = pl.BlockSpec((tm, D_MODEL), lambda i: (i, 0))

    def mod_spec(which):
        return pl.BlockSpec((None, None, None, 1, D_MODEL), lambda i: (layer, _cond_index(i * tm), which, 0, 0))

    return pl.pallas_call(
        _out_proj_kernel,
        out_shape=(jax.ShapeDtypeStruct((N_TOK, D_MODEL), F32), jax.ShapeDtypeStruct((N_TOK, D_MODEL), BF16)),
        grid=(N_TOK // tm,),
        in_specs=[pl.BlockSpec((tm, D_MODEL), lambda i: (jnp.minimum(i, p_tiles - 1), 0)),
                  pl.BlockSpec((tm, D_MODEL), lambda i: (jnp.maximum(i - p_tiles, 0), 0)),
                  pl.BlockSpec((D_MODEL, D_MODEL), lambda i: (0, 0)),
                  row, mod_spec(2), _gain_spec(layer, 1), _gain_spec(layer, 2), mod_spec(3), mod_spec(4)],
        out_specs=(row, row),
        compiler_params=_params(("arbitrary",), 56),
        name="out_project",
    )(o_p, o_s, w_out.astype(BF16), x, mod, gains, gains, mod, mod)


def _mlp_kernel(h_ref, w1_ref, w2_ref, o_ref, u_ref):
    f = pl.program_id(1)
    n_tiles = pl.num_programs(1) - 1

    def up(slot):
        u = jnp.maximum(_dot(h_ref[...], w1_ref[...].astype(BF16)), 0.0)
        u_ref[slot] = (u * u).astype(BF16)

    def down(slot):
        o_ref[...] += _dot(u_ref[slot], w2_ref[...].astype(BF16))

    @pl.when(f == 0)
    def _():
        o_ref[...] = jnp.zeros_like(o_ref)
        up(0)

    for parity in (0, 1):
        @pl.when((f > 0) & (f < n_tiles) & (f % 2 == parity))
        def _():
            down(1 - parity)
            up(parity)

    @pl.when(f == n_tiles)
    def _():
        down((MLP_TILES - 1) % 2)


MLP_TM, MLP_TF = 1024, 512
MLP_TILES = D_FF // MLP_TF


def _mlp(h, w1, w2, layer):
    tm, tf = MLP_TM, MLP_TF
    return pl.pallas_call(
        _mlp_kernel,
        out_shape=jax.ShapeDtypeStruct((N_TOK, D_MODEL), F32),
        grid=(N_TOK // tm, MLP_TILES + 1),
        in_specs=[pl.BlockSpec((tm, D_MODEL), lambda i, f: (i, 0)),
                  pl.BlockSpec((None, D_MODEL, tf), lambda i, f: (layer, 0, jnp.minimum(f, MLP_TILES - 1))),
                  pl.BlockSpec((None, tf, D_MODEL), lambda i, f: (layer, jnp.maximum(f - 1, 0), 0))],
        out_specs=pl.BlockSpec((tm, D_MODEL), lambda i, f: (i, 0)),
        scratch_shapes=[pltpu.VMEM((2, tm, tf), BF16)],
        compiler_params=_params(("arbitrary", "arbitrary"), 58),
        name="mlp",
    )(h, w1, w2)


def _rope_tables():
    t = jnp.arange(DEC_SEQ)
    lane = np.arange(LANES)
    f = lane % 32
    first = f < 16
    inv = ROPE_THETA ** (-jnp.arange(16, dtype=F32) / 16)
    pos = jnp.where((lane % 64 < 32)[None, :], (t // GRID_W)[:, None], (t % GRID_W)[:, None]).astype(F32)
    ang = pos * inv[f % 16][None, :]
    cos, sin = jnp.cos(ang), jnp.sin(ang)
    sin_next = jnp.where(first[None, :], -sin, 0.0)
    sin_prev = jnp.where(first[None, :], 0.0, sin)
    return cos, sin_next, sin_prev


def _rope(x, cos, sin_next, sin_prev):
    return (x * cos + pltpu.roll(x, LANES - 16, axis=1) * sin_next
            + pltpu.roll(x, 16, axis=1) * sin_prev)


def _lane_lo(shape):
    return lax.broadcasted_iota(jnp.int32, shape, 1) < HALF


def _with_ones(v):
    return jnp.concatenate([v.astype(BF16), jnp.ones(v.shape, BF16)], axis=1)


def _attend(scores, vals_ones, sink=None):
    m = None
    for s in scores:
        mi = jnp.max(s, axis=-1, keepdims=True)
        m = mi if m is None else jnp.maximum(m, mi)
    if sink is not None:
        m = jnp.maximum(m, sink)
    acc = None
    for s, v in zip(scores, vals_ones):
        part = _dot(jnp.exp(s - m).astype(BF16), v)
        acc = part if acc is None else acc + part
    den = acc[:, LANES:]
    if sink is not None:
        den = den + jnp.exp(sink - m)
    return acc[:, :LANES] / den


CTX_HEADS = 4


A_SCALE = A_HALF_DIM ** -0.5


def _diff_lambda(lp, lam_init):
    a = jnp.sum(lp[0] * lp[1], axis=-1, keepdims=True)
    b = jnp.sum(lp[2] * lp[3], axis=-1, keepdims=True)
    return jnp.exp(a) - jnp.exp(b) + lam_init


def _diff_finish(o, subln_ref, lam_init):
    return (_rms(o, subln_ref[...]) * (1.0 - lam_init)).astype(BF16)


def _a_ctx_kernel(lam_ref, subln_ref, q_ref, k_ref, v_ref, o_ref, ko_ref, vo_ref, *, lam_init):
    lo = _lane_lo((SEQ, LANES))
    for hh in range(CTX_HEADS):
        cols = slice(hh * LANES, (hh + 1) * LANES)
        q = q_ref[:, cols] * A_SCALE
        k = k_ref[:, cols]
        v = v_ref[:, cols]
        ko_ref[:, cols] = k
        vo_ref[:, cols] = v
        kb, v1 = k.astype(BF16), _with_ones(v)
        o1 = _attend([_dot_nt(jnp.where(lo, q, 0.0).astype(BF16), kb)], [v1])
        o2 = _attend([_dot_nt(jnp.where(lo, 0.0, q).astype(BF16), kb)], [v1])
        lam = _diff_lambda(lam_ref[:, hh], lam_init)
        o_ref[:, cols] = _diff_finish(o1 - lam * o2, subln_ref, lam_init)


A_LAT_TQ = 256


def _a_lat_kernel(lam_ref, subln_ref, cos_ref, nxt_ref, prv_ref, q_ref, k_ref, v_ref, pk_ref, pv_ref, o_ref,
                  kb_ref, v1_ref, e_ref, *, lam_init):
    kb_ref[:PAST_LEN, :] = pk_ref[...].astype(BF16)
    kb_ref[PAST_LEN:, :] = _rope(k_ref[...], cos_ref[...], nxt_ref[...], prv_ref[...]).astype(BF16)
    v1_ref[:PAST_LEN, :] = _with_ones(pv_ref[...])
    v1_ref[PAST_LEN:, :] = _with_ones(v_ref[...])
    lam = _diff_lambda(lam_ref[...], lam_init)
    lo = _lane_lo((A_LAT_TQ, LANES))
    n_blocks = DEC_SEQ // A_LAT_TQ

    def weights(i):
        rows = slice(i * A_LAT_TQ, (i + 1) * A_LAT_TQ)
        q = _rope(q_ref[rows, :], cos_ref[rows, :], nxt_ref[rows, :], prv_ref[rows, :]) * A_SCALE
        for half, keep in enumerate((lo, ~lo)):
            s = _dot_nt(jnp.where(keep, q, 0.0).astype(BF16), kb_ref[...])
            e_ref[i % 2, half] = jnp.exp(s - jnp.max(s, axis=-1, keepdims=True)).astype(BF16)

    def values(i):
        rows = slice(i * A_LAT_TQ, (i + 1) * A_LAT_TQ)
        outs = []
        for half in (0, 1):
            acc = _dot(e_ref[i % 2, half], v1_ref[...])
            outs.append(acc[:, :LANES] / acc[:, LANES:])
        o_ref[rows, :] = _diff_finish(outs[0] - lam * outs[1], subln_ref, lam_init)

    weights(0)
    for i in range(1, n_blocks):
        values(i - 1)
        weights(i)
    values(n_blocks - 1)


def _mixer_a(qkv, cache_k, cache_v, lam_p, subln, lam_init, rope):
    nh = A_HEADS
    lam4 = lam_p.reshape(4, nh, 1, A_HALF_DIM)
    sub2 = subln.reshape(1, A_V_DIM)
    cw = CTX_HEADS * LANES
    nt = nh // CTX_HEADS
    ctx_out = pl.BlockSpec((SEQ, cw), lambda b, h: (b, h))
    kv_shape = jax.ShapeDtypeStruct((N_PROMPT, nh * A_V_DIM), F32)
    o_p, k_new, v_new = pl.pallas_call(
        functools.partial(_a_ctx_kernel, lam_init=lam_init),
        out_shape=(jax.ShapeDtypeStruct((N_PROMPT, nh * A_V_DIM), BF16), kv_shape, kv_shape),
        grid=(BATCH, nt),
        in_specs=[pl.BlockSpec((4, CTX_HEADS, 1, A_HALF_DIM), lambda b, h: (0, h, 0, 0)),
                  pl.BlockSpec((1, A_V_DIM), lambda b, h: (0, 0)),
                  pl.BlockSpec((SEQ, cw), lambda b, h: (b, h)),
                  pl.BlockSpec((SEQ, cw), lambda b, h: (b, nt + h)),
                  pl.BlockSpec((SEQ, cw), lambda b, h: (b, 2 * nt + h))],
        out_specs=(ctx_out, ctx_out, ctx_out),
        compiler_params=_params(("arbitrary", "arbitrary"), 32),
        name="diff_attn_ctx",
    )(lam4, sub2, qkv, qkv, qkv)

    table = pl.BlockSpec((DEC_SEQ, LANES), lambda b, h: (0, 0))
    past = pl.BlockSpec((PAST_LEN, LANES), lambda b, h: (b, h))
    o_s = pl.pallas_call(
        functools.partial(_a_lat_kernel, lam_init=lam_init),
        out_shape=jax.ShapeDtypeStruct((N_SAMPLE, nh * A_V_DIM), BF16),
        grid=(DEC_BATCH, nh),
        in_specs=[pl.BlockSpec((4, None, 1, A_HALF_DIM), lambda b, h: (0, h, 0, 0)),
                  pl.BlockSpec((1, A_V_DIM), lambda b, h: (0, 0)),
                  table, table, table,
                  pl.BlockSpec((DEC_SEQ, LANES), lambda b, h: (SAMPLE_ROW0 + b, h)),
                  pl.BlockSpec((DEC_SEQ, LANES), lambda b, h: (SAMPLE_ROW0 + b, nh + h)),
                  pl.BlockSpec((DEC_SEQ, LANES), lambda b, h: (SAMPLE_ROW0 + b, 2 * nh + h)),
                  past, past],
        out_specs=pl.BlockSpec((DEC_SEQ, LANES), lambda b, h: (b, h)),
        scratch_shapes=[pltpu.VMEM((PAST_LEN + DEC_SEQ, LANES), BF16),
                        pltpu.VMEM((PAST_LEN + DEC_SEQ, 2 * LANES), BF16),
                        pltpu.VMEM((2, 2, A_LAT_TQ, PAST_LEN + DEC_SEQ), BF16)],
        compiler_params=_params(("arbitrary", "arbitrary"), 40),
        name="diff_attn_lat",
    )(lam4, sub2, *rope, qkv, qkv, qkv,
      cache_k.reshape(DEC_BATCH * PAST_LEN, nh * A_V_DIM), cache_v.reshape(DEC_BATCH * PAST_LEN, nh * A_V_DIM))
    return o_p, o_s, k_new, v_new


B_CHUNK = 256


def _log_sigmoid(x):
    return jnp.minimum(x, 0.0) - jnp.log(1.0 + jnp.exp(-jnp.abs(x)))


def _mlstm_kernel(*refs, n_chunks, has_init, emit_state):
    it = iter(refs)
    q_ref, k_ref, v_ref, og_ref, gcol_ref, grow_ref, bcol_ref, brow_ref, nw_ref = (next(it) for _ in range(9))
    if has_init:
        c0_ref, n0_ref, m0_ref = next(it), next(it), next(it)
    h_ref = next(it)
    if emit_state:
        c_out, n_out, m_out = next(it), next(it), next(it)
    hs_ref = next(it)

    L = B_CHUNK
    ti = lax.broadcasted_iota(jnp.int32, (L, L), 0)
    si = lax.broadcasted_iota(jnp.int32, (L, L), 1)
    gcol = gcol_ref[...] + brow_ref[...]
    grow = grow_ref[...] + bcol_ref[...]
    qscale = B_DK ** -0.5

    for d in (0, 1):
        valid = (si <= ti) if d == 0 else (si >= ti)
        valid_t = (ti <= si) if d == 0 else (ti >= si)
        last = L - 1 if d == 0 else 0
        if has_init:
            C, n, m = c0_ref[d], n0_ref[d], m0_ref[d][:, :1]
        else:
            C, n, m = jnp.zeros((B_DV, B_DK), F32), jnp.zeros((1, B_DK), F32), jnp.zeros((1, 1), F32)
        for ci in range(n_chunks):
            c = ci if d == 0 else n_chunks - 1 - ci
            rows = slice(c * L, (c + 1) * L)
            q = q_ref[rows, :] * qscale
            k = k_ref[rows, :]
            v = v_ref[rows, :]
            qb, kb = q.astype(BF16), k.astype(BF16)
            ig_col = gcol[rows, 2 * d:2 * d + 1]
            lf_col = _log_sigmoid(gcol[rows, 2 * d + 1:2 * d + 2])
            ig_row = grow[2 * d:2 * d + 1, rows]
            lf_row = _log_sigmoid(grow[2 * d + 1:2 * d + 2, rows])
            b_col = jnp.sum(jnp.where(valid, lf_row, 0.0), axis=1, keepdims=True)
            b_row = jnp.sum(jnp.where(valid_t, lf_col, 0.0), axis=0, keepdims=True)
            dm = jnp.where(valid, b_col - b_row + ig_row, NEG)
            inter = b_col + m
            m_t = jnp.maximum(inter, jnp.max(dm, axis=1, keepdims=True))
            w = jnp.exp(dm - m_t)
            a_in = jnp.exp(inter - m_t)
            sw = _dot_nt(qb, kb) * w
            num = _dot(sw.astype(BF16), v.astype(BF16)) + a_in * _dot_nt(qb, C.astype(BF16))
            den = jnp.sum(sw, axis=1, keepdims=True) + a_in * jnp.sum(q * n, axis=1, keepdims=True)
            h = num / jnp.maximum(jnp.abs(den), jnp.exp(-m_t))
            if d == 0:
                hs_ref[rows, :] = h
            else:
                hs_ref[rows, :] += h
            if emit_state or ci + 1 < n_chunks:
                m_last = m_t[last:last + 1, :]
                al = a_in[last:last + 1, :]
                wl = jnp.exp(b_col[last:last + 1, :] - b_col + ig_col - m_last)
                C = al * C + _dot_tn((v * wl).astype(BF16), kb)
                n = al * n + jnp.sum(wl * k, axis=0, keepdims=True)
                m = m_last
        if emit_state:
            c_out[d] = C
            n_out[d] = n
            m_out[d] = jnp.broadcast_to(m, (1, LANES))

    og = og_ref[...]
    h_ref[...] = (_rms(hs_ref[...], nw_ref[...]) * (1.0 / (1.0 + jnp.exp(-og)))).astype(BF16)


def _mlstm_call(z, gcol, grow, bias, norm_w, seq, batch, row0, init):
    nh = B_HEADS
    has_init = init is not None
    emit_state = not has_init
    bcol = bias.reshape(2, 2, nh).transpose(2, 0, 1).reshape(nh, 4, 1)
    brow = bcol.reshape(nh, 1, 4)
    qk_tiles = B_QK // B_DK
    vd0 = 2 * B_QK // B_DV
    in_specs = [
        pl.BlockSpec((seq, B_DK), lambda b, h: (row0 + b, h)),
        pl.BlockSpec((seq, B_DK), lambda b, h: (row0 + b, qk_tiles + h)),
        pl.BlockSpec((seq, B_DV), lambda b, h: (row0 + b, vd0 + h)),
        pl.BlockSpec((seq, B_DV), lambda b, h: (row0 + b, vd0 + nh + h)),
        pl.BlockSpec((None, seq, 4), lambda b, h: (h, row0 + b, 0)),
        pl.BlockSpec((None, 4, seq), lambda b, h: (h, 0, row0 + b)),
        pl.BlockSpec((None, 4, 1), lambda b, h: (h, 0, 0)),
        pl.BlockSpec((None, 1, 4), lambda b, h: (h, 0, 0)),
        pl.BlockSpec((1, B_DV), lambda b, h: (0, h)),
    ]
    args = [z, z, z, z, gcol, grow, bcol, brow, norm_w.reshape(1, B_VD)]
    state_c = pl.BlockSpec((None, 2, None, B_DV, B_DK), lambda b, h: (b, 0, h, 0, 0))
    state_v = pl.BlockSpec((None, 2, None, 1, B_DK), lambda b, h: (b, 0, h, 0, 0))
    if has_init:
        c0, n0, m0 = init
        in_specs += [state_c, state_v, state_v]
        args += [c0, n0.reshape(batch, 2, nh, 1, B_DK),
                 jnp.broadcast_to(m0[..., None, None], (batch, 2, nh, 1, B_DK))]
    h_shape = jax.ShapeDtypeStruct((batch * seq, B_VD), BF16)
    h_spec = pl.BlockSpec((seq, B_DV), lambda b, h: (b, h))
    if emit_state:
        out_shape = (h_shape,
                     jax.ShapeDtypeStruct((batch, 2, nh, B_DV, B_DK), F32),
                     jax.ShapeDtypeStruct((batch, 2, nh, 1, B_DK), F32),
                     jax.ShapeDtypeStruct((batch, 2, nh, 1, B_DK), F32))
        out_specs = (h_spec, state_c, state_v, state_v)
    else:
        out_shape, out_specs = h_shape, h_spec
    return pl.pallas_call(
        functools.partial(_mlstm_kernel, n_chunks=seq // B_CHUNK, has_init=has_init, emit_state=emit_state),
        out_shape=out_shape,
        grid=(batch, nh),
        in_specs=in_specs, out_specs=out_specs,
        scratch_shapes=[pltpu.VMEM((seq, B_DV), F32)],
        compiler_params=_params(("arbitrary", "arbitrary"), 48),
        name="mlstm_ctx" if emit_state else "mlstm_lat",
    )(*args)


def _mixer_b(z, gates, state_c, state_n, state_m, bias, norm_w):
    nh = B_HEADS
    g = gates[:, :4 * nh].reshape(N_TOK, 4, nh)
    gcol = g.transpose(2, 0, 1)
    grow = g.transpose(2, 1, 0)
    o_p, c_new, n_new, m_new = _mlstm_call(z, gcol, grow, bias, norm_w, SEQ, BATCH, 0, None)
    o_s = _mlstm_call(z, gcol, grow, bias, norm_w, DEC_SEQ, DEC_BATCH, SAMPLE_ROW0,
                      (state_c, state_n, state_m))
    return o_p, o_s, c_new, n_new.reshape(BATCH, 2, nh, B_DK), m_new[:, :, :, 0, 0]


C_SCALE = C_HEAD_DIM ** -0.5


def _c_ctx_kernel(q_ref, k_ref, v_ref, o_ref, ko_ref, vo_ref):
    for hh in range(CTX_HEADS):
        cols = slice(hh * LANES, (hh + 1) * LANES)
        k = k_ref[:, cols]
        v = v_ref[:, cols]
        ko_ref[:, cols] = k
        vo_ref[:, cols] = v
        s = _dot_nt(q_ref[:, cols].astype(BF16), k.astype(BF16)) * C_SCALE
        o_ref[:, cols] = _attend([s], [_with_ones(v)]).astype(BF16)


def _na_row_start(r):
    return min(max(r - NA_ROWS // 2, 0), GRID_H - NA_ROWS)


def _na_row_groups():
    groups = []
    for r in range(GRID_H):
        if groups and _na_row_start(groups[-1][0]) == _na_row_start(r):
            groups[-1].append(r)
        else:
            groups.append([r])
    return groups


def _c_lat_kernel(bias_ref, q_ref, k_ref, v_ref, pk_ref, pv_ref, o_ref, kb_ref, v1_ref, sc_ref, ec_ref, acc_ref):
    n_win = NA_ROWS * GRID_W
    kb_ref[...] = k_ref[...].astype(BF16)
    v1_ref[...] = _with_ones(v_ref[...])
    sc_ref[...] = _dot_nt(q_ref[...].astype(BF16), pk_ref[...].astype(BF16)) * C_SCALE
    for rows_g in _na_row_groups():
        r0 = _na_row_start(rows_g[0])
        rows = slice(rows_g[0] * GRID_W, (rows_g[-1] + 1) * GRID_W)
        win = slice(r0 * GRID_W, r0 * GRID_W + n_win)
        strips = []
        for r in rows_g:
            strip = NA_ROWS - 1 - (r - r0)
            even = strip - strip % 2
            strips.append(bias_ref[strip % 2, :, even * GRID_W:even * GRID_W + n_win])
        bias = strips[0] if len(strips) == 1 else jnp.concatenate(strips, axis=0)
        s_n = _dot_nt(q_ref[rows, :].astype(BF16), kb_ref[win, :]) * C_SCALE + bias
        s_c = sc_ref[rows, :]
        m = jnp.maximum(jnp.max(s_n, axis=-1, keepdims=True), jnp.max(s_c, axis=-1, keepdims=True))
        ec_ref[rows, :] = jnp.exp(s_c - m).astype(BF16)
        acc_ref[rows, :] = _dot(jnp.exp(s_n - m).astype(BF16), v1_ref[win, :])
    acc = acc_ref[...] + _dot(ec_ref[...], _with_ones(pv_ref[...]))
    o_ref[...] = (acc[:, :LANES] / acc[:, LANES:]).astype(BF16)


def _na_bias(rpb):
    n_drow, n_dcol = 2 * NA_ROWS - 1, 2 * NA_COLS - 1
    cq = np.arange(GRID_W)[:, None]
    kc = np.arange(GRID_W)[None, :]
    cstart = np.clip(cq - NA_COLS // 2, 0, GRID_W - NA_COLS)
    col_ok = (kc >= cstart) & (kc < cstart + NA_COLS)
    dcol = np.clip(kc - cq, -(NA_COLS - 1), NA_COLS - 1) + NA_COLS - 1
    onehot = (dcol[:, :, None] == np.arange(n_dcol)).astype(np.float32)
    strips = jnp.einsum('hab,qkb->hqak', rpb.astype(F32), onehot, precision=lax.Precision.HIGHEST)
    strips = jnp.where(col_ok[None, :, None, :], strips, NEG).reshape(C_HEADS, GRID_W, n_drow * GRID_W)
    strips = jnp.pad(strips, ((0, 0), (0, 0), (0, 2 * GRID_W)))
    return jnp.stack([strips[:, :, :(n_drow + 1) * GRID_W], strips[:, :, GRID_W:]], axis=1)


def _mixer_c(qkv, cache_k, cache_v, rpb):
    nh = C_HEADS
    cw = CTX_HEADS * LANES
    nt = nh // CTX_HEADS
    ctx_out = pl.BlockSpec((SEQ, cw), lambda b, h: (b, h))
    kv_shape = jax.ShapeDtypeStruct((N_PROMPT, D_MODEL), F32)
    o_p, k_new, v_new = pl.pallas_call(
        _c_ctx_kernel,
        out_shape=(jax.ShapeDtypeStruct((N_PROMPT, D_MODEL), BF16), kv_shape, kv_shape),
        grid=(BATCH, nt),
        in_specs=[pl.BlockSpec((SEQ, cw), lambda b, h: (b, h)),
                  pl.BlockSpec((SEQ, cw), lambda b, h: (b, nt + h)),
                  pl.BlockSpec((SEQ, cw), lambda b, h: (b, 2 * nt + h))],
        out_specs=(ctx_out, ctx_out, ctx_out),
        compiler_params=_params(("arbitrary", "arbitrary"), 32),
        name="na_ctx",
    )(qkv, qkv, qkv)

    past = pl.BlockSpec((PAST_LEN, LANES), lambda h, b: (b, h))
    o_s = pl.pallas_call(
        _c_lat_kernel,
        out_shape=jax.ShapeDtypeStruct((N_SAMPLE, D_MODEL), BF16),
        grid=(nh, DEC_BATCH),
        in_specs=[pl.BlockSpec((None, 2, GRID_W, 2 * NA_ROWS * GRID_W), lambda h, b: (h, 0, 0, 0)),
                  pl.BlockSpec((DEC_SEQ, LANES), lambda h, b: (SAMPLE_ROW0 + b, h)),
                  pl.BlockSpec((DEC_SEQ, LANES), lambda h, b: (SAMPLE_ROW0 + b, nh + h)),
                  pl.BlockSpec((DEC_SEQ, LANES), lambda h, b: (SAMPLE_ROW0 + b, 2 * nh + h)),
                  past, past],
        out_specs=pl.BlockSpec((DEC_SEQ, LANES), lambda h, b: (b, h)),
        scratch_shapes=[pltpu.VMEM((DEC_SEQ, LANES), BF16), pltpu.VMEM((DEC_SEQ, 2 * LANES), BF16),
                        pltpu.VMEM((DEC_SEQ, PAST_LEN), F32), pltpu.VMEM((DEC_SEQ, PAST_LEN), BF16),
                        pltpu.VMEM((DEC_SEQ, 2 * LANES), F32)],
        compiler_params=_params(("arbitrary", "arbitrary"), 40),
        name="na_lat",
    )(_na_bias(rpb), qkv, qkv, qkv,
      cache_k.reshape(DEC_BATCH * PAST_LEN, D_MODEL), cache_v.reshape(DEC_BATCH * PAST_LEN, D_MODEL))
    return o_p, o_s, k_new, v_new


D_SCALE = D_HEAD_DIM ** -0.5
D_QBLOCK = 128
D_BAND = 3 * D_QBLOCK


def _d_ctx_kernel(sink_ref, q_ref, k_ref, v_ref, o_ref, ko_ref, vo_ref):
    pair = pl.program_id(1)
    k = k_ref[...]
    v = v_ref[...]
    ko_ref[...] = k
    vo_ref[...] = v
    kb, v1 = k.astype(BF16), _with_ones(v)
    lo = _lane_lo((SEQ, LANES))
    for tile in range(4):
        kv = tile // 2
        keep = lo if kv == 0 else ~lo
        qt = q_ref[:, tile * LANES:(tile + 1) * LANES] * D_SCALE
        halves = []
        for e in (0, 1):
            qe = qt if e == kv else pltpu.roll(qt, HALF, axis=1)
            s = _dot_nt(jnp.where(keep, qe, 0.0).astype(BF16), kb)
            o = _attend([s], [v1], sink=sink_ref[pair * 2 * D_GROUP + tile * 2 + e])
            halves.append(o if e == kv else pltpu.roll(o, HALF, axis=1))
        o_ref[:, tile * LANES:(tile + 1) * LANES] = jnp.where(lo, halves[0], halves[1]).astype(BF16)


def _d_lat_kernel(sink_ref, cos_ref, nxt_ref, prv_ref, q_ref, k_ref, v_ref, pk_ref, pv_ref, o_ref,
                  kb_ref, v1_ref, pkb_ref, pv1_ref, e_ref, es_ref):
    pair = pl.program_id(1)
    kb_ref[...] = _rope(k_ref[...], cos_ref[...], nxt_ref[...], prv_ref[...]).astype(BF16)
    v1_ref[...] = _with_ones(v_ref[...])
    pkb_ref[...] = pk_ref[...].astype(BF16)
    pv1_ref[...] = _with_ones(pv_ref[...])
    tq = D_QBLOCK
    lo = _lane_lo((tq, LANES))
    n_blocks = DEC_SEQ // tq

    def band_of(bi):
        start = min(max((bi - 1) * tq, 0), DEC_SEQ - D_BAND)
        return start, slice(start, start + D_BAND)

    def weights(bi):
        rows = slice(bi * tq, (bi + 1) * tq)
        start, band = band_of(bi)
        cos, nxt, prv = cos_ref[rows, :], nxt_ref[rows, :], prv_ref[rows, :]
        qpos = bi * tq + lax.broadcasted_iota(jnp.int32, (tq, D_BAND), 0)
        kpos = start + lax.broadcasted_iota(jnp.int32, (tq, D_BAND), 1)
        mask = jnp.where(jnp.abs(qpos - kpos) <= WINDOW, 0.0, NEG)
        mask4 = jnp.concatenate([mask] * D_GROUP, axis=0)
        tiles = [_rope(q_ref[rows, t * LANES:(t + 1) * LANES], cos, nxt, prv) * D_SCALE for t in range(4)]
        for kv in (0, 1):
            keep = lo if kv == 0 else ~lo
            stack, sinks = [], []
            for t in (2 * kv, 2 * kv + 1):
                for e in (0, 1):
                    qe = tiles[t] if e == kv else pltpu.roll(tiles[t], HALF, axis=1)
                    stack.append(jnp.where(keep, qe, 0.0))
                    sinks.append(jnp.full((tq, 1), sink_ref[pair * 2 * D_GROUP + t * 2 + e], F32))
            qs = jnp.concatenate(stack, axis=0).astype(BF16)
            sink = jnp.concatenate(sinks, axis=0)
            s_band = _dot_nt(qs, kb_ref[band, :]) + mask4
            s_past = _dot_nt(qs, pkb_ref[...])
            m = jnp.maximum(jnp.maximum(jnp.max(s_band, axis=-1, keepdims=True),
                                        jnp.max(s_past, axis=-1, keepdims=True)), sink)
            e_ref[bi % 2, kv, :, :D_BAND] = jnp.exp(s_band - m).astype(BF16)
            e_ref[bi % 2, kv, :, D_BAND:] = jnp.exp(s_past - m).astype(BF16)
            es_ref[bi % 2, kv] = jnp.exp(sink - m)

    def values(bi):
        rows = slice(bi * tq, (bi + 1) * tq)
        _, band = band_of(bi)
        outs = {}
        for kv in (0, 1):
            acc = (_dot(e_ref[bi % 2, kv, :, :D_BAND], v1_ref[band, :])
                   + _dot(e_ref[bi % 2, kv, :, D_BAND:], pv1_ref[...]))
            o = acc[:, :LANES] / (acc[:, LANES:] + es_ref[bi % 2, kv])
            for i, t in enumerate((2 * kv, 2 * kv + 1)):
                for e in (0, 1):
                    piece = o[(2 * i + e) * tq:(2 * i + e + 1) * tq, :]
                    outs[(t, e)] = piece if e == kv else pltpu.roll(piece, HALF, axis=1)
        for t in range(4):
            o_ref[rows, t * LANES:(t + 1) * LANES] = jnp.where(lo, outs[(t, 0)], outs[(t, 1)]).astype(BF16)

    weights(0)
    for bi in range(1, n_blocks):
        values(bi - 1)
        weights(bi)
    values(n_blocks - 1)


def _mixer_d(qkv, cache_k, cache_v, sink, rope):
    n_pairs = D_KV_HEADS // 2
    qw = 2 * D_GROUP * D_HEAD_DIM
    k0 = D_HEADS * D_HEAD_DIM // LANES
    v0 = k0 + n_pairs
    kvw = D_KV_HEADS * D_HEAD_DIM
    smem = pl.BlockSpec(memory_space=pltpu.SMEM)
    sink = sink.astype(F32)
    kv_out = pl.BlockSpec((SEQ, LANES), lambda b, p: (b, p))
    kv_shape = jax.ShapeDtypeStruct((N_PROMPT, kvw), F32)
    o_p, k_new, v_new = pl.pallas_call(
        _d_ctx_kernel,
        out_shape=(jax.ShapeDtypeStruct((N_PROMPT, D_MODEL), BF16), kv_shape, kv_shape),
        grid=(BATCH, n_pairs),
        in_specs=[smem,
                  pl.BlockSpec((SEQ, qw), lambda b, p: (b, p)),
                  pl.BlockSpec((SEQ, LANES), lambda b, p: (b, k0 + p)),
                  pl.BlockSpec((SEQ, LANES), lambda b, p: (b, v0 + p))],
        out_specs=(pl.BlockSpec((SEQ, qw), lambda b, p: (b, p)), kv_out, kv_out),
        compiler_params=_params(("arbitrary", "arbitrary"), 40),
        name="gqa_ctx",
    )(sink, qkv, qkv, qkv)

    table = pl.BlockSpec((DEC_SEQ, LANES), lambda b, p: (0, 0))
    past = pl.BlockSpec((PAST_LEN, LANES), lambda b, p: (b, p))
    o_s = pl.pallas_call(
        _d_lat_kernel,
        out_shape=jax.ShapeDtypeStruct((N_SAMPLE, D_MODEL), BF16),
        grid=(DEC_BATCH, n_pairs),
        in_specs=[smem, table, table, table,
                  pl.BlockSpec((DEC_SEQ, qw), lambda b, p: (SAMPLE_ROW0 + b, p)),
                  pl.BlockSpec((DEC_SEQ, LANES), lambda b, p: (SAMPLE_ROW0 + b, k0 + p)),
                  pl.BlockSpec((DEC_SEQ, LANES), lambda b, p: (SAMPLE_ROW0 + b, v0 + p)),
                  past, past],
        out_specs=pl.BlockSpec((DEC_SEQ, qw), lambda b, p: (b, p)),
        scratch_shapes=[pltpu.VMEM((DEC_SEQ, LANES), BF16), pltpu.VMEM((DEC_SEQ, 2 * LANES), BF16),
                        pltpu.VMEM((PAST_LEN, LANES), BF16), pltpu.VMEM((PAST_LEN, 2 * LANES), BF16),
                        pltpu.VMEM((2, 2, D_GROUP * D_QBLOCK, D_BAND + PAST_LEN), BF16),
                        pltpu.VMEM((2, 2, D_GROUP * D_QBLOCK, 1), F32)],
        compiler_params=_params(("arbitrary", "arbitrary"), 40),
        name="gqa_lat",
    )(sink, *rope, qkv, qkv, qkv,
      cache_k.reshape(DEC_BATCH * PAST_LEN, kvw), cache_v.reshape(DEC_BATCH * PAST_LEN, kvw))
    return o_p, o_s, k_new, v_new


def kernel(x_prompt, x_sample, cache_a_k, cache_a_v, state_b_C, state_b_n, state_b_m, cache_c_k, cache_c_v,
           cache_d_k, cache_d_v, c, c_ctx, w_mod, b_mod, g_norm, w_ff1, w_ff2, a_w_in, a_w_out, a_lambda,
           a_subln, b_w_in, b_gate_bias, b_w_out, b_norm, c_w_in, c_w_out, c_rpb, d_w_in, d_w_out, d_sink):
    cond = jnp.concatenate([c_ctx[None, :], c, jnp.zeros((N_COND - 1 - DEC_BATCH, D_MODEL), F32)], axis=0)
    mod = _modulation(cond, w_mod, b_mod).reshape(DEPTH, N_COND, N_MOD, 1, D_MODEL)
    gains = g_norm.reshape(DEPTH, 4, 1, D_MODEL)
    rope = _rope_tables()
    new = {name: [] for name in ("a_k", "a_v", "b_C", "b_n", "b_m", "c_k", "c_v", "d_k", "d_v")}

    x, h = _adaln(x_prompt, x_sample, gains, mod, 0)
    for i in range(DEPTH):
        kind, j = i % N_MIXERS, i // N_MIXERS
        if kind == 0:
            lam_init = 0.8 - 0.6 * math.exp(-0.3 * i)
            qkv = _project(h, a_w_in[j])
            o_p, o_s, k_new, v_new = _mixer_a(qkv, cache_a_k[:, j], cache_a_v[:, j], a_lambda[j], a_subln[j],
                                              lam_init, rope)
            w_out = a_w_out[j]
            new["a_k"].append(k_new.reshape(BATCH, SEQ, A_HEADS, A_V_DIM))
            new["a_v"].append(v_new.reshape(BATCH, SEQ, A_HEADS, A_V_DIM))
        elif kind == 1:
            w_in = b_w_in[j]
            z = _project(h, w_in, n=B_MAIN)
            w_gate = jnp.pad(w_in[:, B_MAIN:], ((0, 0), (0, LANES - 4 * B_HEADS)))
            gates = _project(h, w_gate)
            o_p, o_s, c_new, n_new, m_new = _mixer_b(z, gates, state_b_C[:, j], state_b_n[:, j], state_b_m[:, j],
                                                     b_gate_bias[j], b_norm[j])
            w_out = b_w_out[j]
            new["b_C"].append(c_new)
            new["b_n"].append(n_new)
            new["b_m"].append(m_new)
        elif kind == 2:
            qkv = _project(h, c_w_in[j])
            o_p, o_s, k_new, v_new = _mixer_c(qkv, cache_c_k[:, j], cache_c_v[:, j], c_rpb[j])
            w_out = c_w_out[j]
            new["c_k"].append(k_new.reshape(BATCH, SEQ, C_HEADS, C_HEAD_DIM))
            new["c_v"].append(v_new.reshape(BATCH, SEQ, C_HEADS, C_HEAD_DIM))
        else:
            qkv = _project(h, d_w_in[j])
            o_p, o_s, k_new, v_new = _mixer_d(qkv, cache_d_k[:, j], cache_d_v[:, j], d_sink[j], rope)
            w_out = d_w_out[j]
            new["d_k"].append(k_new.reshape(BATCH, SEQ, D_KV_HEADS, D_HEAD_DIM))
            new["d_v"].append(v_new.reshape(BATCH, SEQ, D_KV_HEADS, D_HEAD_DIM))
        x, h = _out_project(o_p, o_s, w_out, x, gains, mod, i)
        y = _mlp(h, w_ff1, w_ff2, i)
        x, h = _residual(x, y, gains, mod, i, 5, 3, (i + 1, 0, 0, 1) if i + 1 < DEPTH else None)

    x_p, x_s = x
    stack = lambda name: jnp.stack(new[name], axis=1)
    return (x_p.reshape(BATCH, SEQ, D_MODEL), x_s.reshape(DEC_BATCH, DEC_SEQ, D_MODEL),
            stack("a_k"), stack("a_v"), stack("b_C"), stack("b_n"), stack("b_m"),
            stack("c_k"), stack("c_v"), stack("d_k"), stack("d_v"))
```

```python
import functools
import math

import numpy as np
import jax
import jax.numpy as jnp
from jax import lax
from jax.experimental import pallas as pl
from jax.experimental.pallas import tpu as pltpu

D_MODEL = 2048
BATCH = 16
SEQ = 256
DEPTH = 4
DEC_BATCH = 4
DEC_SEQ = 1024
PAST_LEN = 512
GRID_W = 64
N_MIXERS = 4
D_FF = 4 * D_MODEL
N_MOD = 6
RMS_EPS = 1e-6
ROPE_THETA = 10000.0

A_HEADS = 16
A_HALF_DIM = 64
A_V_DIM = 128
B_HEADS = 8
B_DK = 128
B_DV = 256
B_QK = B_HEADS * B_DK
B_VD = B_HEADS * B_DV
B_MAIN = 2 * B_QK + 2 * B_VD
C_HEADS = 16
C_HEAD_DIM = 128
NA_ROWS = 8
NA_COLS = 16
GRID_H = DEC_SEQ // GRID_W
D_HEADS = 32
D_KV_HEADS = 8
D_GROUP = 4
D_HEAD_DIM = 64
WINDOW = 128

N_PROMPT = BATCH * SEQ
N_SAMPLE = DEC_BATCH * DEC_SEQ
N_TOK = N_PROMPT + N_SAMPLE
N_COND = 8
SAMPLE_ROW0 = N_PROMPT // DEC_SEQ

LANES = 128
HALF = LANES // 2
NEG = -1e30
MIB = 1024 * 1024
F32 = jnp.float32
BF16 = jnp.bfloat16


def _params(sem, vmem_mib):
    return pltpu.CompilerParams(dimension_semantics=sem, vmem_limit_bytes=vmem_mib * MIB)


def _dot(a, b):
    return jnp.dot(a, b, preferred_element_type=F32)


def _dot_nt(a, b):
    return lax.dot_general(a, b, (((1,), (1,)), ((), ())), preferred_element_type=F32)


def _dot_tn(a, b):
    return lax.dot_general(a, b, (((0,), (0,)), ((), ())), preferred_element_type=F32)


def _rms(x, g):
    return x * lax.rsqrt(jnp.mean(x * x, axis=-1, keepdims=True) + RMS_EPS) * g


def _cond_index(row0):
    return jnp.where(row0 < N_PROMPT, 0, 1 + (row0 - N_PROMPT) // DEC_SEQ)


def _mod_kernel(c_ref, w_ref, b_ref, o_ref):
    c = c_ref[...]
    s = (c / (1.0 + jnp.exp(-c))).astype(BF16)
    o_ref[...] = _dot(s, w_ref[...].astype(BF16)) + b_ref[...]


def _modulation(cond, w_mod, b_mod):
    tn = 1024
    n = N_MOD * D_MODEL
    return pl.pallas_call(
        _mod_kernel,
        out_shape=jax.ShapeDtypeStruct((DEPTH, N_COND, n), F32),
        grid=(DEPTH, n // tn),
        in_specs=[
            pl.BlockSpec((N_COND, D_MODEL), lambda l, j: (0, 0)),
            pl.BlockSpec((None, D_MODEL, tn), lambda l, j: (l, 0, j)),
            pl.BlockSpec((None, 1, tn), lambda l, j: (l, 0, j)),
        ],
        out_specs=pl.BlockSpec((None, N_COND, tn), lambda l, j: (l, 0, j)),
        compiler_params=_params(("arbitrary", "arbitrary"), 40),
        name="modulation",
    )(cond, w_mod, b_mod.reshape(DEPTH, 1, n))


ROW_TILE = 512


def _mod_spec(layer, which, tile0=0):
    return pl.BlockSpec((None, None, None, 1, D_MODEL),
                        lambda i: (layer, _cond_index((i + tile0) * ROW_TILE), which, 0, 0))


def _gain_spec(layer, which):
    return pl.BlockSpec((None, None, 1, D_MODEL), lambda i: (layer, which, 0, 0))


def _adaln_kernel(xp_ref, xs_ref, g_ref, shift_ref, scale_ref, x_ref, h_ref):
    def emit(x):
        x_ref[...] = x
        h_ref[...] = (_rms(x, g_ref[...]) * (1.0 + scale_ref[...]) + shift_ref[...]).astype(BF16)

    @pl.when(pl.program_id(0) < N_PROMPT // ROW_TILE)
    def _():
        emit(xp_ref[...])

    @pl.when(pl.program_id(0) >= N_PROMPT // ROW_TILE)
    def _():
        emit(xs_ref[...])


def _adaln(x_prompt, x_sample, gains, mod, layer):
    p_tiles = N_PROMPT // ROW_TILE
    row = pl.BlockSpec((ROW_TILE, D_MODEL), lambda i: (i, 0))
    return pl.pallas_call(
        _adaln_kernel,
        out_shape=(jax.ShapeDtypeStruct((N_TOK, D_MODEL), F32), jax.ShapeDtypeStruct((N_TOK, D_MODEL), BF16)),
        grid=(N_TOK // ROW_TILE,),
        in_specs=[pl.BlockSpec((ROW_TILE, D_MODEL), lambda i: (jnp.minimum(i, p_tiles - 1), 0)),
                  pl.BlockSpec((ROW_TILE, D_MODEL), lambda i: (jnp.maximum(i - p_tiles, 0), 0)),
                  _gain_spec(layer, 0), _mod_spec(layer, 0), _mod_spec(layer, 1)],
        out_specs=(row, row),
        compiler_params=_params(("arbitrary",), 40),
        name="adaln",
    )(x_prompt.reshape(N_PROMPT, D_MODEL), x_sample.reshape(N_SAMPLE, D_MODEL), gains, mod, mod)


def _residual_adaln_kernel(x_ref, y_ref, gate_ref, gpost_ref, gpre_ref, shift_ref, scale_ref, xo_ref, h_ref):
    x = x_ref[...] + gate_ref[...] * _rms(y_ref[...], gpost_ref[...])
    xo_ref[...] = x
    h = _rms(x, gpre_ref[...]) * (1.0 + scale_ref[...]) + shift_ref[...]
    h_ref[...] = h.astype(BF16)


def _residual_kernel(x_ref, y_ref, gate_ref, gpost_ref, xo_ref):
    xo_ref[...] = x_ref[...] + gate_ref[...] * _rms(y_ref[...], gpost_ref[...])


def _residual(x, y, gains, mod, layer, gate_idx, post_idx, nxt):
    row = pl.BlockSpec((ROW_TILE, D_MODEL), lambda i: (i, 0))
    if nxt is None:
        outs = []
        for tile0, n_rows in ((0, N_PROMPT), (N_PROMPT // ROW_TILE, N_SAMPLE)):
            src = pl.BlockSpec((ROW_TILE, D_MODEL), lambda i, t=tile0: (i + t, 0))
            outs.append(pl.pallas_call(
                _residual_kernel, out_shape=jax.ShapeDtypeStruct((n_rows, D_MODEL), F32),
                grid=(n_rows // ROW_TILE,),
                in_specs=[src, src, _mod_spec(layer, gate_idx, tile0), _gain_spec(layer, post_idx)],
                out_specs=row,
                compiler_params=_params(("arbitrary",), 32), name="residual",
            )(x, y, mod, gains))
        return outs, None
    nl, ng, nshift, nscale = nxt
    return pl.pallas_call(
        _residual_adaln_kernel,
        out_shape=(jax.ShapeDtypeStruct((N_TOK, D_MODEL), F32), jax.ShapeDtypeStruct((N_TOK, D_MODEL), BF16)),
        grid=(N_TOK // ROW_TILE,),
        in_specs=[row, row, _mod_spec(layer, gate_idx), _gain_spec(layer, post_idx),
                  _gain_spec(nl, ng), _mod_spec(nl, nshift), _mod_spec(nl, nscale)],
        out_specs=(row, row),
        compiler_params=_params(("arbitrary",), 40), name="residual_adaln",
    )(x, y, mod, gains, gains, mod, mod)


PROJ_TM = 1024


def _proj_kernel(a_ref, w_ref, o_ref, wbf_ref):
    @pl.when(pl.program_id(1) == 0)
    def _():
        wbf_ref[...] = w_ref[...].astype(BF16)

    o_ref[...] = _dot(a_ref[...], wbf_ref[...]).astype(o_ref.dtype)


def _project(a, w, n=None, out_dtype=F32):
    k = w.shape[0]
    n = w.shape[1] if n is None else n
    tm = PROJ_TM
    tn = min(n, 1024)
    return pl.pallas_call(
        _proj_kernel,
        out_shape=jax.ShapeDtypeStruct((N_TOK, n), out_dtype),
        grid=(n // tn, N_TOK // tm),
        in_specs=[pl.BlockSpec((tm, k), lambda j, i: (i, 0)),
                  pl.BlockSpec((k, tn), lambda j, i: (0, j))],
        out_specs=pl.BlockSpec((tm, tn), lambda j, i: (i, j)),
        scratch_shapes=[pltpu.VMEM((k, tn), BF16)],
        compiler_params=_params(("arbitrary", "arbitrary"), 48),
        name="project",
    )(a, w)


OUT_TM = 512


def _out_proj_kernel(ap_ref, as_ref, w_ref, x_ref, gate_ref, gpost_ref, gpre_ref, shift_ref, scale_ref,
                     xo_ref, h_ref):
    def finish(a):
        x = x_ref[...] + gate_ref[...] * _rms(_dot(a, w_ref[...]), gpost_ref[...])
        xo_ref[...] = x
        h_ref[...] = (_rms(x, gpre_ref[...]) * (1.0 + scale_ref[...]) + shift_ref[...]).astype(BF16)

    @pl.when(pl.program_id(0) < N_PROMPT // OUT_TM)
    def _():
        finish(ap_ref[...])

    @pl.when(pl.program_id(0) >= N_PROMPT // OUT_TM)
    def _():
        finish(as_ref[...])


def _out_project(o_p, o_s, w_out, x, gains, mod, layer):
    tm = OUT_TM
    p_tiles = N_PROMPT // tm
    row = pl.BlockSpec((tm, D_MODEL), lambda i: (i, 0))

    def mod_spec(which):
        return pl.BlockSpec((None, None, None, 1, D_MODEL), lambda i: (layer, _cond_index(i * tm), which, 0, 0))

    return pl.pallas_call(
        _out_proj_kernel,
        out_shape=(jax.ShapeDtypeStruct((N_TOK, D_MODEL), F32), jax.ShapeDtypeStruct((N_TOK, D_MODEL), BF16)),
        grid=(N_TOK // tm,),
        in_specs=[pl.BlockSpec((tm, D_MODEL), lambda i: (jnp.minimum(i, p_tiles - 1), 0)),
                  pl.BlockSpec((tm, D_MODEL), lambda i: (jnp.maximum(i - p_tiles, 0), 0)),
                  pl.BlockSpec((D_MODEL, D_MODEL), lambda i: (0, 0)),
                  row, mod_spec(2), _gain_spec(layer, 1), _gain_spec(layer, 2), mod_spec(3), mod_spec(4)],
        out_specs=(row, row),
        compiler_params=_params(("arbitrary",), 56),
        name="out_project",
    )(o_p, o_s, w_out.astype(BF16), x, mod, gains, gains, mod, mod)


def _mlp_kernel(h_ref, w1_ref, w2_ref, o_ref):
    @pl.when(pl.program_id(1) == 0)
    def _():
        o_ref[...] = jnp.zeros_like(o_ref)

    u = jnp.maximum(_dot(h_ref[...], w1_ref[...].astype(BF16)), 0.0)
    o_ref[...] += _dot((u * u).astype(BF16), w2_ref[...].astype(BF16))


def _mlp(h, w1, w2, layer):
    tm, tf = 1024, 512
    return pl.pallas_call(
        _mlp_kernel,
        out_shape=jax.ShapeDtypeStruct((N_TOK, D_MODEL), F32),
        grid=(N_TOK // tm, D_FF // tf),
        in_specs=[pl.BlockSpec((tm, D_MODEL), lambda i, f: (i, 0)),
                  pl.BlockSpec((None, D_MODEL, tf), lambda i, f: (layer, 0, f)),
                  pl.BlockSpec((None, tf, D_MODEL), lambda i, f: (layer, f, 0))],
        out_specs=pl.BlockSpec((tm, D_MODEL), lambda i, f: (i, 0)),
        compiler_params=_params(("arbitrary", "arbitrary"), 56),
        name="mlp",
    )(h, w1, w2)


def _rope_tables():
    t = jnp.arange(DEC_SEQ)
    lane = np.arange(LANES)
    f = lane % 32
    first = f < 16
    inv = ROPE_THETA ** (-jnp.arange(16, dtype=F32) / 16)
    pos = jnp.where((lane % 64 < 32)[None, :], (t // GRID_W)[:, None], (t % GRID_W)[:, None]).astype(F32)
    ang = pos * inv[f % 16][None, :]
    cos, sin = jnp.cos(ang), jnp.sin(ang)
    sin_next = jnp.where(first[None, :], -sin, 0.0)
    sin_prev = jnp.where(first[None, :], 0.0, sin)
    return cos, sin_next, sin_prev


def _rope(x, cos, sin_next, sin_prev):
    return (x * cos + pltpu.roll(x, LANES - 16, axis=1) * sin_next
            + pltpu.roll(x, 16, axis=1) * sin_prev)


def _lane_lo(shape):
    return lax.broadcasted_iota(jnp.int32, shape, 1) < HALF


def _with_ones(v):
    return jnp.concatenate([v.astype(BF16), jnp.ones(v.shape, BF16)], axis=1)


def _attend(scores, vals_ones, sink=None):
    m = None
    for s in scores:
        mi = jnp.max(s, axis=-1, keepdims=True)
        m = mi if m is None else jnp.maximum(m, mi)
    if sink is not None:
        m = jnp.maximum(m, sink)
    acc = None
    for s, v in zip(scores, vals_ones):
        part = _dot(jnp.exp(s - m).astype(BF16), v)
        acc = part if acc is None else acc + part
    den = acc[:, LANES:]
    if sink is not None:
        den = den + jnp.exp(sink - m)
    return acc[:, :LANES] / den


CTX_HEADS = 8


A_SCALE = A_HALF_DIM ** -0.5


def _diff_lambda(lp, lam_init):
    a = jnp.sum(lp[0] * lp[1], axis=-1, keepdims=True)
    b = jnp.sum(lp[2] * lp[3], axis=-1, keepdims=True)
    return jnp.exp(a) - jnp.exp(b) + lam_init


def _diff_finish(o, subln_ref, lam_init):
    return (_rms(o, subln_ref[...]) * (1.0 - lam_init)).astype(BF16)


def _a_ctx_kernel(lam_ref, subln_ref, q_ref, k_ref, v_ref, o_ref, ko_ref, vo_ref, *, lam_init):
    lo = _lane_lo((SEQ, LANES))
    for hh in range(CTX_HEADS):
        cols = slice(hh * LANES, (hh + 1) * LANES)
        q = q_ref[:, cols] * A_SCALE
        k = k_ref[:, cols]
        v = v_ref[:, cols]
        ko_ref[:, cols] = k
        vo_ref[:, cols] = v
        kb, v1 = k.astype(BF16), _with_ones(v)
        o1 = _attend([_dot_nt(jnp.where(lo, q, 0.0).astype(BF16), kb)], [v1])
        o2 = _attend([_dot_nt(jnp.where(lo, 0.0, q).astype(BF16), kb)], [v1])
        lam = _diff_lambda(lam_ref[:, hh], lam_init)
        o_ref[:, cols] = _diff_finish(o1 - lam * o2, subln_ref, lam_init)


A_LAT_TQ = 256


def _a_lat_kernel(lam_ref, subln_ref, cos_ref, nxt_ref, prv_ref, q_ref, k_ref, v_ref, pk_ref, pv_ref, o_ref,
                  kb_ref, v1_ref, e_ref, *, lam_init):
    kb_ref[:PAST_LEN, :] = pk_ref[...].astype(BF16)
    kb_ref[PAST_LEN:, :] = _rope(k_ref[...], cos_ref[...], nxt_ref[...], prv_ref[...]).astype(BF16)
    v1_ref[:PAST_LEN, :] = _with_ones(pv_ref[...])
    v1_ref[PAST_LEN:, :] = _with_ones(v_ref[...])
    lam = _diff_lambda(lam_ref[...], lam_init)
    lo = _lane_lo((A_LAT_TQ, LANES))
    n_blocks = DEC_SEQ // A_LAT_TQ

    def weights(i):
        rows = slice(i * A_LAT_TQ, (i + 1) * A_LAT_TQ)
        q = _rope(q_ref[rows, :], cos_ref[rows, :], nxt_ref[rows, :], prv_ref[rows, :]) * A_SCALE
        for half, keep in enumerate((lo, ~lo)):
            s = _dot_nt(jnp.where(keep, q, 0.0).astype(BF16), kb_ref[...])
            e_ref[i % 2, half] = jnp.exp(s - jnp.max(s, axis=-1, keepdims=True)).astype(BF16)

    def values(i):
        rows = slice(i * A_LAT_TQ, (i + 1) * A_LAT_TQ)
        outs = []
        for half in (0, 1):
            acc = _dot(e_ref[i % 2, half], v1_ref[...])
            outs.append(acc[:, :LANES] / acc[:, LANES:])
        o_ref[rows, :] = _diff_finish(outs[0] - lam * outs[1], subln_ref, lam_init)

    weights(0)
    for i in range(1, n_blocks):
        values(i - 1)
        weights(i)
    values(n_blocks - 1)


def _mixer_a(qkv, cache_k, cache_v, lam_p, subln, lam_init, rope):
    nh = A_HEADS
    lam4 = lam_p.reshape(4, nh, 1, A_HALF_DIM)
    sub2 = subln.reshape(1, A_V_DIM)
    cw = CTX_HEADS * LANES
    nt = nh // CTX_HEADS
    ctx_out = pl.BlockSpec((SEQ, cw), lambda b, h: (b, h))
    kv_shape = jax.ShapeDtypeStruct((N_PROMPT, nh * A_V_DIM), F32)
    o_p, k_new, v_new = pl.pallas_call(
        functools.partial(_a_ctx_kernel, lam_init=lam_init),
        out_shape=(jax.ShapeDtypeStruct((N_PROMPT, nh * A_V_DIM), BF16), kv_shape, kv_shape),
        grid=(BATCH, nt),
        in_specs=[pl.BlockSpec((4, CTX_HEADS, 1, A_HALF_DIM), lambda b, h: (0, h, 0, 0)),
                  pl.BlockSpec((1, A_V_DIM), lambda b, h: (0, 0)),
                  pl.BlockSpec((SEQ, cw), lambda b, h: (b, h)),
                  pl.BlockSpec((SEQ, cw), lambda b, h: (b, nt + h)),
                  pl.BlockSpec((SEQ, cw), lambda b, h: (b, 2 * nt + h))],
        out_specs=(ctx_out, ctx_out, ctx_out),
        compiler_params=_params(("arbitrary", "arbitrary"), 32),
        name="diff_attn_ctx",
    )(lam4, sub2, qkv, qkv, qkv)

    table = pl.BlockSpec((DEC_SEQ, LANES), lambda b, h: (0, 0))
    past = pl.BlockSpec((PAST_LEN, LANES), lambda b, h: (b, h))
    o_s = pl.pallas_call(
        functools.partial(_a_lat_kernel, lam_init=lam_init),
        out_shape=jax.ShapeDtypeStruct((N_SAMPLE, nh * A_V_DIM), BF16),
        grid=(DEC_BATCH, nh),
        in_specs=[pl.BlockSpec((4, None, 1, A_HALF_DIM), lambda b, h: (0, h, 0, 0)),
                  pl.BlockSpec((1, A_V_DIM), lambda b, h: (0, 0)),
                  table, table, table,
                  pl.BlockSpec((DEC_SEQ, LANES), lambda b, h: (SAMPLE_ROW0 + b, h)),
                  pl.BlockSpec((DEC_SEQ, LANES), lambda b, h: (SAMPLE_ROW0 + b, nh + h)),
                  pl.BlockSpec((DEC_SEQ, LANES), lambda b, h: (SAMPLE_ROW0 + b, 2 * nh + h)),
                  past, past],
        out_specs=pl.BlockSpec((DEC_SEQ, LANES), lambda b, h: (b, h)),
        scratch_shapes=[pltpu.VMEM((PAST_LEN + DEC_SEQ, LANES), BF16),
                        pltpu.VMEM((PAST_LEN + DEC_SEQ, 2 * LANES), BF16),
                        pltpu.VMEM((2, 2, A_LAT_TQ, PAST_LEN + DEC_SEQ), BF16)],
        compiler_params=_params(("arbitrary", "arbitrary"), 40),
        name="diff_attn_lat",
    )(lam4, sub2, *rope, qkv, qkv, qkv,
      cache_k.reshape(DEC_BATCH * PAST_LEN, nh * A_V_DIM), cache_v.reshape(DEC_BATCH * PAST_LEN, nh * A_V_DIM))
    return o_p, o_s, k_new, v_new


B_CHUNK = 256


def _log_sigmoid(x):
    return jnp.minimum(x, 0.0) - jnp.log(1.0 + jnp.exp(-jnp.abs(x)))


def _mlstm_kernel(*refs, heads, **static):
    def head_view(ref, hh, kind):
        if kind == "cols":
            width = ref.shape[1] // heads
            return ref.at[:, hh * width:(hh + 1) * width]
        if kind == "lead":
            return ref.at[hh]
        return ref.at[:, hh]

    kinds = (["cols"] * 4 + ["lead"] * 4 + ["cols"] + (["state"] * 3 if static["has_init"] else [])
             + ["cols"] + (["state"] * 3 if static["emit_state"] else []) + ["lead", "lead"])
    assert len(kinds) == len(refs)
    for hh in range(heads):
        _mlstm_head(*[head_view(r, hh, kd) for r, kd in zip(refs, kinds)], **static)


def _mlstm_head(*refs, n_chunks, has_init, emit_state):
    it = iter(refs)
    q_ref, k_ref, v_ref, og_ref, gcol_ref, grow_ref, bcol_ref, brow_ref, nw_ref = (next(it) for _ in range(9))
    if has_init:
        c0_ref, n0_ref, m0_ref = next(it), next(it), next(it)
    h_ref = next(it)
    if emit_state:
        c_out, n_out, m_out = next(it), next(it), next(it)
    hst_ref, vt_ref = next(it), next(it)

    L = B_CHUNK
    si = lax.broadcasted_iota(jnp.int32, (L, L), 0)
    ti = lax.broadcasted_iota(jnp.int32, (L, L), 1)
    gcol = gcol_ref[...] + brow_ref[...]
    grow = grow_ref[...] + bcol_ref[...]
    qscale = B_DK ** -0.5
    chunks = [slice(c * L, (c + 1) * L) for c in range(n_chunks)]
    for rows in chunks:
        vt_ref[:, rows] = v_ref[rows, :].T

    for d in (0, 1):
        feeds = (si <= ti) if d == 0 else (si >= ti)
        before = (ti <= si) if d == 0 else (ti >= si)
        last = L - 1 if d == 0 else 0
        if has_init:
            C, n, m = c0_ref[d], n0_ref[d], m0_ref[d][:, :1]
        else:
            C, n, m = jnp.zeros((B_DV, B_DK), F32), jnp.zeros((1, B_DK), F32), jnp.zeros((1, 1), F32)
        for ci in range(n_chunks):
            rows = chunks[ci if d == 0 else n_chunks - 1 - ci]
            qb = (q_ref[rows, :] * qscale).astype(BF16)
            kb = k_ref[rows, :].astype(BF16)
            vt = vt_ref[:, rows]
            ig_row = grow[2 * d:2 * d + 1, rows]
            lf_row = _log_sigmoid(grow[2 * d + 1:2 * d + 2, rows])
            ig_col = gcol[rows, 2 * d:2 * d + 1]
            lf_col = _log_sigmoid(gcol[rows, 2 * d + 1:2 * d + 2])
            b_row = jnp.sum(jnp.where(feeds, lf_col, 0.0), axis=0, keepdims=True)
            b_col = jnp.sum(jnp.where(before, lf_row, 0.0), axis=1, keepdims=True)
            dm = jnp.where(feeds, b_row + (ig_col - b_col), NEG)
            inter = b_row + m
            m_t = jnp.maximum(inter, jnp.max(dm, axis=0, keepdims=True))
            w = jnp.exp(dm - m_t)
            a_in = jnp.exp(inter - m_t)
            sw = _dot_nt(kb, qb) * w
            num = _dot(vt.astype(BF16), sw.astype(BF16)) + a_in * _dot_nt(C.astype(BF16), qb)
            nq = _dot_nt(jnp.broadcast_to(n, (8, B_DK)).astype(BF16), qb)[:1]
            den = jnp.sum(sw, axis=0, keepdims=True) + a_in * nq
            h = num * (1.0 / jnp.maximum(jnp.abs(den), jnp.exp(-m_t)))
            if d == 0:
                hst_ref[:, rows] = h
            else:
                hst_ref[:, rows] += h
            if emit_state or ci + 1 < n_chunks:
                m_last = m_t[:, last:last + 1]
                al = a_in[:, last:last + 1]
                wl = jnp.exp(b_row[:, last:last + 1] - b_row + ig_row - m_last)
                C = al * C + _dot((vt * wl).astype(BF16), kb)
                n = al * n + _dot(jnp.broadcast_to(wl, (8, L)).astype(BF16), kb)[:1]
                m = m_last
        if emit_state:
            c_out[d] = C
            n_out[d] = n
            m_out[d] = jnp.broadcast_to(m, (1, LANES))

    for rows in chunks:
        hsum = hst_ref[:, rows].T
        gate = 1.0 / (1.0 + jnp.exp(-og_ref[rows, :]))
        h_ref[rows, :] = (_rms(hsum, nw_ref[...]) * gate).astype(BF16)


def _mlstm_call(z, gcol, grow, bias, norm_w, seq, batch, row0, init, heads):
    nh = B_HEADS
    has_init = init is not None
    emit_state = not has_init
    bcol = bias.reshape(2, 2, nh).transpose(2, 0, 1).reshape(nh, 4, 1)
    brow = bcol.reshape(nh, 1, 4)
    hp = heads
    qk_tiles = B_QK // (hp * B_DK)
    vd0 = 2 * B_QK // (hp * B_DV)
    nt = nh // hp
    in_specs = [
        pl.BlockSpec((seq, hp * B_DK), lambda b, h: (row0 + b, h)),
        pl.BlockSpec((seq, hp * B_DK), lambda b, h: (row0 + b, qk_tiles + h)),
        pl.BlockSpec((seq, hp * B_DV), lambda b, h: (row0 + b, vd0 + h)),
        pl.BlockSpec((seq, hp * B_DV), lambda b, h: (row0 + b, vd0 + nt + h)),
        pl.BlockSpec((hp, seq, 4), lambda b, h: (h, row0 + b, 0)),
        pl.BlockSpec((hp, 4, seq), lambda b, h: (h, 0, row0 + b)),
        pl.BlockSpec((hp, 4, 1), lambda b, h: (h, 0, 0)),
        pl.BlockSpec((hp, 1, 4), lambda b, h: (h, 0, 0)),
        pl.BlockSpec((1, hp * B_DV), lambda b, h: (0, h)),
    ]
    args = [z, z, z, z, gcol, grow, bcol, brow, norm_w.reshape(1, B_VD)]
    state_c = pl.BlockSpec((None, 2, hp, B_DV, B_DK), lambda b, h: (b, 0, h, 0, 0))
    state_v = pl.BlockSpec((None, 2, hp, 1, B_DK), lambda b, h: (b, 0, h, 0, 0))
    if has_init:
        c0, n0, m0 = init
        in_specs += [state_c, state_v, state_v]
        args += [c0, n0.reshape(batch, 2, nh, 1, B_DK),
                 jnp.broadcast_to(m0[..., None, None], (batch, 2, nh, 1, B_DK))]
    h_shape = jax.ShapeDtypeStruct((batch * seq, B_VD), BF16)
    h_spec = pl.BlockSpec((seq, hp * B_DV), lambda b, h: (b, h))
    if emit_state:
        out_shape = (h_shape,
                     jax.ShapeDtypeStruct((batch, 2, nh, B_DV, B_DK), F32),
                     jax.ShapeDtypeStruct((batch, 2, nh, 1, B_DK), F32),
                     jax.ShapeDtypeStruct((batch, 2, nh, 1, B_DK), F32))
        out_specs = (h_spec, state_c, state_v, state_v)
    else:
        out_shape, out_specs = h_shape, h_spec
    return pl.pallas_call(
        functools.partial(_mlstm_kernel, heads=hp, n_chunks=seq // B_CHUNK, has_init=has_init,
                          emit_state=emit_state),
        out_shape=out_shape,
        grid=(batch, nt),
        in_specs=in_specs, out_specs=out_specs,
        scratch_shapes=[pltpu.VMEM((hp, B_DV, seq), F32), pltpu.VMEM((hp, B_DV, seq), F32)],
        compiler_params=_params(("arbitrary", "arbitrary"), 48),
        name="mlstm_ctx" if emit_state else "mlstm_lat",
    )(*args)


def _mixer_b(z, gates, state_c, state_n, state_m, bias, norm_w):
    nh = B_HEADS
    g = gates[:, :4 * nh].reshape(N_TOK, 4, nh)
    gcol = g.transpose(2, 0, 1)
    grow = g.transpose(2, 1, 0)
    o_p, c_new, n_new, m_new = _mlstm_call(z, gcol, grow, bias, norm_w, SEQ, BATCH, 0, None, heads=4)
    o_s = _mlstm_call(z, gcol, grow, bias, norm_w, DEC_SEQ, DEC_BATCH, SAMPLE_ROW0,
                      (state_c, state_n, state_m), heads=1)
    return o_p, o_s, c_new, n_new.reshape(BATCH, 2, nh, B_DK), m_new[:, :, :, 0, 0]


C_SCALE = C_HEAD_DIM ** -0.5


def _c_ctx_kernel(q_ref, k_ref, v_ref, o_ref, ko_ref, vo_ref):
    for hh in range(CTX_HEADS):
        cols = slice(hh * LANES, (hh + 1) * LANES)
        k = k_ref[:, cols]
        v = v_ref[:, cols]
        ko_ref[:, cols] = k
        vo_ref[:, cols] = v
        s = _dot_nt(q_ref[:, cols].astype(BF16), k.astype(BF16)) * C_SCALE
        o_ref[:, cols] = _attend([s], [_with_ones(v)]).astype(BF16)


def _na_row_start(r):
    return min(max(r - NA_ROWS // 2, 0), GRID_H - NA_ROWS)


def _na_row_groups():
    groups = []
    for r in range(GRID_H):
        if groups and _na_row_start(groups[-1][0]) == _na_row_start(r):
            groups[-1].append(r)
        else:
            groups.append([r])
    return groups


def _c_lat_kernel(bias_ref, q_ref, k_ref, v_ref, pk_ref, pv_ref, o_ref, kb_ref, v1_ref, sc_ref, ec_ref, acc_ref):
    n_win = NA_ROWS * GRID_W
    kb_ref[...] = k_ref[...].astype(BF16)
    v1_ref[...] = _with_ones(v_ref[...])
    sc_ref[...] = _dot_nt(q_ref[...].astype(BF16), pk_ref[...].astype(BF16)) * C_SCALE
    for rows_g in _na_row_groups():
        r0 = _na_row_start(rows_g[0])
        rows = slice(rows_g[0] * GRID_W, (rows_g[-1] + 1) * GRID_W)
        win = slice(r0 * GRID_W, r0 * GRID_W + n_win)
        strips = []
        for r in rows_g:
            strip = NA_ROWS - 1 - (r - r0)
            even = strip - strip % 2
            strips.append(bias_ref[strip % 2, :, even * GRID_W:even * GRID_W + n_win])
        bias = strips[0] if len(strips) == 1 else jnp.concatenate(strips, axis=0)
        s_n = _dot_nt(q_ref[rows, :].astype(BF16), kb_ref[win, :]) * C_SCALE + bias
        s_c = sc_ref[rows, :]
        m = jnp.maximum(jnp.max(s_n, axis=-1, keepdims=True), jnp.max(s_c, axis=-1, keepdims=True))
        ec_ref[rows, :] = jnp.exp(s_c - m).astype(BF16)
        acc_ref[rows, :] = _dot(jnp.exp(s_n - m).astype(BF16), v1_ref[win, :])
    acc = acc_ref[...] + _dot(ec_ref[...], _with_ones(pv_ref[...]))
    o_ref[...] = (acc[:, :LANES] / acc[:, LANES:]).astype(BF16)


def _na_bias(rpb):
    n_drow, n_dcol = 2 * NA_ROWS - 1, 2 * NA_COLS - 1
    cq = np.arange(GRID_W)[:, None]
    kc = np.arange(GRID_W)[None, :]
    cstart = np.clip(cq - NA_COLS // 2, 0, GRID_W - NA_COLS)
    col_ok = (kc >= cstart) & (kc < cstart + NA_COLS)
    dcol = np.clip(kc - cq, -(NA_COLS - 1), NA_COLS - 1) + NA_COLS - 1
    onehot = (dcol[:, :, None] == np.arange(n_dcol)).astype(np.float32)
    strips = jnp.einsum('hab,qkb->hqak', rpb.astype(F32), onehot, precision=lax.Precision.HIGHEST)
    strips = jnp.where(col_ok[None, :, None, :], strips, NEG).reshape(C_HEADS, GRID_W, n_drow * GRID_W)
    strips = jnp.pad(strips, ((0, 0), (0, 0), (0, 2 * GRID_W)))
    return jnp.stack([strips[:, :, :(n_drow + 1) * GRID_W], strips[:, :, GRID_W:]], axis=1)


def _mixer_c(qkv, cache_k, cache_v, rpb):
    nh = C_HEADS
    cw = CTX_HEADS * LANES
    nt = nh // CTX_HEADS
    ctx_out = pl.BlockSpec((SEQ, cw), lambda b, h: (b, h))
    kv_shape = jax.ShapeDtypeStruct((N_PROMPT, D_MODEL), F32)
    o_p, k_new, v_new = pl.pallas_call(
        _c_ctx_kernel,
        out_shape=(jax.ShapeDtypeStruct((N_PROMPT, D_MODEL), BF16), kv_shape, kv_shape),
        grid=(BATCH, nt),
        in_specs=[pl.BlockSpec((SEQ, cw), lambda b, h: (b, h)),
                  pl.BlockSpec((SEQ, cw), lambda b, h: (b, nt + h)),
                  pl.BlockSpec((SEQ, cw), lambda b, h: (b, 2 * nt + h))],
        out_specs=(ctx_out, ctx_out, ctx_out),
        compiler_params=_params(("arbitrary", "arbitrary"), 32),
        name="na_ctx",
    )(qkv, qkv, qkv)

    past = pl.BlockSpec((PAST_LEN, LANES), lambda h, b: (b, h))
    o_s = pl.pallas_call(
        _c_lat_kernel,
        out_shape=jax.ShapeDtypeStruct((N_SAMPLE, D_MODEL), BF16),
        grid=(nh, DEC_BATCH),
        in_specs=[pl.BlockSpec((None, 2, GRID_W, 2 * NA_ROWS * GRID_W), lambda h, b: (h, 0, 0, 0)),
                  pl.BlockSpec((DEC_SEQ, LANES), lambda h, b: (SAMPLE_ROW0 + b, h)),
                  pl.BlockSpec((DEC_SEQ, LANES), lambda h, b: (SAMPLE_ROW0 + b, nh + h)),
                  pl.BlockSpec((DEC_SEQ, LANES), lambda h, b: (SAMPLE_ROW0 + b, 2 * nh + h)),
                  past, past],
        out_specs=pl.BlockSpec((DEC_SEQ, LANES), lambda h, b: (b, h)),
        scratch_shapes=[pltpu.VMEM((DEC_SEQ, LANES), BF16), pltpu.VMEM((DEC_SEQ, 2 * LANES), BF16),
                        pltpu.VMEM((DEC_SEQ, PAST_LEN), F32), pltpu.VMEM((DEC_SEQ, PAST_LEN), BF16),
                        pltpu.VMEM((DEC_SEQ, 2 * LANES), F32)],
        compiler_params=_params(("arbitrary", "arbitrary"), 40),
        name="na_lat",
    )(_na_bias(rpb), qkv, qkv, qkv,
      cache_k.reshape(DEC_BATCH * PAST_LEN, D_MODEL), cache_v.reshape(DEC_BATCH * PAST_LEN, D_MODEL))
    return o_p, o_s, k_new, v_new


D_SCALE = D_HEAD_DIM ** -0.5
D_QBLOCK = 128
D_BAND = 3 * D_QBLOCK


def _d_ctx_kernel(sink_ref, q_ref, k_ref, v_ref, o_ref, ko_ref, vo_ref):
    pair = pl.program_id(1)
    k = k_ref[...]
    v = v_ref[...]
    ko_ref[...] = k
    vo_ref[...] = v
    kb, v1 = k.astype(BF16), _with_ones(v)
    lo = _lane_lo((SEQ, LANES))
    for tile in range(4):
        kv = tile // 2
        keep = lo if kv == 0 else ~lo
        qt = q_ref[:, tile * LANES:(tile + 1) * LANES] * D_SCALE
        halves = []
        for e in (0, 1):
            qe = qt if e == kv else pltpu.roll(qt, HALF, axis=1)
            s = _dot_nt(jnp.where(keep, qe, 0.0).astype(BF16), kb)
            o = _attend([s], [v1], sink=sink_ref[pair * 2 * D_GROUP + tile * 2 + e])
            halves.append(o if e == kv else pltpu.roll(o, HALF, axis=1))
        o_ref[:, tile * LANES:(tile + 1) * LANES] = jnp.where(lo, halves[0], halves[1]).astype(BF16)


def _d_lat_kernel(sink_ref, cos_ref, nxt_ref, prv_ref, q_ref, k_ref, v_ref, pk_ref, pv_ref, o_ref,
                  kb_ref, v1_ref, pkb_ref, pv1_ref, e_ref, es_ref):
    pair = pl.program_id(1)
    kb_ref[...] = _rope(k_ref[...], cos_ref[...], nxt_ref[...], prv_ref[...]).astype(BF16)
    v1_ref[...] = _with_ones(v_ref[...])
    pkb_ref[...] = pk_ref[...].astype(BF16)
    pv1_ref[...] = _with_ones(pv_ref[...])
    tq = D_QBLOCK
    lo = _lane_lo((tq, LANES))
    n_blocks = DEC_SEQ // tq

    def band_of(bi):
        start = min(max((bi - 1) * tq, 0), DEC_SEQ - D_BAND)
        return start, slice(start, start + D_BAND)

    def weights(bi):
        rows = slice(bi * tq, (bi + 1) * tq)
        start, band = band_of(bi)
        cos, nxt, prv = cos_ref[rows, :], nxt_ref[rows, :], prv_ref[rows, :]
        qpos = bi * tq + lax.broadcasted_iota(jnp.int32, (tq, D_BAND), 0)
        kpos = start + lax.broadcasted_iota(jnp.int32, (tq, D_BAND), 1)
        mask = jnp.where(jnp.abs(qpos - kpos) <= WINDOW, 0.0, NEG)
        mask4 = jnp.concatenate([mask] * D_GROUP, axis=0)
        tiles = [_rope(q_ref[rows, t * LANES:(t + 1) * LANES], cos, nxt, prv) * D_SCALE for t in range(4)]
        for kv in (0, 1):
            keep = lo if kv == 0 else ~lo
            stack, sinks = [], []
            for t in (2 * kv, 2 * kv + 1):
                for e in (0, 1):
                    qe = tiles[t] if e == kv else pltpu.roll(tiles[t], HALF, axis=1)
                    stack.append(jnp.where(keep, qe, 0.0))
                    sinks.append(jnp.full((tq, 1), sink_ref[pair * 2 * D_GROUP + t * 2 + e], F32))
            qs = jnp.concatenate(stack, axis=0).astype(BF16)
            sink = jnp.concatenate(sinks, axis=0)
            s_band = _dot_nt(qs, kb_ref[band, :]) + mask4
            s_past = _dot_nt(qs, pkb_ref[...])
            m = jnp.maximum(jnp.maximum(jnp.max(s_band, axis=-1, keepdims=True),
                                        jnp.max(s_past, axis=-1, keepdims=True)), sink)
            e_ref[bi % 2, kv, :, :D_BAND] = jnp.exp(s_band - m).astype(BF16)
            e_ref[bi % 2, kv, :, D_BAND:] = jnp.exp(s_past - m).astype(BF16)
            es_ref[bi % 2, kv] = jnp.exp(sink - m)

    def values(bi):
        rows = slice(bi * tq, (bi + 1) * tq)
        _, band = band_of(bi)
        outs = {}
        for kv in (0, 1):
            acc = (_dot(e_ref[bi % 2, kv, :, :D_BAND], v1_ref[band, :])
                   + _dot(e_ref[bi % 2, kv, :, D_BAND:], pv1_ref[...]))
            o = acc[:, :LANES] / (acc[:, LANES:] + es_ref[bi % 2, kv])
            for i, t in enumerate((2 * kv, 2 * kv + 1)):
                for e in (0, 1):
                    piece = o[(2 * i + e) * tq:(2 * i + e + 1) * tq, :]
                    outs[(t, e)] = piece if e == kv else pltpu.roll(piece, HALF, axis=1)
        for t in range(4):
            o_ref[rows, t * LANES:(t + 1) * LANES] = jnp.where(lo, outs[(t, 0)], outs[(t, 1)]).astype(BF16)

    weights(0)
    for bi in range(1, n_blocks):
        values(bi - 1)
        weights(bi)
    values(n_blocks - 1)


def _mixer_d(qkv, cache_k, cache_v, sink, rope):
    n_pairs = D_KV_HEADS // 2
    qw = 2 * D_GROUP * D_HEAD_DIM
    k0 = D_HEADS * D_HEAD_DIM // LANES
    v0 = k0 + n_pairs
    kvw = D_KV_HEADS * D_HEAD_DIM
    smem = pl.BlockSpec(memory_space=pltpu.SMEM)
    sink = sink.astype(F32)
    kv_out = pl.BlockSpec((SEQ, LANES), lambda b, p: (b, p))
    kv_shape = jax.ShapeDtypeStruct((N_PROMPT, kvw), F32)
    o_p, k_new, v_new = pl.pallas_call(
        _d_ctx_kernel,
        out_shape=(jax.ShapeDtypeStruct((N_PROMPT, D_MODEL), BF16), kv_shape, kv_shape),
        grid=(BATCH, n_pairs),
        in_specs=[smem,
                  pl.BlockSpec((SEQ, qw), lambda b, p: (b, p)),
                  pl.BlockSpec((SEQ, LANES), lambda b, p: (b, k0 + p)),
                  pl.BlockSpec((SEQ, LANES), lambda b, p: (b, v0 + p))],
        out_specs=(pl.BlockSpec((SEQ, qw), lambda b, p: (b, p)), kv_out, kv_out),
        compiler_params=_params(("arbitrary", "arbitrary"), 40),
        name="gqa_ctx",
    )(sink, qkv, qkv, qkv)

    table = pl.BlockSpec((DEC_SEQ, LANES), lambda b, p: (0, 0))
    past = pl.BlockSpec((PAST_LEN, LANES), lambda b, p: (b, p))
    o_s = pl.pallas_call(
        _d_lat_kernel,
        out_shape=jax.ShapeDtypeStruct((N_SAMPLE, D_MODEL), BF16),
        grid=(DEC_BATCH, n_pairs),
        in_specs=[smem, table, table, table,
                  pl.BlockSpec((DEC_SEQ, qw), lambda b, p: (SAMPLE_ROW0 + b, p)),
                  pl.BlockSpec((DEC_SEQ, LANES), lambda b, p: (SAMPLE_ROW0 + b, k0 + p)),
                  pl.BlockSpec((DEC_SEQ, LANES), lambda b, p: (SAMPLE_ROW0 + b, v0 + p)),
                  past, past],
        out_specs=pl.BlockSpec((DEC_SEQ, qw), lambda b, p: (b, p)),
        scratch_shapes=[pltpu.VMEM((DEC_SEQ, LANES), BF16), pltpu.VMEM((DEC_SEQ, 2 * LANES), BF16),
                        pltpu.VMEM((PAST_LEN, LANES), BF16), pltpu.VMEM((PAST_LEN, 2 * LANES), BF16),
                        pltpu.VMEM((2, 2, D_GROUP * D_QBLOCK, D_BAND + PAST_LEN), BF16),
                        pltpu.VMEM((2, 2, D_GROUP * D_QBLOCK, 1), F32)],
        compiler_params=_params(("arbitrary", "arbitrary"), 40),
        name="gqa_lat",
    )(sink, *rope, qkv, qkv, qkv,
      cache_k.reshape(DEC_BATCH * PAST_LEN, kvw), cache_v.reshape(DEC_BATCH * PAST_LEN, kvw))
    return o_p, o_s, k_new, v_new


def kernel(x_prompt, x_sample, cache_a_k, cache_a_v, state_b_C, state_b_n, state_b_m, cache_c_k, cache_c_v,
           cache_d_k, cache_d_v, c, c_ctx, w_mod, b_mod, g_norm, w_ff1, w_ff2, a_w_in, a_w_out, a_lambda,
           a_subln, b_w_in, b_gate_bias, b_w_out, b_norm, c_w_in, c_w_out, c_rpb, d_w_in, d_w_out, d_sink):
    cond = jnp.concatenate([c_ctx[None, :], c, jnp.zeros((N_COND - 1 - DEC_BATCH, D_MODEL), F32)], axis=0)
    mod = _modulation(cond, w_mod, b_mod).reshape(DEPTH, N_COND, N_MOD, 1, D_MODEL)
    gains = g_norm.reshape(DEPTH, 4, 1, D_MODEL)
    rope = _rope_tables()
    new = {name: [] for name in ("a_k", "a_v", "b_C", "b_n", "b_m", "c_k", "c_v", "d_k", "d_v")}

    x, h = _adaln(x_prompt, x_sample, gains, mod, 0)
    for i in range(DEPTH):
        kind, j = i % N_MIXERS, i // N_MIXERS
        if kind == 0:
            lam_init = 0.8 - 0.6 * math.exp(-0.3 * i)
            qkv = _project(h, a_w_in[j])
            o_p, o_s, k_new, v_new = _mixer_a(qkv, cache_a_k[:, j], cache_a_v[:, j], a_lambda[j], a_subln[j],
                                              lam_init, rope)
            w_out = a_w_out[j]
            new["a_k"].append(k_new.reshape(BATCH, SEQ, A_HEADS, A_V_DIM))
            new["a_v"].append(v_new.reshape(BATCH, SEQ, A_HEADS, A_V_DIM))
        elif kind == 1:
            w_in = b_w_in[j]
            z = _project(h, w_in, n=B_MAIN)
            w_gate = jnp.pad(w_in[:, B_MAIN:], ((0, 0), (0, LANES - 4 * B_HEADS)))
            gates = _project(h, w_gate)
            o_p, o_s, c_new, n_new, m_new = _mixer_b(z, gates, state_b_C[:, j], state_b_n[:, j], state_b_m[:, j],
                                                     b_gate_bias[j], b_norm[j])
            w_out = b_w_out[j]
            new["b_C"].append(c_new)
            new["b_n"].append(n_new)
            new["b_m"].append(m_new)
        elif kind == 2:
            qkv = _project(h, c_w_in[j])
            o_p, o_s, k_new, v_new = _mixer_c(qkv, cache_c_k[:, j], cache_c_v[:, j], c_rpb[j])
            w_out = c_w_out[j]
            new["c_k"].append(k_new.reshape(BATCH, SEQ, C_HEADS, C_HEAD_DIM))
            new["c_v"].append(v_new.reshape(BATCH, SEQ, C_HEADS, C_HEAD_DIM))
        else:
            qkv = _project(h, d_w_in[j])
            o_p, o_s, k_new, v_new = _mixer_d(qkv, cache_d_k[:, j], cache_d_v[:, j], d_sink[j], rope)
            w_out = d_w_out[j]
            new["d_k"].append(k_new.reshape(BATCH, SEQ, D_KV_HEADS, D_HEAD_DIM))
            new["d_v"].append(v_new.reshape(BATCH, SEQ, D_KV_HEADS, D_HEAD_DIM))
        x, h = _out_project(o_p, o_s, w_out, x, gains, mod, i)
        y = _mlp(h, w_ff1, w_ff2, i)
        x, h = _residual(x, y, gains, mod, i, 5, 3, (i + 1, 0, 0, 1) if i + 1 < DEPTH else None)

    x_p, x_s = x
    stack = lambda name: jnp.stack(new[name], axis=1)
    return (x_p.reshape(BATCH, SEQ, D_MODEL), x_s.reshape(DEC_BATCH, DEC_SEQ, D_MODEL),
            stack("a_k"), stack("a_v"), stack("b_C"), stack("b_n"), stack("b_m"),
            stack("c_k"), stack("c_v"), stack("d_k"), stack("d_v"))
```

```python
import functools
import math

import numpy as np
import jax
import jax.numpy as jnp
from jax import lax
from jax.experimental import pallas as pl
from jax.experimental.pallas import tpu as pltpu

D_MODEL = 2048
BATCH = 16
SEQ = 256
DEPTH = 4
DEC_BATCH = 4
DEC_SEQ = 1024
PAST_LEN = 512
GRID_W = 64
N_MIXERS = 4
D_FF = 4 * D_MODEL
N_MOD = 6
RMS_EPS = 1e-6
ROPE_THETA = 10000.0

A_HEADS = 16
A_HALF_DIM = 64
A_V_DIM = 128
B_HEADS = 8
B_DK = 128
B_DV = 256
B_QK = B_HEADS * B_DK
B_VD = B_HEADS * B_DV
B_MAIN = 2 * B_QK + 2 * B_VD
C_HEADS = 16
C_HEAD_DIM = 128
NA_ROWS = 8
NA_COLS = 16
GRID_H = DEC_SEQ // GRID_W
D_HEADS = 32
D_KV_HEADS = 8
D_GROUP = 4
D_HEAD_DIM = 64
WINDOW = 128

N_PROMPT = BATCH * SEQ
N_SAMPLE = DEC_BATCH * DEC_SEQ
N_TOK = N_PROMPT + N_SAMPLE
N_COND = 8
SAMPLE_ROW0 = N_PROMPT // DEC_SEQ

LANES = 128
HALF = LANES // 2
NEG = -1e30
MIB = 1024 * 1024
F32 = jnp.float32
BF16 = jnp.bfloat16


def _params(sem, vmem_mib):
    return pltpu.CompilerParams(dimension_semantics=sem, vmem_limit_bytes=vmem_mib * MIB)


def _dot(a, b):
    return jnp.dot(a, b, preferred_element_type=F32)


def _dot_nt(a, b):
    return lax.dot_general(a, b, (((1,), (1,)), ((), ())), preferred_element_type=F32)


def _dot_tn(a, b):
    return lax.dot_general(a, b, (((0,), (0,)), ((), ())), preferred_element_type=F32)


def _rms(x, g):
    return x * lax.rsqrt(jnp.mean(x * x, axis=-1, keepdims=True) + RMS_EPS) * g


def _cond_index(row0):
    return jnp.where(row0 < N_PROMPT, 0, 1 + (row0 - N_PROMPT) // DEC_SEQ)


def _mod_kernel(c_ref, w_ref, b_ref, o_ref):
    c = c_ref[...]
    s = (c / (1.0 + jnp.exp(-c))).astype(BF16)
    o_ref[...] = _dot(s, w_ref[...].astype(BF16)) + b_ref[...]


def _modulation(cond, w_mod, b_mod):
    tn = 1024
    n = N_MOD * D_MODEL
    return pl.pallas_call(
        _mod_kernel,
        out_shape=jax.ShapeDtypeStruct((DEPTH, N_COND, n), F32),
        grid=(DEPTH, n // tn),
        in_specs=[
            pl.BlockSpec((N_COND, D_MODEL), lambda l, j: (0, 0)),
            pl.BlockSpec((None, D_MODEL, tn), lambda l, j: (l, 0, j)),
            pl.BlockSpec((None, 1, tn), lambda l, j: (l, 0, j)),
        ],
        out_specs=pl.BlockSpec((None, N_COND, tn), lambda l, j: (l, 0, j)),
        compiler_params=_params(("arbitrary", "arbitrary"), 40),
        name="modulation",
    )(cond, w_mod, b_mod.reshape(DEPTH, 1, n))


ROW_TILE = 512


def _mod_spec(layer, which, tile0=0):
    return pl.BlockSpec((None, None, None, 1, D_MODEL),
                        lambda i: (layer, _cond_index((i + tile0) * ROW_TILE), which, 0, 0))


def _gain_spec(layer, which):
    return pl.BlockSpec((None, None, 1, D_MODEL), lambda i: (layer, which, 0, 0))


def _adaln_kernel(xp_ref, xs_ref, g_ref, shift_ref, scale_ref, x_ref, h_ref):
    def emit(x):
        x_ref[...] = x
        h_ref[...] = (_rms(x, g_ref[...]) * (1.0 + scale_ref[...]) + shift_ref[...]).astype(BF16)

    @pl.when(pl.program_id(0) < N_PROMPT // ROW_TILE)
    def _():
        emit(xp_ref[...])

    @pl.when(pl.program_id(0) >= N_PROMPT // ROW_TILE)
    def _():
        emit(xs_ref[...])


def _adaln(x_prompt, x_sample, gains, mod, layer):
    p_tiles = N_PROMPT // ROW_TILE
    row = pl.BlockSpec((ROW_TILE, D_MODEL), lambda i: (i, 0))
    return pl.pallas_call(
        _adaln_kernel,
        out_shape=(jax.ShapeDtypeStruct((N_TOK, D_MODEL), F32), jax.ShapeDtypeStruct((N_TOK, D_MODEL), BF16)),
        grid=(N_TOK // ROW_TILE,),
        in_specs=[pl.BlockSpec((ROW_TILE, D_MODEL), lambda i: (jnp.minimum(i, p_tiles - 1), 0)),
                  pl.BlockSpec((ROW_TILE, D_MODEL), lambda i: (jnp.maximum(i - p_tiles, 0), 0)),
                  _gain_spec(layer, 0), _mod_spec(layer, 0), _mod_spec(layer, 1)],
        out_specs=(row, row),
        compiler_params=_params(("arbitrary",), 40),
        name="adaln",
    )(x_prompt.reshape(N_PROMPT, D_MODEL), x_sample.reshape(N_SAMPLE, D_MODEL), gains, mod, mod)


def _residual_adaln_kernel(x_ref, y_ref, gate_ref, gpost_ref, gpre_ref, shift_ref, scale_ref, xo_ref, h_ref):
    x = x_ref[...] + gate_ref[...] * _rms(y_ref[...], gpost_ref[...])
    xo_ref[...] = x
    h = _rms(x, gpre_ref[...]) * (1.0 + scale_ref[...]) + shift_ref[...]
    h_ref[...] = h.astype(BF16)


def _residual_kernel(x_ref, y_ref, gate_ref, gpost_ref, xo_ref):
    xo_ref[...] = x_ref[...] + gate_ref[...] * _rms(y_ref[...], gpost_ref[...])


def _residual(x, y, gains, mod, layer, gate_idx, post_idx, nxt):
    row = pl.BlockSpec((ROW_TILE, D_MODEL), lambda i: (i, 0))
    if nxt is None:
        outs = []
        for tile0, n_rows in ((0, N_PROMPT), (N_PROMPT // ROW_TILE, N_SAMPLE)):
            src = pl.BlockSpec((ROW_TILE, D_MODEL), lambda i, t=tile0: (i + t, 0))
            outs.append(pl.pallas_call(
                _residual_kernel, out_shape=jax.ShapeDtypeStruct((n_rows, D_MODEL), F32),
                grid=(n_rows // ROW_TILE,),
                in_specs=[src, src, _mod_spec(layer, gate_idx, tile0), _gain_spec(layer, post_idx)],
                out_specs=row,
                compiler_params=_params(("arbitrary",), 32), name="residual",
            )(x, y, mod, gains))
        return outs, None
    nl, ng, nshift, nscale = nxt
    return pl.pallas_call(
        _residual_adaln_kernel,
        out_shape=(jax.ShapeDtypeStruct((N_TOK, D_MODEL), F32), jax.ShapeDtypeStruct((N_TOK, D_MODEL), BF16)),
        grid=(N_TOK // ROW_TILE,),
        in_specs=[row, row, _mod_spec(layer, gate_idx), _gain_spec(layer, post_idx),
                  _gain_spec(nl, ng), _mod_spec(nl, nshift), _mod_spec(nl, nscale)],
        out_specs=(row, row),
        compiler_params=_params(("arbitrary",), 40), name="residual_adaln",
    )(x, y, mod, gains, gains, mod, mod)


PROJ_TM = 1024


def _proj_kernel(a_ref, w_ref, o_ref, wbf_ref):
    @pl.when(pl.program_id(1) == 0)
    def _():
        wbf_ref[...] = w_ref[...].astype(BF16)

    o_ref[...] = _dot(a_ref[...], wbf_ref[...]).astype(o_ref.dtype)


def _project(a, w, n=None, out_dtype=F32):
    k = w.shape[0]
    n = w.shape[1] if n is None else n
    tm = PROJ_TM
    tn = min(n, 1024)
    return pl.pallas_call(
        _proj_kernel,
        out_shape=jax.ShapeDtypeStruct((N_TOK, n), out_dtype),
        grid=(n // tn, N_TOK // tm),
        in_specs=[pl.BlockSpec((tm, k), lambda j, i: (i, 0)),
                  pl.BlockSpec((k, tn), lambda j, i: (0, j))],
        out_specs=pl.BlockSpec((tm, tn), lambda j, i: (i, j)),
        scratch_shapes=[pltpu.VMEM((k, tn), BF16)],
        compiler_params=_params(("arbitrary", "arbitrary"), 48),
        name="project",
    )(a, w)


OUT_TM = 512


def _out_proj_kernel(ap_ref, as_ref, w_ref, x_ref, gate_ref, gpost_ref, gpre_ref, shift_ref, scale_ref,
                     xo_ref, h_ref):
    def finish(a):
        x = x_ref[...] + gate_ref[...] * _rms(_dot(a, w_ref[...]), gpost_ref[...])
        xo_ref[...] = x
        h_ref[...] = (_rms(x, gpre_ref[...]) * (1.0 + scale_ref[...]) + shift_ref[...]).astype(BF16)

    @pl.when(pl.program_id(0) < N_PROMPT // OUT_TM)
    def _():
        finish(ap_ref[...])

    @pl.when(pl.program_id(0) >= N_PROMPT // OUT_TM)
    def _():
        finish(as_ref[...])


def _out_project(o_p, o_s, w_out, x, gains, mod, layer):
    tm = OUT_TM
    p_tiles = N_PROMPT // tm
    row = pl.BlockSpec((tm, D_MODEL), lambda i: (i, 0))

    def mod_spec(which):
        return pl.BlockSpec((None, None, None, 1, D_MODEL), lambda i: (layer, _cond_index(i * tm), which, 0, 0))

    return pl.pallas_call(
        _out_proj_kernel,
        out_shape=(jax.ShapeDtypeStruct((N_TOK, D_MODEL), F32), jax.ShapeDtypeStruct((N_TOK, D_MODEL), BF16)),
        grid=(N_TOK // tm,),
        in_specs=[pl.BlockSpec((tm, D_MODEL), lambda i: (jnp.minimum(i, p_tiles - 1), 0)),
                  pl.BlockSpec((tm, D_MODEL), lambda i: (jnp.maximum(i - p_tiles, 0), 0)),
                  pl.BlockSpec((D_MODEL, D_MODEL), lambda i: (0, 0)),
                  row, mod_spec(2), _gain_spec(layer, 1), _gain_spec(layer, 2), mod_spec(3), mod_spec(4)],
        out_specs=(row, row),
        compiler_params=_params(("arbitrary",), 56),
        name="out_project",
    )(o_p, o_s, w_out.astype(BF16), x, mod, gains, gains, mod, mod)


def _mlp_kernel(h_ref, w1_ref, w2_ref, o_ref):
    @pl.when(pl.program_id(1) == 0)
    def _():
        o_ref[...] = jnp.zeros_like(o_ref)

    u = jnp.maximum(_dot(h_ref[...], w1_ref[...].astype(BF16)), 0.0)
    o_ref[...] += _dot((u * u).astype(BF16), w2_ref[...].astype(BF16))


def _mlp(h, w1, w2, layer):
    tm, tf = 1024, 512
    return pl.pallas_call(
        _mlp_kernel,
        out_shape=jax.ShapeDtypeStruct((N_TOK, D_MODEL), F32),
        grid=(N_TOK // tm, D_FF // tf),
        in_specs=[pl.BlockSpec((tm, D_MODEL), lambda i, f: (i, 0)),
                  pl.BlockSpec((None, D_MODEL, tf), lambda i, f: (layer, 0, f)),
                  pl.BlockSpec((None, tf, D_MODEL), lambda i, f: (layer, f, 0))],
        out_specs=pl.BlockSpec((tm, D_MODEL), lambda i, f: (i, 0)),
        compiler_params=_params(("arbitrary", "arbitrary"), 56),
        name="mlp",
    )(h, w1, w2)


def _rope_tables():
    t = jnp.arange(DEC_SEQ)
    lane = np.arange(LANES)
    f = lane % 32
    first = f < 16
    inv = ROPE_THETA ** (-jnp.arange(16, dtype=F32) / 16)
    pos = jnp.where((lane % 64 < 32)[None, :], (t // GRID_W)[:, None], (t % GRID_W)[:, None]).astype(F32)
    ang = pos * inv[f % 16][None, :]
    cos, sin = jnp.cos(ang), jnp.sin(ang)
    sin_next = jnp.where(first[None, :], -sin, 0.0)
    sin_prev = jnp.where(first[None, :], 0.0, sin)
    return cos, sin_next, sin_prev


def _rope(x, cos, sin_next, sin_prev):
    return (x * cos + pltpu.roll(x, LANES - 16, axis=1) * sin_next
            + pltpu.roll(x, 16, axis=1) * sin_prev)


def _lane_lo(shape):
    return lax.broadcasted_iota(jnp.int32, shape, 1) < HALF


def _with_ones(v):
    return jnp.concatenate([v.astype(BF16), jnp.ones(v.shape, BF16)], axis=1)


def _attend(scores, vals_ones, sink=None):
    m = None
    for s in scores:
        mi = jnp.max(s, axis=-1, keepdims=True)
        m = mi if m is None else jnp.maximum(m, mi)
    if sink is not None:
        m = jnp.maximum(m, sink)
    acc = None
    for s, v in zip(scores, vals_ones):
        part = _dot(jnp.exp(s - m).astype(BF16), v)
        acc = part if acc is None else acc + part
    den = acc[:, LANES:]
    if sink is not None:
        den = den + jnp.exp(sink - m)
    return acc[:, :LANES] / den


CTX_HEADS = 8


A_SCALE = A_HALF_DIM ** -0.5


def _diff_lambda(lp, lam_init):
    a = jnp.sum(lp[0] * lp[1], axis=-1, keepdims=True)
    b = jnp.sum(lp[2] * lp[3], axis=-1, keepdims=True)
    return jnp.exp(a) - jnp.exp(b) + lam_init


def _diff_finish(o, subln_ref, lam_init):
    return (_rms(o, subln_ref[...]) * (1.0 - lam_init)).astype(BF16)


def _head_rows(head, n_tokens, n_heads):
    return pl.ds(head, n_tokens, stride=n_heads)


def _a_ctx_kernel(lam_ref, subln_ref, q_ref, k_ref, v_ref, o_ref, ko_ref, vo_ref, *, lam_init):
    lo = _lane_lo((SEQ, LANES))
    for hh in range(CTX_HEADS):
        cols = slice(hh * LANES, (hh + 1) * LANES)
        q = q_ref[:, cols] * A_SCALE
        k = k_ref[:, cols]
        v = v_ref[:, cols]
        cache_rows = _head_rows(pl.program_id(1) * CTX_HEADS + hh, SEQ, A_HEADS)
        ko_ref[cache_rows, :] = k
        vo_ref[cache_rows, :] = v
        kb, v1 = k.astype(BF16), _with_ones(v)
        o1 = _attend([_dot_nt(jnp.where(lo, q, 0.0).astype(BF16), kb)], [v1])
        o2 = _attend([_dot_nt(jnp.where(lo, 0.0, q).astype(BF16), kb)], [v1])
        lam = _diff_lambda(lam_ref[:, hh], lam_init)
        o_ref[:, cols] = _diff_finish(o1 - lam * o2, subln_ref, lam_init)


A_LAT_TQ = 256


def _a_lat_kernel(lam_ref, subln_ref, cos_ref, nxt_ref, prv_ref, q_ref, k_ref, v_ref, pk_ref, pv_ref, o_ref,
                  kb_ref, v1_ref, e_ref, *, lam_init):
    past_rows = _head_rows(pl.program_id(1), PAST_LEN, A_HEADS)
    kb_ref[:PAST_LEN, :] = pk_ref[past_rows, :].astype(BF16)
    kb_ref[PAST_LEN:, :] = _rope(k_ref[...], cos_ref[...], nxt_ref[...], prv_ref[...]).astype(BF16)
    v1_ref[:PAST_LEN, :] = _with_ones(pv_ref[past_rows, :])
    v1_ref[PAST_LEN:, :] = _with_ones(v_ref[...])
    lam = _diff_lambda(lam_ref[...], lam_init)
    lo = _lane_lo((A_LAT_TQ, LANES))
    n_blocks = DEC_SEQ // A_LAT_TQ

    def weights(i):
        rows = slice(i * A_LAT_TQ, (i + 1) * A_LAT_TQ)
        q = _rope(q_ref[rows, :], cos_ref[rows, :], nxt_ref[rows, :], prv_ref[rows, :]) * A_SCALE
        for half, keep in enumerate((lo, ~lo)):
            s = _dot_nt(jnp.where(keep, q, 0.0).astype(BF16), kb_ref[...])
            e_ref[i % 2, half] = jnp.exp(s - jnp.max(s, axis=-1, keepdims=True)).astype(BF16)

    def values(i):
        rows = slice(i * A_LAT_TQ, (i + 1) * A_LAT_TQ)
        outs = []
        for half in (0, 1):
            acc = _dot(e_ref[i % 2, half], v1_ref[...])
            outs.append(acc[:, :LANES] / acc[:, LANES:])
        o_ref[rows, :] = _diff_finish(outs[0] - lam * outs[1], subln_ref, lam_init)

    weights(0)
    for i in range(1, n_blocks):
        values(i - 1)
        weights(i)
    values(n_blocks - 1)


def _mixer_a(qkv, cache_k, cache_v, lam_p, subln, lam_init, rope):
    nh = A_HEADS
    lam4 = lam_p.reshape(4, nh, 1, A_HALF_DIM)
    sub2 = subln.reshape(1, A_V_DIM)
    cw = CTX_HEADS * LANES
    nt = nh // CTX_HEADS
    ctx_out = pl.BlockSpec((SEQ, cw), lambda b, h: (b, h))
    kv_out = pl.BlockSpec((SEQ * nh, A_V_DIM), lambda b, h: (b, 0))
    kv_shape = jax.ShapeDtypeStruct((N_PROMPT * nh, A_V_DIM), F32)
    o_p, k_new, v_new = pl.pallas_call(
        functools.partial(_a_ctx_kernel, lam_init=lam_init),
        out_shape=(jax.ShapeDtypeStruct((N_PROMPT, nh * A_V_DIM), BF16), kv_shape, kv_shape),
        grid=(BATCH, nt),
        in_specs=[pl.BlockSpec((4, CTX_HEADS, 1, A_HALF_DIM), lambda b, h: (0, h, 0, 0)),
                  pl.BlockSpec((1, A_V_DIM), lambda b, h: (0, 0)),
                  pl.BlockSpec((SEQ, cw), lambda b, h: (b, h)),
                  pl.BlockSpec((SEQ, cw), lambda b, h: (b, nt + h)),
                  pl.BlockSpec((SEQ, cw), lambda b, h: (b, 2 * nt + h))],
        out_specs=(ctx_out, kv_out, kv_out),
        compiler_params=_params(("arbitrary", "arbitrary"), 40),
        name="diff_attn_ctx",
    )(lam4, sub2, qkv, qkv, qkv)

    table = pl.BlockSpec((DEC_SEQ, LANES), lambda b, h: (0, 0))
    past = pl.BlockSpec((PAST_LEN * nh, A_V_DIM), lambda b, h: (b, 0))
    o_s = pl.pallas_call(
        functools.partial(_a_lat_kernel, lam_init=lam_init),
        out_shape=jax.ShapeDtypeStruct((N_SAMPLE, nh * A_V_DIM), BF16),
        grid=(DEC_BATCH, nh),
        in_specs=[pl.BlockSpec((4, None, 1, A_HALF_DIM), lambda b, h: (0, h, 0, 0)),
                  pl.BlockSpec((1, A_V_DIM), lambda b, h: (0, 0)),
                  table, table, table,
                  pl.BlockSpec((DEC_SEQ, LANES), lambda b, h: (SAMPLE_ROW0 + b, h)),
                  pl.BlockSpec((DEC_SEQ, LANES), lambda b, h: (SAMPLE_ROW0 + b, nh + h)),
                  pl.BlockSpec((DEC_SEQ, LANES), lambda b, h: (SAMPLE_ROW0 + b, 2 * nh + h)),
                  past, past],
        out_specs=pl.BlockSpec((DEC_SEQ, LANES), lambda b, h: (b, h)),
        scratch_shapes=[pltpu.VMEM((PAST_LEN + DEC_SEQ, LANES), BF16),
                        pltpu.VMEM((PAST_LEN + DEC_SEQ, 2 * LANES), BF16),
                        pltpu.VMEM((2, 2, A_LAT_TQ, PAST_LEN + DEC_SEQ), BF16)],
        compiler_params=_params(("arbitrary", "arbitrary"), 48),
        name="diff_attn_lat",
    )(lam4, sub2, *rope, qkv, qkv, qkv,
      cache_k.reshape(DEC_BATCH * PAST_LEN * nh, A_V_DIM), cache_v.reshape(DEC_BATCH * PAST_LEN * nh, A_V_DIM))
    return o_p, o_s, k_new, v_new


B_CHUNK = 256


def _log_sigmoid(x):
    return jnp.minimum(x, 0.0) - jnp.log(1.0 + jnp.exp(-jnp.abs(x)))


def _mlstm_kernel(*refs, heads, **static):
    def head_view(ref, hh, kind):
        if kind == "cols":
            width = ref.shape[1] // heads
            return ref.at[:, hh * width:(hh + 1) * width]
        if kind == "lead":
            return ref.at[hh]
        return ref.at[:, hh]

    kinds = (["cols"] * 4 + ["lead"] * 4 + ["cols"] + (["state"] * 3 if static["has_init"] else [])
             + ["cols"] + (["state"] * 3 if static["emit_state"] else []) + ["lead", "lead"])
    assert len(kinds) == len(refs)
    for hh in range(heads):
        _mlstm_head(*[head_view(r, hh, kd) for r, kd in zip(refs, kinds)], **static)


def _mlstm_head(*refs, n_chunks, has_init, emit_state):
    it = iter(refs)
    q_ref, k_ref, v_ref, og_ref, gcol_ref, grow_ref, bcol_ref, brow_ref, nw_ref = (next(it) for _ in range(9))
    if has_init:
        c0_ref, n0_ref, m0_ref = next(it), next(it), next(it)
    h_ref = next(it)
    if emit_state:
        c_out, n_out, m_out = next(it), next(it), next(it)
    hst_ref, vt_ref = next(it), next(it)

    L = B_CHUNK
    si = lax.broadcasted_iota(jnp.int32, (L, L), 0)
    ti = lax.broadcasted_iota(jnp.int32, (L, L), 1)
    gcol = gcol_ref[...] + brow_ref[...]
    grow = grow_ref[...] + bcol_ref[...]
    qscale = B_DK ** -0.5
    chunks = [slice(c * L, (c + 1) * L) for c in range(n_chunks)]
    for rows in chunks:
        vt_ref[:, rows] = v_ref[rows, :].T

    for d in (0, 1):
        feeds = (si <= ti) if d == 0 else (si >= ti)
        before = (ti <= si) if d == 0 else (ti >= si)
        last = L - 1 if d == 0 else 0
        if has_init:
            C, n, m = c0_ref[d], n0_ref[d], m0_ref[d][:, :1]
        else:
            C, n, m = jnp.zeros((B_DV, B_DK), F32), jnp.zeros((1, B_DK), F32), jnp.zeros((1, 1), F32)
        for ci in range(n_chunks):
            rows = chunks[ci if d == 0 else n_chunks - 1 - ci]
            qb = (q_ref[rows, :] * qscale).astype(BF16)
            kb = k_ref[rows, :].astype(BF16)
            vt = vt_ref[:, rows]
            ig_row = grow[2 * d:2 * d + 1, rows]
            lf_row = _log_sigmoid(grow[2 * d + 1:2 * d + 2, rows])
            ig_col = gcol[rows, 2 * d:2 * d + 1]
            lf_col = _log_sigmoid(gcol[rows, 2 * d + 1:2 * d + 2])
            b_row = jnp.sum(jnp.where(feeds, lf_col, 0.0), axis=0, keepdims=True)
            b_col = jnp.sum(jnp.where(before, lf_row, 0.0), axis=1, keepdims=True)
            dm = jnp.where(feeds, b_row + (ig_col - b_col), NEG)
            inter = b_row + m
            m_t = jnp.maximum(inter, jnp.max(dm, axis=0, keepdims=True))
            w = jnp.exp(dm - m_t)
            a_in = jnp.exp(inter - m_t)
            sw = _dot_nt(kb, qb) * w
            num = _dot(vt.astype(BF16), sw.astype(BF16)) + a_in * _dot_nt(C.astype(BF16), qb)
            nq = _dot_nt(jnp.broadcast_to(n, (8, B_DK)).astype(BF16), qb)[:1]
            den = jnp.sum(sw, axis=0, keepdims=True) + a_in * nq
            h = num * (1.0 / jnp.maximum(jnp.abs(den), jnp.exp(-m_t)))
            if d == 0:
                hst_ref[:, rows] = h
            else:
                hst_ref[:, rows] += h
            if emit_state or ci + 1 < n_chunks:
                m_last = m_t[:, last:last + 1]
                al = a_in[:, last:last + 1]
                wl = jnp.exp(b_row[:, last:last + 1] - b_row + ig_row - m_last)
                C = al * C + _dot((vt * wl).astype(BF16), kb)
                n = al * n + _dot(jnp.broadcast_to(wl, (8, L)).astype(BF16), kb)[:1]
                m = m_last
        if emit_state:
            c_out[d] = C
            n_out[d] = n
            m_out[d] = jnp.broadcast_to(m, (1, LANES))

    for rows in chunks:
        hsum = hst_ref[:, rows].T
        gate = 1.0 / (1.0 + jnp.exp(-og_ref[rows, :]))
        h_ref[rows, :] = (_rms(hsum, nw_ref[...]) * gate).astype(BF16)


def _mlstm_call(z, gcol, grow, bias, norm_w, seq, batch, row0, init, heads):
    nh = B_HEADS
    has_init = init is not None
    emit_state = not has_init
    bcol = bias.reshape(2, 2, nh).transpose(2, 0, 1).reshape(nh, 4, 1)
    brow = bcol.reshape(nh, 1, 4)
    hp = heads
    qk_tiles = B_QK // (hp * B_DK)
    vd0 = 2 * B_QK // (hp * B_DV)
    nt = nh // hp
    in_specs = [
        pl.BlockSpec((seq, hp * B_DK), lambda b, h: (row0 + b, h)),
        pl.BlockSpec((seq, hp * B_DK), lambda b, h: (row0 + b, qk_tiles + h)),
        pl.BlockSpec((seq, hp * B_DV), lambda b, h: (row0 + b, vd0 + h)),
        pl.BlockSpec((seq, hp * B_DV), lambda b, h: (row0 + b, vd0 + nt + h)),
        pl.BlockSpec((hp, seq, 4), lambda b, h: (h, row0 + b, 0)),
        pl.BlockSpec((hp, 4, seq), lambda b, h: (h, 0, row0 + b)),
        pl.BlockSpec((hp, 4, 1), lambda b, h: (h, 0, 0)),
        pl.BlockSpec((hp, 1, 4), lambda b, h: (h, 0, 0)),
        pl.BlockSpec((1, hp * B_DV), lambda b, h: (0, h)),
    ]
    args = [z, z, z, z, gcol, grow, bcol, brow, norm_w.reshape(1, B_VD)]
    state_c = pl.BlockSpec((None, 2, hp, B_DV, B_DK), lambda b, h: (b, 0, h, 0, 0))
    state_v = pl.BlockSpec((None, 2, hp, 1, B_DK), lambda b, h: (b, 0, h, 0, 0))
    if has_init:
        c0, n0, m0 = init
        in_specs += [state_c, state_v, state_v]
        args += [c0, n0.reshape(batch, 2, nh, 1, B_DK),
                 jnp.broadcast_to(m0[..., None, None], (batch, 2, nh, 1, B_DK))]
    h_shape = jax.ShapeDtypeStruct((batch * seq, B_VD), BF16)
    h_spec = pl.BlockSpec((seq, hp * B_DV), lambda b, h: (b, h))
    if emit_state:
        out_shape = (h_shape,
                     jax.ShapeDtypeStruct((batch, 2, nh, B_DV, B_DK), F32),
                     jax.ShapeDtypeStruct((batch, 2, nh, 1, B_DK), F32),
                     jax.ShapeDtypeStruct((batch, 2, nh, 1, B_DK), F32))
        out_specs = (h_spec, state_c, state_v, state_v)
    else:
        out_shape, out_specs = h_shape, h_spec
    return pl.pallas_call(
        functools.partial(_mlstm_kernel, heads=hp, n_chunks=seq // B_CHUNK, has_init=has_init,
                          emit_state=emit_state),
        out_shape=out_shape,
        grid=(batch, nt),
        in_specs=in_specs, out_specs=out_specs,
        scratch_shapes=[pltpu.VMEM((hp, B_DV, seq), F32), pltpu.VMEM((hp, B_DV, seq), F32)],
        compiler_params=_params(("arbitrary", "arbitrary"), 48),
        name="mlstm_ctx" if emit_state else "mlstm_lat",
    )(*args)


def _mixer_b(z, gates, state_c, state_n, state_m, bias, norm_w):
    nh = B_HEADS
    g = gates[:, :4 * nh].reshape(N_TOK, 4, nh)
    gcol = g.transpose(2, 0, 1)
    grow = g.transpose(2, 1, 0)
    o_p, c_new, n_new, m_new = _mlstm_call(z, gcol, grow, bias, norm_w, SEQ, BATCH, 0, None, heads=4)
    o_s = _mlstm_call(z, gcol, grow, bias, norm_w, DEC_SEQ, DEC_BATCH, SAMPLE_ROW0,
                      (state_c, state_n, state_m), heads=1)
    return o_p, o_s, c_new, n_new.reshape(BATCH, 2, nh, B_DK), m_new[:, :, :, 0, 0]


C_SCALE = C_HEAD_DIM ** -0.5


def _c_ctx_kernel(q_ref, k_ref, v_ref, o_ref, ko_ref, vo_ref):
    for hh in range(CTX_HEADS):
        cols = slice(hh * LANES, (hh + 1) * LANES)
        k = k_ref[:, cols]
        v = v_ref[:, cols]
        cache_rows = _head_rows(pl.program_id(1) * CTX_HEADS + hh, SEQ, C_HEADS)
        ko_ref[cache_rows, :] = k
        vo_ref[cache_rows, :] = v
        s = _dot_nt(q_ref[:, cols].astype(BF16), k.astype(BF16)) * C_SCALE
        o_ref[:, cols] = _attend([s], [_with_ones(v)]).astype(BF16)


def _na_row_start(r):
    return min(max(r - NA_ROWS // 2, 0), GRID_H - NA_ROWS)


def _na_row_groups():
    groups = []
    for r in range(GRID_H):
        if groups and _na_row_start(groups[-1][0]) == _na_row_start(r):
            groups[-1].append(r)
        else:
            groups.append([r])
    return groups


C_LAT_HEADS = 2


def _c_lat_kernel(bias_ref, q_ref, k_ref, v_ref, pk_ref, pv_ref, o_ref, *scratch):
    for hh in range(C_LAT_HEADS):
        cols = slice(hh * LANES, (hh + 1) * LANES)
        _c_lat_head(pl.program_id(1) * C_LAT_HEADS + hh, bias_ref.at[hh], q_ref.at[:, cols], k_ref.at[:, cols],
                    v_ref.at[:, cols], pk_ref, pv_ref, o_ref.at[:, cols], *[s.at[hh] for s in scratch])


def _c_lat_head(head, bias_ref, q_ref, k_ref, v_ref, pk_ref, pv_ref, o_ref, kb_ref, v1_ref, sc_ref, ec_ref, acc_ref):
    n_win = NA_ROWS * GRID_W
    kb_ref[...] = k_ref[...].astype(BF16)
    v1_ref[...] = _with_ones(v_ref[...])
    past_rows = _head_rows(head, PAST_LEN, C_HEADS)
    sc_ref[...] = _dot_nt(q_ref[...].astype(BF16), pk_ref[past_rows, :].astype(BF16)) * C_SCALE
    for rows_g in _na_row_groups():
        r0 = _na_row_start(rows_g[0])
        rows = slice(rows_g[0] * GRID_W, (rows_g[-1] + 1) * GRID_W)
        win = slice(r0 * GRID_W, r0 * GRID_W + n_win)
        strips = []
        for r in rows_g:
            strip = NA_ROWS - 1 - (r - r0)
            even = strip - strip % 2
            strips.append(bias_ref[strip % 2, :, even * GRID_W:even * GRID_W + n_win])
        bias = strips[0] if len(strips) == 1 else jnp.concatenate(strips, axis=0)
        s_n = _dot_nt(q_ref[rows, :].astype(BF16), kb_ref[win, :]) * C_SCALE + bias
        s_c = sc_ref[rows, :]
        m = jnp.maximum(jnp.max(s_n, axis=-1, keepdims=True), jnp.max(s_c, axis=-1, keepdims=True))
        ec_ref[rows, :] = jnp.exp(s_c - m).astype(BF16)
        acc_ref[rows, :] = _dot(jnp.exp(s_n - m).astype(BF16), v1_ref[win, :])
    acc = acc_ref[...] + _dot(ec_ref[...], _with_ones(pv_ref[past_rows, :]))
    o_ref[...] = (acc[:, :LANES] / acc[:, LANES:]).astype(BF16)


def _na_bias(rpb):
    n_drow, n_dcol = 2 * NA_ROWS - 1, 2 * NA_COLS - 1
    cq = np.arange(GRID_W)[:, None]
    kc = np.arange(GRID_W)[None, :]
    cstart = np.clip(cq - NA_COLS // 2, 0, GRID_W - NA_COLS)
    col_ok = (kc >= cstart) & (kc < cstart + NA_COLS)
    dcol = np.clip(kc - cq, -(NA_COLS - 1), NA_COLS - 1) + NA_COLS - 1
    onehot = (dcol[:, :, None] == np.arange(n_dcol)).astype(np.float32)
    strips = jnp.einsum('hab,qkb->hqak', rpb.astype(F32), onehot, precision=lax.Precision.HIGHEST)
    strips = jnp.where(col_ok[None, :, None, :], strips, NEG).reshape(C_HEADS, GRID_W, n_drow * GRID_W)
    strips = jnp.pad(strips, ((0, 0), (0, 0), (0, 2 * GRID_W)))
    return jnp.stack([strips[:, :, :(n_drow + 1) * GRID_W], strips[:, :, GRID_W:]], axis=1)


def _mixer_c(qkv, cache_k, cache_v, rpb):
    nh = C_HEADS
    cw = CTX_HEADS * LANES
    nt = nh // CTX_HEADS
    ctx_out = pl.BlockSpec((SEQ, cw), lambda b, h: (b, h))
    kv_out = pl.BlockSpec((SEQ * nh, C_HEAD_DIM), lambda b, h: (b, 0))
    kv_shape = jax.ShapeDtypeStruct((N_PROMPT * nh, C_HEAD_DIM), F32)
    o_p, k_new, v_new = pl.pallas_call(
        _c_ctx_kernel,
        out_shape=(jax.ShapeDtypeStruct((N_PROMPT, D_MODEL), BF16), kv_shape, kv_shape),
        grid=(BATCH, nt),
        in_specs=[pl.BlockSpec((SEQ, cw), lambda b, h: (b, h)),
                  pl.BlockSpec((SEQ, cw), lambda b, h: (b, nt + h)),
                  pl.BlockSpec((SEQ, cw), lambda b, h: (b, 2 * nt + h))],
        out_specs=(ctx_out, kv_out, kv_out),
        compiler_params=_params(("arbitrary", "arbitrary"), 40),
        name="na_ctx",
    )(qkv, qkv, qkv)

    past = pl.BlockSpec((PAST_LEN * nh, C_HEAD_DIM), lambda b, h: (b, 0))
    lh = C_LAT_HEADS
    lw, nl = lh * LANES, nh // lh
    o_s = pl.pallas_call(
        _c_lat_kernel,
        out_shape=jax.ShapeDtypeStruct((N_SAMPLE, D_MODEL), BF16),
        grid=(DEC_BATCH, nl),
        in_specs=[pl.BlockSpec((lh, 2, GRID_W, 2 * NA_ROWS * GRID_W), lambda b, h: (h, 0, 0, 0)),
                  pl.BlockSpec((DEC_SEQ, lw), lambda b, h: (SAMPLE_ROW0 + b, h)),
                  pl.BlockSpec((DEC_SEQ, lw), lambda b, h: (SAMPLE_ROW0 + b, nl + h)),
                  pl.BlockSpec((DEC_SEQ, lw), lambda b, h: (SAMPLE_ROW0 + b, 2 * nl + h)),
                  past, past],
        out_specs=pl.BlockSpec((DEC_SEQ, lw), lambda b, h: (b, h)),
        scratch_shapes=[pltpu.VMEM((lh, DEC_SEQ, LANES), BF16), pltpu.VMEM((lh, DEC_SEQ, 2 * LANES), BF16),
                        pltpu.VMEM((lh, DEC_SEQ, PAST_LEN), F32), pltpu.VMEM((lh, DEC_SEQ, PAST_LEN), BF16),
                        pltpu.VMEM((lh, DEC_SEQ, 2 * LANES), F32)],
        compiler_params=_params(("arbitrary", "arbitrary"), 56),
        name="na_lat",
    )(_na_bias(rpb), qkv, qkv, qkv,
      cache_k.reshape(DEC_BATCH * PAST_LEN * nh, C_HEAD_DIM), cache_v.reshape(DEC_BATCH * PAST_LEN * nh, C_HEAD_DIM))
    return o_p, o_s, k_new, v_new


D_SCALE = D_HEAD_DIM ** -0.5
D_QBLOCK = 128
D_BAND = 3 * D_QBLOCK


D_CTX_PAIRS = 2


def _d_ctx_kernel(sink_ref, q_ref, k_ref, v_ref, o_ref, ko_ref, vo_ref):
    ko_ref[...] = k_ref[...]
    vo_ref[...] = v_ref[...]
    lo = _lane_lo((SEQ, LANES))
    for pp in range(D_CTX_PAIRS):
        pair = pl.program_id(1) * D_CTX_PAIRS + pp
        kv_cols = slice(pp * LANES, (pp + 1) * LANES)
        kb, v1 = k_ref[:, kv_cols].astype(BF16), _with_ones(v_ref[:, kv_cols])
        for tile in range(4):
            kv = tile // 2
            keep = lo if kv == 0 else ~lo
            cols = slice((4 * pp + tile) * LANES, (4 * pp + tile + 1) * LANES)
            qt = q_ref[:, cols] * D_SCALE
            halves = []
            for e in (0, 1):
                qe = qt if e == kv else pltpu.roll(qt, HALF, axis=1)
                s = _dot_nt(jnp.where(keep, qe, 0.0).astype(BF16), kb)
                o = _attend([s], [v1], sink=sink_ref[pair * 2 * D_GROUP + tile * 2 + e])
                halves.append(o if e == kv else pltpu.roll(o, HALF, axis=1))
            o_ref[:, cols] = jnp.where(lo, halves[0], halves[1]).astype(BF16)


def _d_lat_kernel(sink_ref, cos_ref, nxt_ref, prv_ref, q_ref, k_ref, v_ref, pk_ref, pv_ref, o_ref,
                  kb_ref, v1_ref, pkb_ref, pv1_ref, e_ref, es_ref):
    pair = pl.program_id(1)
    kb_ref[...] = _rope(k_ref[...], cos_ref[...], nxt_ref[...], prv_ref[...]).astype(BF16)
    v1_ref[...] = _with_ones(v_ref[...])
    pkb_ref[...] = pk_ref[...].astype(BF16)
    pv1_ref[...] = _with_ones(pv_ref[...])
    tq = D_QBLOCK
    lo = _lane_lo((tq, LANES))
    n_blocks = DEC_SEQ // tq

    def band_of(bi):
        start = min(max((bi - 1) * tq, 0), DEC_SEQ - D_BAND)
        return start, slice(start, start + D_BAND)

    def weights(bi):
        rows = slice(bi * tq, (bi + 1) * tq)
        start, band = band_of(bi)
        cos, nxt, prv = cos_ref[rows, :], nxt_ref[rows, :], prv_ref[rows, :]
        qpos = bi * tq + lax.broadcasted_iota(jnp.int32, (tq, D_BAND), 0)
        kpos = start + lax.broadcasted_iota(jnp.int32, (tq, D_BAND), 1)
        mask = jnp.where(jnp.abs(qpos - kpos) <= WINDOW, 0.0, NEG)
        mask4 = jnp.concatenate([mask] * D_GROUP, axis=0)
        tiles = [_rope(q_ref[rows, t * LANES:(t + 1) * LANES], cos, nxt, prv) * D_SCALE for t in range(4)]
        for kv in (0, 1):
            keep = lo if kv == 0 else ~lo
            stack, sinks = [], []
            for t in (2 * kv, 2 * kv + 1):
                for e in (0, 1):
                    qe = tiles[t] if e == kv else pltpu.roll(tiles[t], HALF, axis=1)
                    stack.append(jnp.where(keep, qe, 0.0))
                    sinks.append(jnp.full((tq, 1), sink_ref[pair * 2 * D_GROUP + t * 2 + e], F32))
            qs = jnp.concatenate(stack, axis=0).astype(BF16)
            sink = jnp.concatenate(sinks, axis=0)
            s_band = _dot_nt(qs, kb_ref[band, :]) + mask4
            s_past = _dot_nt(qs, pkb_ref[...])
            m = jnp.maximum(jnp.maximum(jnp.max(s_band, axis=-1, keepdims=True),
                                        jnp.max(s_past, axis=-1, keepdims=True)), sink)
            e_ref[bi % 2, kv, :, :D_BAND] = jnp.exp(s_band - m).astype(BF16)
            e_ref[bi % 2, kv, :, D_BAND:] = jnp.exp(s_past - m).astype(BF16)
            es_ref[bi % 2, kv] = jnp.exp(sink - m)

    def values(bi):
        rows = slice(bi * tq, (bi + 1) * tq)
        _, band = band_of(bi)
        outs = {}
        for kv in (0, 1):
            acc = (_dot(e_ref[bi % 2, kv, :, :D_BAND], v1_ref[band, :])
                   + _dot(e_ref[bi % 2, kv, :, D_BAND:], pv1_ref[...]))
            o = acc[:, :LANES] / (acc[:, LANES:] + es_ref[bi % 2, kv])
            for i, t in enumerate((2 * kv, 2 * kv + 1)):
                for e in (0, 1):
                    piece = o[(2 * i + e) * tq:(2 * i + e + 1) * tq, :]
                    outs[(t, e)] = piece if e == kv else pltpu.roll(piece, HALF, axis=1)
        for t in range(4):
            o_ref[rows, t * LANES:(t + 1) * LANES] = jnp.where(lo, outs[(t, 0)], outs[(t, 1)]).astype(BF16)

    weights(0)
    for bi in range(1, n_blocks):
        values(bi - 1)
        weights(bi)
    values(n_blocks - 1)


def _mixer_d(qkv, cache_k, cache_v, sink, rope):
    n_pairs = D_KV_HEADS // 2
    qw = 2 * D_GROUP * D_HEAD_DIM
    k0 = D_HEADS * D_HEAD_DIM // LANES
    v0 = k0 + n_pairs
    kvw = D_KV_HEADS * D_HEAD_DIM
    smem = pl.BlockSpec(memory_space=pltpu.SMEM)
    sink = sink.astype(F32)
    cp = D_CTX_PAIRS
    kv_out = pl.BlockSpec((SEQ, cp * LANES), lambda b, p: (b, p))
    kv_shape = jax.ShapeDtypeStruct((N_PROMPT, kvw), F32)
    o_p, k_new, v_new = pl.pallas_call(
        _d_ctx_kernel,
        out_shape=(jax.ShapeDtypeStruct((N_PROMPT, D_MODEL), BF16), kv_shape, kv_shape),
        grid=(BATCH, n_pairs // cp),
        in_specs=[smem,
                  pl.BlockSpec((SEQ, cp * qw), lambda b, p: (b, p)),
                  pl.BlockSpec((SEQ, cp * LANES), lambda b, p: (b, k0 // cp + p)),
                  pl.BlockSpec((SEQ, cp * LANES), lambda b, p: (b, v0 // cp + p))],
        out_specs=(pl.BlockSpec((SEQ, cp * qw), lambda b, p: (b, p)), kv_out, kv_out),
        compiler_params=_params(("arbitrary", "arbitrary"), 40),
        name="gqa_ctx",
    )(sink, qkv, qkv, qkv)

    table = pl.BlockSpec((DEC_SEQ, LANES), lambda b, p: (0, 0))
    past = pl.BlockSpec((PAST_LEN, LANES), lambda b, p: (b, p))
    o_s = pl.pallas_call(
        _d_lat_kernel,
        out_shape=jax.ShapeDtypeStruct((N_SAMPLE, D_MODEL), BF16),
        grid=(DEC_BATCH, n_pairs),
        in_specs=[smem, table, table, table,
                  pl.BlockSpec((DEC_SEQ, qw), lambda b, p: (SAMPLE_ROW0 + b, p)),
                  pl.BlockSpec((DEC_SEQ, LANES), lambda b, p: (SAMPLE_ROW0 + b, k0 + p)),
                  pl.BlockSpec((DEC_SEQ, LANES), lambda b, p: (SAMPLE_ROW0 + b, v0 + p)),
                  past, past],
        out_specs=pl.BlockSpec((DEC_SEQ, qw), lambda b, p: (b, p)),
        scratch_shapes=[pltpu.VMEM((DEC_SEQ, LANES), BF16), pltpu.VMEM((DEC_SEQ, 2 * LANES), BF16),
                        pltpu.VMEM((PAST_LEN, LANES), BF16), pltpu.VMEM((PAST_LEN, 2 * LANES), BF16),
                        pltpu.VMEM((2, 2, D_GROUP * D_QBLOCK, D_BAND + PAST_LEN), BF16),
                        pltpu.VMEM((2, 2, D_GROUP * D_QBLOCK, 1), F32)],
        compiler_params=_params(("arbitrary", "arbitrary"), 40),
        name="gqa_lat",
    )(sink, *rope, qkv, qkv, qkv,
      cache_k.reshape(DEC_BATCH * PAST_LEN, kvw), cache_v.reshape(DEC_BATCH * PAST_LEN, kvw))
    return o_p, o_s, k_new, v_new


def kernel(x_prompt, x_sample, cache_a_k, cache_a_v, state_b_C, state_b_n, state_b_m, cache_c_k, cache_c_v,
           cache_d_k, cache_d_v, c, c_ctx, w_mod, b_mod, g_norm, w_ff1, w_ff2, a_w_in, a_w_out, a_lambda,
           a_subln, b_w_in, b_gate_bias, b_w_out, b_norm, c_w_in, c_w_out, c_rpb, d_w_in, d_w_out, d_sink):
    cond = jnp.concatenate([c_ctx[None, :], c, jnp.zeros((N_COND - 1 - DEC_BATCH, D_MODEL), F32)], axis=0)
    mod = _modulation(cond, w_mod, b_mod).reshape(DEPTH, N_COND, N_MOD, 1, D_MODEL)
    gains = g_norm.reshape(DEPTH, 4, 1, D_MODEL)
    rope = _rope_tables()
    new = {name: [] for name in ("a_k", "a_v", "b_C", "b_n", "b_m", "c_k", "c_v", "d_k", "d_v")}

    x, h = _adaln(x_prompt, x_sample, gains, mod, 0)
    for i in range(DEPTH):
        kind, j = i % N_MIXERS, i // N_MIXERS
        if kind == 0:
            lam_init = 0.8 - 0.6 * math.exp(-0.3 * i)
            qkv = _project(h, a_w_in[j])
            o_p, o_s, k_new, v_new = _mixer_a(qkv, cache_a_k[:, j], cache_a_v[:, j], a_lambda[j], a_subln[j],
                                              lam_init, rope)
            w_out = a_w_out[j]
            new["a_k"].append(k_new.reshape(BATCH, SEQ, A_HEADS, A_V_DIM))
            new["a_v"].append(v_new.reshape(BATCH, SEQ, A_HEADS, A_V_DIM))
        elif kind == 1:
            w_in = b_w_in[j]
            z = _project(h, w_in, n=B_MAIN)
            w_gate = jnp.pad(w_in[:, B_MAIN:], ((0, 0), (0, LANES - 4 * B_HEADS)))
            gates = _project(h, w_gate)
            o_p, o_s, c_new, n_new, m_new = _mixer_b(z, gates, state_b_C[:, j], state_b_n[:, j], state_b_m[:, j],
                                                     b_gate_bias[j], b_norm[j])
            w_out = b_w_out[j]
            new["b_C"].append(c_new)
            new["b_n"].append(n_new)
            new["b_m"].append(m_new)
        elif kind == 2:
            qkv = _project(h, c_w_in[j])
            o_p, o_s, k_new, v_new = _mixer_c(qkv, cache_c_k[:, j], cache_c_v[:, j], c_rpb[j])
            w_out = c_w_out[j]
            new["c_k"].append(k_new.reshape(BATCH, SEQ, C_HEADS, C_HEAD_DIM))
            new["c_v"].append(v_new.reshape(BATCH, SEQ, C_HEADS, C_HEAD_DIM))
        else:
            qkv = _project(h, d_w_in[j])
            o_p, o_s, k_new, v_new = _mixer_d(qkv, cache_d_k[:, j], cache_d_v[:, j], d_sink[j], rope)
            w_out = d_w_out[j]
            new["d_k"].append(k_new.reshape(BATCH, SEQ, D_KV_HEADS, D_HEAD_DIM))
            new["d_v"].append(v_new.reshape(BATCH, SEQ, D_KV_HEADS, D_HEAD_DIM))
        x, h = _out_project(o_p, o_s, w_out, x, gains, mod, i)
        y = _mlp(h, w_ff1, w_ff2, i)
        x, h = _residual(x, y, gains, mod, i, 5, 3, (i + 1, 0, 0, 1) if i + 1 < DEPTH else None)

    x_p, x_s = x
    stack = lambda name: jnp.stack(new[name], axis=1)
    return (x_p.reshape(BATCH, SEQ, D_MODEL), x_s.reshape(DEC_BATCH, DEC_SEQ, D_MODEL),
            stack("a_k"), stack("a_v"), stack("b_C"), stack("b_n"), stack("b_m"),
            stack("c_k"), stack("c_v"), stack("d_k"), stack("d_v"))
```

```python
import functools
import math

import numpy as np
import jax
import jax.numpy as jnp
from jax import lax
from jax.experimental import pallas as pl
from jax.experimental.pallas import tpu as pltpu

D_MODEL = 2048
BATCH = 16
SEQ = 256
DEPTH = 4
DEC_BATCH = 4
DEC_SEQ = 1024
PAST_LEN = 512
GRID_W = 64
N_MIXERS = 4
D_FF = 4 * D_MODEL
N_MOD = 6
RMS_EPS = 1e-6
ROPE_THETA = 10000.0

A_HEADS = 16
A_HALF_DIM = 64
A_V_DIM = 128
B_HEADS = 8
B_DK = 128
B_DV = 256
B_QK = B_HEADS * B_DK
B_VD = B_HEADS * B_DV
B_MAIN = 2 * B_QK + 2 * B_VD
C_HEADS = 16
C_HEAD_DIM = 128
NA_ROWS = 8
NA_COLS = 16
GRID_H = DEC_SEQ // GRID_W
D_HEADS = 32
D_KV_HEADS = 8
D_GROUP = 4
D_HEAD_DIM = 64
WINDOW = 128

N_PROMPT = BATCH * SEQ
N_SAMPLE = DEC_BATCH * DEC_SEQ
N_TOK = N_PROMPT + N_SAMPLE
N_COND = 8
SAMPLE_ROW0 = N_PROMPT // DEC_SEQ

LANES = 128
HALF = LANES // 2
NEG = -1e30
MIB = 1024 * 1024
F32 = jnp.float32
BF16 = jnp.bfloat16


def _params(sem, vmem_mib):
    return pltpu.CompilerParams(dimension_semantics=sem, vmem_limit_bytes=vmem_mib * MIB)


def _dot(a, b):
    return jnp.dot(a, b, preferred_element_type=F32)


def _dot_nt(a, b):
    return lax.dot_general(a, b, (((1,), (1,)), ((), ())), preferred_element_type=F32)


def _dot_tn(a, b):
    return lax.dot_general(a, b, (((0,), (0,)), ((), ())), preferred_element_type=F32)


def _rms(x, g):
    return x * lax.rsqrt(jnp.mean(x * x, axis=-1, keepdims=True) + RMS_EPS) * g


def _cond_index(row0):
    return jnp.where(row0 < N_PROMPT, 0, 1 + (row0 - N_PROMPT) // DEC_SEQ)


def _mod_kernel(c_ref, w_ref, b_ref, o_ref):
    c = c_ref[...]
    s = (c / (1.0 + jnp.exp(-c))).astype(BF16)
    o_ref[...] = _dot(s, w_ref[...].astype(BF16)) + b_ref[...]


def _modulation(cond, w_mod, b_mod):
    tn = 1024
    n = N_MOD * D_MODEL
    return pl.pallas_call(
        _mod_kernel,
        out_shape=jax.ShapeDtypeStruct((DEPTH, N_COND, n), F32),
        grid=(DEPTH, n // tn),
        in_specs=[
            pl.BlockSpec((N_COND, D_MODEL), lambda l, j: (0, 0)),
            pl.BlockSpec((None, D_MODEL, tn), lambda l, j: (l, 0, j)),
            pl.BlockSpec((None, 1, tn), lambda l, j: (l, 0, j)),
        ],
        out_specs=pl.BlockSpec((None, N_COND, tn), lambda l, j: (l, 0, j)),
        compiler_params=_params(("arbitrary", "arbitrary"), 40),
        name="modulation",
    )(cond, w_mod, b_mod.reshape(DEPTH, 1, n))


ROW_TILE = 512


def _mod_spec(layer, which, tile0=0):
    return pl.BlockSpec((None, None, None, 1, D_MODEL),
                        lambda i: (layer, _cond_index((i + tile0) * ROW_TILE), which, 0, 0))


def _gain_spec(layer, which):
    return pl.BlockSpec((None, None, 1, D_MODEL), lambda i: (layer, which, 0, 0))


def _adaln_kernel(xp_ref, xs_ref, g_ref, shift_ref, scale_ref, x_ref, h_ref):
    def emit(x):
        x_ref[...] = x
        h_ref[...] = (_rms(x, g_ref[...]) * (1.0 + scale_ref[...]) + shift_ref[...]).astype(BF16)

    @pl.when(pl.program_id(0) < N_PROMPT // ROW_TILE)
    def _():
        emit(xp_ref[...])

    @pl.when(pl.program_id(0) >= N_PROMPT // ROW_TILE)
    def _():
        emit(xs_ref[...])


def _adaln(x_prompt, x_sample, gains, mod, layer):
    p_tiles = N_PROMPT // ROW_TILE
    row = pl.BlockSpec((ROW_TILE, D_MODEL), lambda i: (i, 0))
    return pl.pallas_call(
        _adaln_kernel,
        out_shape=(jax.ShapeDtypeStruct((N_TOK, D_MODEL), F32), jax.ShapeDtypeStruct((N_TOK, D_MODEL), BF16)),
        grid=(N_TOK // ROW_TILE,),
        in_specs=[pl.BlockSpec((ROW_TILE, D_MODEL), lambda i: (jnp.minimum(i, p_tiles - 1), 0)),
                  pl.BlockSpec((ROW_TILE, D_MODEL), lambda i: (jnp.maximum(i - p_tiles, 0), 0)),
                  _gain_spec(layer, 0), _mod_spec(layer, 0), _mod_spec(layer, 1)],
        out_specs=(row, row),
        compiler_params=_params(("arbitrary",), 40),
        name="adaln",
    )(x_prompt.reshape(N_PROMPT, D_MODEL), x_sample.reshape(N_SAMPLE, D_MODEL), gains, mod, mod)


def _residual_adaln_kernel(x_ref, y_ref, gate_ref, gpost_ref, gpre_ref, shift_ref, scale_ref, xo_ref, h_ref):
    x = x_ref[...] + gate_ref[...] * _rms(y_ref[...], gpost_ref[...])
    xo_ref[...] = x
    h = _rms(x, gpre_ref[...]) * (1.0 + scale_ref[...]) + shift_ref[...]
    h_ref[...] = h.astype(BF16)


def _residual_kernel(x_ref, y_ref, gate_ref, gpost_ref, xo_ref):
    xo_ref[...] = x_ref[...] + gate_ref[...] * _rms(y_ref[...], gpost_ref[...])


def _residual(x, y, gains, mod, layer, gate_idx, post_idx, nxt):
    row = pl.BlockSpec((ROW_TILE, D_MODEL), lambda i: (i, 0))
    if nxt is None:
        outs = []
        for tile0, n_rows in ((0, N_PROMPT), (N_PROMPT // ROW_TILE, N_SAMPLE)):
            src = pl.BlockSpec((ROW_TILE, D_MODEL), lambda i, t=tile0: (i + t, 0))
            outs.append(pl.pallas_call(
                _residual_kernel, out_shape=jax.ShapeDtypeStruct((n_rows, D_MODEL), F32),
                grid=(n_rows // ROW_TILE,),
                in_specs=[src, src, _mod_spec(layer, gate_idx, tile0), _gain_spec(layer, post_idx)],
                out_specs=row,
                compiler_params=_params(("arbitrary",), 32), name="residual",
            )(x, y, mod, gains))
        return outs, None
    nl, ng, nshift, nscale = nxt
    return pl.pallas_call(
        _residual_adaln_kernel,
        out_shape=(jax.ShapeDtypeStruct((N_TOK, D_MODEL), F32), jax.ShapeDtypeStruct((N_TOK, D_MODEL), BF16)),
        grid=(N_TOK // ROW_TILE,),
        in_specs=[row, row, _mod_spec(layer, gate_idx), _gain_spec(layer, post_idx),
                  _gain_spec(nl, ng), _mod_spec(nl, nshift), _mod_spec(nl, nscale)],
        out_specs=(row, row),
        compiler_params=_params(("arbitrary",), 40), name="residual_adaln",
    )(x, y, mod, gains, gains, mod, mod)


PROJ_TM = 1024


def _proj_kernel(a_ref, w_ref, o_ref, wbf_ref):
    @pl.when(pl.program_id(1) == 0)
    def _():
        wbf_ref[...] = w_ref[...].astype(BF16)

    o_ref[...] = _dot(a_ref[...], wbf_ref[...]).astype(o_ref.dtype)


def _project(a, w, n=None, out_dtype=F32):
    k = w.shape[0]
    n = w.shape[1] if n is None else n
    tm = PROJ_TM
    tn = min(n, 1024)
    return pl.pallas_call(
        _proj_kernel,
        out_shape=jax.ShapeDtypeStruct((N_TOK, n), out_dtype),
        grid=(n // tn, N_TOK // tm),
        in_specs=[pl.BlockSpec((tm, k), lambda j, i: (i, 0)),
                  pl.BlockSpec((k, tn), lambda j, i: (0, j))],
        out_specs=pl.BlockSpec((tm, tn), lambda j, i: (i, j)),
        scratch_shapes=[pltpu.VMEM((k, tn), BF16)],
        compiler_params=_params(("arbitrary", "arbitrary"), 48),
        name="project",
    )(a, w)


OUT_TM = 512


def _out_proj_kernel(ap_ref, as_ref, w_ref, x_ref, gate_ref, gpost_ref, gpre_ref, shift_ref, scale_ref,
                     xo_ref, h_ref):
    def finish(a):
        x = x_ref[...] + gate_ref[...] * _rms(_dot(a, w_ref[...]), gpost_ref[...])
        xo_ref[...] = x
        h_ref[...] = (_rms(x, gpre_ref[...]) * (1.0 + scale_ref[...]) + shift_ref[...]).astype(BF16)

    @pl.when(pl.program_id(0) < N_PROMPT // OUT_TM)
    def _():
        finish(ap_ref[...])

    @pl.when(pl.program_id(0) >= N_PROMPT // OUT_TM)
    def _():
        finish(as_ref[...])


def _out_project(o_p, o_s, w_out, x, gains, mod, layer):
    tm = OUT_TM
    p_tiles = N_PROMPT // tm
    row = pl.BlockSpec((tm, D_MODEL), lambda i: (i, 0))

    def mod_spec(which):
        return pl.BlockSpec((None, None, None, 1, D_MODEL), lambda i: (layer, _cond_index(i * tm), which, 0, 0))

    return pl.pallas_call(
        _out_proj_kernel,
        out_shape=(jax.ShapeDtypeStruct((N_TOK, D_MODEL), F32), jax.ShapeDtypeStruct((N_TOK, D_MODEL), BF16)),
        grid=(N_TOK // tm,),
        in_specs=[pl.BlockSpec((tm, D_MODEL), lambda i: (jnp.minimum(i, p_tiles - 1), 0)),
                  pl.BlockSpec((tm, D_MODEL), lambda i: (jnp.maximum(i - p_tiles, 0), 0)),
                  pl.BlockSpec((D_MODEL, D_MODEL), lambda i: (0, 0)),
                  row, mod_spec(2), _gain_spec(layer, 1), _gain_spec(layer, 2), mod_spec(3), mod_spec(4)],
        out_specs=(row, row),
        compiler_params=_params(("arbitrary",), 56),
        name="out_project",
    )(o_p, o_s, w_out.astype(BF16), x, mod, gains, gains, mod, mod)


def _mlp_kernel(h_ref, w1_ref, w2_ref, o_ref):
    @pl.when(pl.program_id(1) == 0)
    def _():
        o_ref[...] = jnp.zeros_like(o_ref)

    u = jnp.maximum(_dot(h_ref[...], w1_ref[...].astype(BF16)), 0.0)
    o_ref[...] += _dot((u * u).astype(BF16), w2_ref[...].astype(BF16))


def _mlp(h, w1, w2, layer):
    tm, tf = 1024, 512
    return pl.pallas_call(
        _mlp_kernel,
        out_shape=jax.ShapeDtypeStruct((N_TOK, D_MODEL), F32),
        grid=(N_TOK // tm, D_FF // tf),
        in_specs=[pl.BlockSpec((tm, D_MODEL), lambda i, f: (i, 0)),
                  pl.BlockSpec((None, D_MODEL, tf), lambda i, f: (layer, 0, f)),
                  pl.BlockSpec((None, tf, D_MODEL), lambda i, f: (layer, f, 0))],
        out_specs=pl.BlockSpec((tm, D_MODEL), lambda i, f: (i, 0)),
        compiler_params=_params(("arbitrary", "arbitrary"), 56),
        name="mlp",
    )(h, w1, w2)


def _rope_tables():
    t = jnp.arange(DEC_SEQ)
    lane = np.arange(LANES)
    f = lane % 32
    first = f < 16
    inv = ROPE_THETA ** (-jnp.arange(16, dtype=F32) / 16)
    pos = jnp.where((lane % 64 < 32)[None, :], (t // GRID_W)[:, None], (t % GRID_W)[:, None]).astype(F32)
    ang = pos * inv[f % 16][None, :]
    cos, sin = jnp.cos(ang), jnp.sin(ang)
    sin_next = jnp.where(first[None, :], -sin, 0.0)
    sin_prev = jnp.where(first[None, :], 0.0, sin)
    return cos, sin_next, sin_prev


def _rope(x, cos, sin_next, sin_prev):
    return (x * cos + pltpu.roll(x, LANES - 16, axis=1) * sin_next
            + pltpu.roll(x, 16, axis=1) * sin_prev)


def _lane_lo(shape):
    return lax.broadcasted_iota(jnp.int32, shape, 1) < HALF


def _with_ones(v):
    return jnp.concatenate([v.astype(BF16), jnp.ones(v.shape, BF16)], axis=1)


def _attend(scores, vals_ones, sink=None):
    m = None
    for s in scores:
        mi = jnp.max(s, axis=-1, keepdims=True)
        m = mi if m is None else jnp.maximum(m, mi)
    if sink is not None:
        m = jnp.maximum(m, sink)
    acc = None
    for s, v in zip(scores, vals_ones):
        part = _dot(jnp.exp(s - m).astype(BF16), v)
        acc = part if acc is None else acc + part
    den = acc[:, LANES:]
    if sink is not None:
        den = den + jnp.exp(sink - m)
    return acc[:, :LANES] / den


CTX_HEADS = 8


A_SCALE = A_HALF_DIM ** -0.5


def _diff_lambda(lp, lam_init):
    a = jnp.sum(lp[0] * lp[1], axis=-1, keepdims=True)
    b = jnp.sum(lp[2] * lp[3], axis=-1, keepdims=True)
    return jnp.exp(a) - jnp.exp(b) + lam_init


def _diff_finish(o, subln_ref, lam_init):
    return (_rms(o, subln_ref[...]) * (1.0 - lam_init)).astype(BF16)


def _head_rows(head, n_tokens, n_heads):
    return pl.ds(head, n_tokens, stride=n_heads)


def _a_ctx_kernel(lam_ref, subln_ref, q_ref, k_ref, v_ref, o_ref, ko_ref, vo_ref, *, lam_init):
    lo = _lane_lo((SEQ, LANES))
    for hh in range(CTX_HEADS):
        cols = slice(hh * LANES, (hh + 1) * LANES)
        q = q_ref[:, cols] * A_SCALE
        k = k_ref[:, cols]
        v = v_ref[:, cols]
        cache_rows = _head_rows(pl.program_id(1) * CTX_HEADS + hh, SEQ, A_HEADS)
        ko_ref[cache_rows, :] = k
        vo_ref[cache_rows, :] = v
        kb, v1 = k.astype(BF16), _with_ones(v)
        o1 = _attend([_dot_nt(jnp.where(lo, q, 0.0).astype(BF16), kb)], [v1])
        o2 = _attend([_dot_nt(jnp.where(lo, 0.0, q).astype(BF16), kb)], [v1])
        lam = _diff_lambda(lam_ref[:, hh], lam_init)
        o_ref[:, cols] = _diff_finish(o1 - lam * o2, subln_ref, lam_init)


A_LAT_TQ = 256


def _a_lat_kernel(lam_ref, subln_ref, cos_ref, nxt_ref, prv_ref, q_ref, k_ref, v_ref, pk_ref, pv_ref, o_ref,
                  kb_ref, v1_ref, e_ref, *, lam_init):
    past_rows = _head_rows(pl.program_id(1), PAST_LEN, A_HEADS)
    kb_ref[:PAST_LEN, :] = pk_ref[past_rows, :].astype(BF16)
    kb_ref[PAST_LEN:, :] = _rope(k_ref[...], cos_ref[...], nxt_ref[...], prv_ref[...]).astype(BF16)
    v1_ref[:PAST_LEN, :] = _with_ones(pv_ref[past_rows, :])
    v1_ref[PAST_LEN:, :] = _with_ones(v_ref[...])
    lam = _diff_lambda(lam_ref[...], lam_init)
    lo = _lane_lo((A_LAT_TQ, LANES))
    n_blocks = DEC_SEQ // A_LAT_TQ

    def weights(i):
        rows = slice(i * A_LAT_TQ, (i + 1) * A_LAT_TQ)
        q = _rope(q_ref[rows, :], cos_ref[rows, :], nxt_ref[rows, :], prv_ref[rows, :]) * A_SCALE
        for half, keep in enumerate((lo, ~lo)):
            s = _dot_nt(jnp.where(keep, q, 0.0).astype(BF16), kb_ref[...])
            e_ref[i % 2, half] = jnp.exp(s - jnp.max(s, axis=-1, keepdims=True)).astype(BF16)

    def values(i):
        rows = slice(i * A_LAT_TQ, (i + 1) * A_LAT_TQ)
        outs = []
        for half in (0, 1):
            acc = _dot(e_ref[i % 2, half], v1_ref[...])
            outs.append(acc[:, :LANES] / acc[:, LANES:])
        o_ref[rows, :] = _diff_finish(outs[0] - lam * outs[1], subln_ref, lam_init)

    weights(0)
    for i in range(1, n_blocks):
        values(i - 1)
        weights(i)
    values(n_blocks - 1)


def _mixer_a(qkv, cache_k, cache_v, lam_p, subln, lam_init, rope):
    nh = A_HEADS
    lam4 = lam_p.reshape(4, nh, 1, A_HALF_DIM)
    sub2 = subln.reshape(1, A_V_DIM)
    cw = CTX_HEADS * LANES
    nt = nh // CTX_HEADS
    ctx_out = pl.BlockSpec((SEQ, cw), lambda b, h: (b, h))
    kv_out = pl.BlockSpec((SEQ * nh, A_V_DIM), lambda b, h: (b, 0))
    kv_shape = jax.ShapeDtypeStruct((N_PROMPT * nh, A_V_DIM), F32)
    o_p, k_new, v_new = pl.pallas_call(
        functools.partial(_a_ctx_kernel, lam_init=lam_init),
        out_shape=(jax.ShapeDtypeStruct((N_PROMPT, nh * A_V_DIM), BF16), kv_shape, kv_shape),
        grid=(BATCH, nt),
        in_specs=[pl.BlockSpec((4, CTX_HEADS, 1, A_HALF_DIM), lambda b, h: (0, h, 0, 0)),
                  pl.BlockSpec((1, A_V_DIM), lambda b, h: (0, 0)),
                  pl.BlockSpec((SEQ, cw), lambda b, h: (b, h)),
                  pl.BlockSpec((SEQ, cw), lambda b, h: (b, nt + h)),
                  pl.BlockSpec((SEQ, cw), lambda b, h: (b, 2 * nt + h))],
        out_specs=(ctx_out, kv_out, kv_out),
        compiler_params=_params(("arbitrary", "arbitrary"), 40),
        name="diff_attn_ctx",
    )(lam4, sub2, qkv, qkv, qkv)

    table = pl.BlockSpec((DEC_SEQ, LANES), lambda b, h: (0, 0))
    past = pl.BlockSpec((PAST_LEN * nh, A_V_DIM), lambda b, h: (b, 0))
    o_s = pl.pallas_call(
        functools.partial(_a_lat_kernel, lam_init=lam_init),
        out_shape=jax.ShapeDtypeStruct((N_SAMPLE, nh * A_V_DIM), BF16),
        grid=(DEC_BATCH, nh),
        in_specs=[pl.BlockSpec((4, None, 1, A_HALF_DIM), lambda b, h: (0, h, 0, 0)),
                  pl.BlockSpec((1, A_V_DIM), lambda b, h: (0, 0)),
                  table, table, table,
                  pl.BlockSpec((DEC_SEQ, LANES), lambda b, h: (SAMPLE_ROW0 + b, h)),
                  pl.BlockSpec((DEC_SEQ, LANES), lambda b, h: (SAMPLE_ROW0 + b, nh + h)),
                  pl.BlockSpec((DEC_SEQ, LANES), lambda b, h: (SAMPLE_ROW0 + b, 2 * nh + h)),
                  past, past],
        out_specs=pl.BlockSpec((DEC_SEQ, LANES), lambda b, h: (b, h)),
        scratch_shapes=[pltpu.VMEM((PAST_LEN + DEC_SEQ, LANES), BF16),
                        pltpu.VMEM((PAST_LEN + DEC_SEQ, 2 * LANES), BF16),
                        pltpu.VMEM((2, 2, A_LAT_TQ, PAST_LEN + DEC_SEQ), BF16)],
        compiler_params=_params(("arbitrary", "arbitrary"), 48),
        name="diff_attn_lat",
    )(lam4, sub2, *rope, qkv, qkv, qkv,
      cache_k.reshape(DEC_BATCH * PAST_LEN * nh, A_V_DIM), cache_v.reshape(DEC_BATCH * PAST_LEN * nh, A_V_DIM))
    return o_p, o_s, k_new, v_new


B_CHUNK = 256


def _log_sigmoid(x):
    return jnp.minimum(x, 0.0) - jnp.log(1.0 + jnp.exp(-jnp.abs(x)))


def _mlstm_kernel(*refs, heads, **static):
    def head_view(ref, hh, kind):
        if kind == "cols":
            width = ref.shape[1] // heads
            return ref.at[:, hh * width:(hh + 1) * width]
        if kind == "lead":
            return ref.at[hh]
        return ref.at[:, hh]

    kinds = (["cols"] * 4 + ["lead"] * 2 + ["cols"] + (["state"] * 3 if static["has_init"] else [])
             + ["cols"] + (["state"] * 3 if static["emit_state"] else []) + ["lead", "lead"])
    assert len(kinds) == len(refs)
    for hh in range(heads):
        _mlstm_head(*[head_view(r, hh, kd) for r, kd in zip(refs, kinds)], **static)


def _mlstm_head(*refs, n_chunks, has_init, emit_state):
    it = iter(refs)
    q_ref, k_ref, v_ref, og_ref, grow_ref, bcol_ref, nw_ref = (next(it) for _ in range(7))
    if has_init:
        c0_ref, n0_ref, m0_ref = next(it), next(it), next(it)
    h_ref = next(it)
    if emit_state:
        c_out, n_out, m_out = next(it), next(it), next(it)
    hst_ref, vt_ref = next(it), next(it)

    L = B_CHUNK
    si = lax.broadcasted_iota(jnp.int32, (L, L), 0)
    ti = lax.broadcasted_iota(jnp.int32, (L, L), 1)
    grow = grow_ref[...] + bcol_ref[...]
    grow8 = jnp.concatenate([grow, jnp.zeros_like(grow)], axis=0)
    qscale = B_DK ** -0.5
    chunks = [slice(c * L, (c + 1) * L) for c in range(n_chunks)]
    gcols = {}
    for rows in chunks:
        vt_ref[:, rows] = v_ref[rows, :].T
        gcols[rows.start] = grow8[:, rows].T

    for d in (0, 1):
        feeds = (si <= ti) if d == 0 else (si >= ti)
        before = (ti <= si) if d == 0 else (ti >= si)
        last = L - 1 if d == 0 else 0
        if has_init:
            C, n, m = c0_ref[d], n0_ref[d], m0_ref[d][:, :1]
        else:
            C, n, m = jnp.zeros((B_DV, B_DK), F32), jnp.zeros((1, B_DK), F32), jnp.zeros((1, 1), F32)
        for ci in range(n_chunks):
            rows = chunks[ci if d == 0 else n_chunks - 1 - ci]
            qb = (q_ref[rows, :] * qscale).astype(BF16)
            kb = k_ref[rows, :].astype(BF16)
            vt = vt_ref[:, rows]
            ig_row = grow[2 * d:2 * d + 1, rows]
            lf_row = _log_sigmoid(grow[2 * d + 1:2 * d + 2, rows])
            ig_col = gcols[rows.start][:, 2 * d:2 * d + 1]
            lf_col = _log_sigmoid(gcols[rows.start][:, 2 * d + 1:2 * d + 2])
            b_row = jnp.sum(jnp.where(feeds, lf_col, 0.0), axis=0, keepdims=True)
            b_col = jnp.sum(jnp.where(before, lf_row, 0.0), axis=1, keepdims=True)
            dm = jnp.where(feeds, b_row + (ig_col - b_col), NEG)
            inter = b_row + m
            m_t = jnp.maximum(inter, jnp.max(dm, axis=0, keepdims=True))
            w = jnp.exp(dm - m_t)
            a_in = jnp.exp(inter - m_t)
            sw = _dot_nt(kb, qb) * w
            num = _dot(vt.astype(BF16), sw.astype(BF16)) + a_in * _dot_nt(C.astype(BF16), qb)
            nq = _dot_nt(jnp.broadcast_to(n, (8, B_DK)).astype(BF16), qb)[:1]
            den = jnp.sum(sw, axis=0, keepdims=True) + a_in * nq
            h = num * (1.0 / jnp.maximum(jnp.abs(den), jnp.exp(-m_t)))
            if d == 0:
                hst_ref[:, rows] = h
            else:
                hst_ref[:, rows] += h
            if emit_state or ci + 1 < n_chunks:
                m_last = m_t[:, last:last + 1]
                al = a_in[:, last:last + 1]
                wl = jnp.exp(b_row[:, last:last + 1] - b_row + ig_row - m_last)
                C = al * C + _dot((vt * wl).astype(BF16), kb)
                n = al * n + _dot(jnp.broadcast_to(wl, (8, L)).astype(BF16), kb)[:1]
                m = m_last
        if emit_state:
            c_out[d] = C
            n_out[d] = n
            m_out[d] = jnp.broadcast_to(m, (1, LANES))

    for rows in chunks:
        hsum = hst_ref[:, rows].T
        gate = 1.0 / (1.0 + jnp.exp(-og_ref[rows, :]))
        h_ref[rows, :] = (_rms(hsum, nw_ref[...]) * gate).astype(BF16)


def _mlstm_call(z, grow, bias, norm_w, seq, batch, row0, init, heads):
    nh = B_HEADS
    has_init = init is not None
    emit_state = not has_init
    bcol = bias.reshape(2, 2, nh).transpose(2, 0, 1).reshape(nh, 4, 1)
    hp = heads
    qk_tiles = B_QK // (hp * B_DK)
    vd0 = 2 * B_QK // (hp * B_DV)
    nt = nh // hp
    in_specs = [
        pl.BlockSpec((seq, hp * B_DK), lambda b, h: (row0 + b, h)),
        pl.BlockSpec((seq, hp * B_DK), lambda b, h: (row0 + b, qk_tiles + h)),
        pl.BlockSpec((seq, hp * B_DV), lambda b, h: (row0 + b, vd0 + h)),
        pl.BlockSpec((seq, hp * B_DV), lambda b, h: (row0 + b, vd0 + nt + h)),
        pl.BlockSpec((hp, 4, seq), lambda b, h: (h, 0, row0 + b)),
        pl.BlockSpec((hp, 4, 1), lambda b, h: (h, 0, 0)),
        pl.BlockSpec((1, hp * B_DV), lambda b, h: (0, h)),
    ]
    args = [z, z, z, z, grow, bcol, norm_w.reshape(1, B_VD)]
    state_c = pl.BlockSpec((None, 2, hp, B_DV, B_DK), lambda b, h: (b, 0, h, 0, 0))
    state_v = pl.BlockSpec((None, 2, hp, 1, B_DK), lambda b, h: (b, 0, h, 0, 0))
    if has_init:
        c0, n0, m0 = init
        in_specs += [state_c, state_v, state_v]
        args += [c0, n0.reshape(batch, 2, nh, 1, B_DK),
                 jnp.broadcast_to(m0[..., None, None], (batch, 2, nh, 1, B_DK))]
    h_shape = jax.ShapeDtypeStruct((batch * seq, B_VD), BF16)
    h_spec = pl.BlockSpec((seq, hp * B_DV), lambda b, h: (b, h))
    if emit_state:
        out_shape = (h_shape,
                     jax.ShapeDtypeStruct((batch, 2, nh, B_DV, B_DK), F32),
                     jax.ShapeDtypeStruct((batch, 2, nh, 1, B_DK), F32),
                     jax.ShapeDtypeStruct((batch, 2, nh, 1, B_DK), F32))
        out_specs = (h_spec, state_c, state_v, state_v)
    else:
        out_shape, out_specs = h_shape, h_spec
    return pl.pallas_call(
        functools.partial(_mlstm_kernel, heads=hp, n_chunks=seq // B_CHUNK, has_init=has_init,
                          emit_state=emit_state),
        out_shape=out_shape,
        grid=(batch, nt),
        in_specs=in_specs, out_specs=out_specs,
        scratch_shapes=[pltpu.VMEM((hp, B_DV, seq), F32), pltpu.VMEM((hp, B_DV, seq), F32)],
        compiler_params=_params(("arbitrary", "arbitrary"), 48),
        name="mlstm_ctx" if emit_state else "mlstm_lat",
    )(*args)


def _gates_kernel(w_ref, h_ref, o_ref):
    o_ref[...] = _dot_nt(w_ref[...].astype(BF16), h_ref[...])


def _gate_rows(h, w_gate):
    nh, tm = B_HEADS, 1024
    w_rows = w_gate.T.reshape(4, nh, D_MODEL).transpose(1, 0, 2).reshape(4 * nh, D_MODEL)
    out = pl.pallas_call(
        _gates_kernel,
        out_shape=jax.ShapeDtypeStruct((4 * nh, N_TOK), F32),
        grid=(N_TOK // tm,),
        in_specs=[pl.BlockSpec((4 * nh, D_MODEL), lambda i: (0, 0)),
                  pl.BlockSpec((tm, D_MODEL), lambda i: (i, 0))],
        out_specs=pl.BlockSpec((4 * nh, tm), lambda i: (0, i)),
        compiler_params=_params(("arbitrary",), 32),
        name="gate_rows",
    )(w_rows, h)
    return out.reshape(nh, 4, N_TOK)


def _mixer_b(z, grow, state_c, state_n, state_m, bias, norm_w):
    nh = B_HEADS
    o_p, c_new, n_new, m_new = _mlstm_call(z, grow, bias, norm_w, SEQ, BATCH, 0, None, heads=4)
    o_s = _mlstm_call(z, grow, bias, norm_w, DEC_SEQ, DEC_BATCH, SAMPLE_ROW0,
                      (state_c, state_n, state_m), heads=1)
    return o_p, o_s, c_new, n_new.reshape(BATCH, 2, nh, B_DK), m_new[:, :, :, 0, 0]


C_SCALE = C_HEAD_DIM ** -0.5


def _c_ctx_kernel(q_ref, k_ref, v_ref, o_ref, ko_ref, vo_ref):
    for hh in range(CTX_HEADS):
        cols = slice(hh * LANES, (hh + 1) * LANES)
        k = k_ref[:, cols]
        v = v_ref[:, cols]
        cache_rows = _head_rows(pl.program_id(1) * CTX_HEADS + hh, SEQ, C_HEADS)
        ko_ref[cache_rows, :] = k
        vo_ref[cache_rows, :] = v
        s = _dot_nt(q_ref[:, cols].astype(BF16), k.astype(BF16)) * C_SCALE
        o_ref[:, cols] = _attend([s], [_with_ones(v)]).astype(BF16)


def _na_row_start(r):
    return min(max(r - NA_ROWS // 2, 0), GRID_H - NA_ROWS)


def _na_row_groups():
    groups = []
    for r in range(GRID_H):
        if groups and _na_row_start(groups[-1][0]) == _na_row_start(r):
            groups[-1].append(r)
        else:
            groups.append([r])
    return groups


C_LAT_HEADS = 2


def _c_lat_kernel(bias_ref, q_ref, k_ref, v_ref, pk_ref, pv_ref, o_ref, *scratch):
    for hh in range(C_LAT_HEADS):
        cols = slice(hh * LANES, (hh + 1) * LANES)
        _c_lat_head(pl.program_id(1) * C_LAT_HEADS + hh, bias_ref.at[hh], q_ref.at[:, cols], k_ref.at[:, cols],
                    v_ref.at[:, cols], pk_ref, pv_ref, o_ref.at[:, cols], *[s.at[hh] for s in scratch])


def _c_lat_head(head, bias_ref, q_ref, k_ref, v_ref, pk_ref, pv_ref, o_ref, kb_ref, v1_ref, sc_ref, ec_ref, acc_ref):
    n_win = NA_ROWS * GRID_W
    kb_ref[...] = k_ref[...].astype(BF16)
    v1_ref[...] = _with_ones(v_ref[...])
    past_rows = _head_rows(head, PAST_LEN, C_HEADS)
    sc_ref[...] = _dot_nt(q_ref[...].astype(BF16), pk_ref[past_rows, :].astype(BF16)) * C_SCALE
    for rows_g in _na_row_groups():
        r0 = _na_row_start(rows_g[0])
        rows = slice(rows_g[0] * GRID_W, (rows_g[-1] + 1) * GRID_W)
        win = slice(r0 * GRID_W, r0 * GRID_W + n_win)
        strips = []
        for r in rows_g:
            strip = NA_ROWS - 1 - (r - r0)
            even = strip - strip % 2
            strips.append(bias_ref[strip % 2, :, even * GRID_W:even * GRID_W + n_win])
        bias = strips[0] if len(strips) == 1 else jnp.concatenate(strips, axis=0)
        s_n = _dot_nt(q_ref[rows, :].astype(BF16), kb_ref[win, :]) * C_SCALE + bias
        s_c = sc_ref[rows, :]
        m = jnp.maximum(jnp.max(s_n, axis=-1, keepdims=True), jnp.max(s_c, axis=-1, keepdims=True))
        ec_ref[rows, :] = jnp.exp(s_c - m).astype(BF16)
        acc_ref[rows, :] = _dot(jnp.exp(s_n - m).astype(BF16), v1_ref[win, :])
    acc = acc_ref[...] + _dot(ec_ref[...], _with_ones(pv_ref[past_rows, :]))
    o_ref[...] = (acc[:, :LANES] / acc[:, LANES:]).astype(BF16)


def _na_bias(rpb):
    n_drow, n_dcol = 2 * NA_ROWS - 1, 2 * NA_COLS - 1
    cq = np.arange(GRID_W)[:, None]
    kc = np.arange(GRID_W)[None, :]
    cstart = np.clip(cq - NA_COLS // 2, 0, GRID_W - NA_COLS)
    col_ok = (kc >= cstart) & (kc < cstart + NA_COLS)
    dcol = np.clip(kc - cq, -(NA_COLS - 1), NA_COLS - 1) + NA_COLS - 1
    onehot = (dcol[:, :, None] == np.arange(n_dcol)).astype(np.float32)
    strips = jnp.einsum('hab,qkb->hqak', rpb.astype(F32), onehot, precision=lax.Precision.HIGHEST)
    strips = jnp.where(col_ok[None, :, None, :], strips, NEG).reshape(C_HEADS, GRID_W, n_drow * GRID_W)
    strips = jnp.pad(strips, ((0, 0), (0, 0), (0, 2 * GRID_W)))
    return jnp.stack([strips[:, :, :(n_drow + 1) * GRID_W], strips[:, :, GRID_W:]], axis=1)


def _mixer_c(qkv, cache_k, cache_v, rpb):
    nh = C_HEADS
    cw = CTX_HEADS * LANES
    nt = nh // CTX_HEADS
    ctx_out = pl.BlockSpec((SEQ, cw), lambda b, h: (b, h))
    kv_out = pl.BlockSpec((SEQ * nh, C_HEAD_DIM), lambda b, h: (b, 0))
    kv_shape = jax.ShapeDtypeStruct((N_PROMPT * nh, C_HEAD_DIM), F32)
    o_p, k_new, v_new = pl.pallas_call(
        _c_ctx_kernel,
        out_shape=(jax.ShapeDtypeStruct((N_PROMPT, D_MODEL), BF16), kv_shape, kv_shape),
        grid=(BATCH, nt),
        in_specs=[pl.BlockSpec((SEQ, cw), lambda b, h: (b, h)),
                  pl.BlockSpec((SEQ, cw), lambda b, h: (b, nt + h)),
                  pl.BlockSpec((SEQ, cw), lambda b, h: (b, 2 * nt + h))],
        out_specs=(ctx_out, kv_out, kv_out),
        compiler_params=_params(("arbitrary", "arbitrary"), 40),
        name="na_ctx",
    )(qkv, qkv, qkv)

    past = pl.BlockSpec((PAST_LEN * nh, C_HEAD_DIM), lambda b, h: (b, 0))
    lh = C_LAT_HEADS
    lw, nl = lh * LANES, nh // lh
    o_s = pl.pallas_call(
        _c_lat_kernel,
        out_shape=jax.ShapeDtypeStruct((N_SAMPLE, D_MODEL), BF16),
        grid=(DEC_BATCH, nl),
        in_specs=[pl.BlockSpec((lh, 2, GRID_W, 2 * NA_ROWS * GRID_W), lambda b, h: (h, 0, 0, 0)),
                  pl.BlockSpec((DEC_SEQ, lw), lambda b, h: (SAMPLE_ROW0 + b, h)),
                  pl.BlockSpec((DEC_SEQ, lw), lambda b, h: (SAMPLE_ROW0 + b, nl + h)),
                  pl.BlockSpec((DEC_SEQ, lw), lambda b, h: (SAMPLE_ROW0 + b, 2 * nl + h)),
                  past, past],
        out_specs=pl.BlockSpec((DEC_SEQ, lw), lambda b, h: (b, h)),
        scratch_shapes=[pltpu.VMEM((lh, DEC_SEQ, LANES), BF16), pltpu.VMEM((lh, DEC_SEQ, 2 * LANES), BF16),
                        pltpu.VMEM((lh, DEC_SEQ, PAST_LEN), F32), pltpu.VMEM((lh, DEC_SEQ, PAST_LEN), BF16),
                        pltpu.VMEM((lh, DEC_SEQ, 2 * LANES), F32)],
        compiler_params=_params(("arbitrary", "arbitrary"), 56),
        name="na_lat",
    )(_na_bias(rpb), qkv, qkv, qkv,
      cache_k.reshape(DEC_BATCH * PAST_LEN * nh, C_HEAD_DIM), cache_v.reshape(DEC_BATCH * PAST_LEN * nh, C_HEAD_DIM))
    return o_p, o_s, k_new, v_new


D_SCALE = D_HEAD_DIM ** -0.5
D_QBLOCK = 128
D_BAND = 3 * D_QBLOCK


D_CTX_PAIRS = 2


def _d_ctx_kernel(sink_ref, q_ref, k_ref, v_ref, o_ref, ko_ref, vo_ref):
    lo = _lane_lo((SEQ, LANES))
    for pp in range(D_CTX_PAIRS):
        pair = pl.program_id(1) * D_CTX_PAIRS + pp
        kv_cols = slice(pp * LANES, (pp + 1) * LANES)
        for half in (0, 1):
            cache_rows = _head_rows(2 * pair + half, SEQ, D_KV_HEADS)
            half_cols = slice(pp * LANES + half * HALF, pp * LANES + (half + 1) * HALF)
            ko_ref[cache_rows, :] = k_ref[:, half_cols]
            vo_ref[cache_rows, :] = v_ref[:, half_cols]
        kb, v1 = k_ref[:, kv_cols].astype(BF16), _with_ones(v_ref[:, kv_cols])
        for tile in range(4):
            kv = tile // 2
            keep = lo if kv == 0 else ~lo
            cols = slice((4 * pp + tile) * LANES, (4 * pp + tile + 1) * LANES)
            qt = q_ref[:, cols] * D_SCALE
            halves = []
            for e in (0, 1):
                qe = qt if e == kv else pltpu.roll(qt, HALF, axis=1)
                s = _dot_nt(jnp.where(keep, qe, 0.0).astype(BF16), kb)
                o = _attend([s], [v1], sink=sink_ref[pair * 2 * D_GROUP + tile * 2 + e])
                halves.append(o if e == kv else pltpu.roll(o, HALF, axis=1))
            o_ref[:, cols] = jnp.where(lo, halves[0], halves[1]).astype(BF16)


def _d_lat_kernel(sink_ref, cos_ref, nxt_ref, prv_ref, q_ref, k_ref, v_ref, pk_ref, pv_ref, o_ref,
                  kb_ref, v1_ref, pkb_ref, pv1_ref, e_ref, es_ref):
    pair = pl.program_id(1)
    kb_ref[...] = _rope(k_ref[...], cos_ref[...], nxt_ref[...], prv_ref[...]).astype(BF16)
    v1_ref[...] = _with_ones(v_ref[...])
    pkb_ref[...] = pk_ref[...].astype(BF16)
    pv1_ref[...] = _with_ones(pv_ref[...])
    tq = D_QBLOCK
    lo = _lane_lo((tq, LANES))
    n_blocks = DEC_SEQ // tq

    def band_of(bi):
        start = min(max((bi - 1) * tq, 0), DEC_SEQ - D_BAND)
        return start, slice(start, start + D_BAND)

    def weights(bi):
        rows = slice(bi * tq, (bi + 1) * tq)
        start, band = band_of(bi)
        cos, nxt, prv = cos_ref[rows, :], nxt_ref[rows, :], prv_ref[rows, :]
        qpos = bi * tq + lax.broadcasted_iota(jnp.int32, (tq, D_BAND), 0)
        kpos = start + lax.broadcasted_iota(jnp.int32, (tq, D_BAND), 1)
        mask = jnp.where(jnp.abs(qpos - kpos) <= WINDOW, 0.0, NEG)
        mask4 = jnp.concatenate([mask] * D_GROUP, axis=0)
        tiles = [_rope(q_ref[rows, t * LANES:(t + 1) * LANES], cos, nxt, prv) * D_SCALE for t in range(4)]
        for kv in (0, 1):
            keep = lo if kv == 0 else ~lo
            stack, sinks = [], []
            for t in (2 * kv, 2 * kv + 1):
                for e in (0, 1):
                    qe = tiles[t] if e == kv else pltpu.roll(tiles[t], HALF, axis=1)
                    stack.append(jnp.where(keep, qe, 0.0))
                    sinks.append(jnp.full((tq, 1), sink_ref[pair * 2 * D_GROUP + t * 2 + e], F32))
            qs = jnp.concatenate(stack, axis=0).astype(BF16)
            sink = jnp.concatenate(sinks, axis=0)
            s_band = _dot_nt(qs, kb_ref[band, :]) + mask4
            s_past = _dot_nt(qs, pkb_ref[...])
            m = jnp.maximum(jnp.maximum(jnp.max(s_band, axis=-1, keepdims=True),
                                        jnp.max(s_past, axis=-1, keepdims=True)), sink)
            e_ref[bi % 2, kv, :, :D_BAND] = jnp.exp(s_band - m).astype(BF16)
            e_ref[bi % 2, kv, :, D_BAND:] = jnp.exp(s_past - m).astype(BF16)
            es_ref[bi % 2, kv] = jnp.exp(sink - m)

    def values(bi):
        rows = slice(bi * tq, (bi + 1) * tq)
        _, band = band_of(bi)
        outs = {}
        for kv in (0, 1):
            acc = (_dot(e_ref[bi % 2, kv, :, :D_BAND], v1_ref[band, :])
                   + _dot(e_ref[bi % 2, kv, :, D_BAND:], pv1_ref[...]))
            o = acc[:, :LANES] / (acc[:, LANES:] + es_ref[bi % 2, kv])
            for i, t in enumerate((2 * kv, 2 * kv + 1)):
                for e in (0, 1):
                    piece = o[(2 * i + e) * tq:(2 * i + e + 1) * tq, :]
                    outs[(t, e)] = piece if e == kv else pltpu.roll(piece, HALF, axis=1)
        for t in range(4):
            o_ref[rows, t * LANES:(t + 1) * LANES] = jnp.where(lo, outs[(t, 0)], outs[(t, 1)]).astype(BF16)

    weights(0)
    for bi in range(1, n_blocks):
        values(bi - 1)
        weights(bi)
    values(n_blocks - 1)


def _mixer_d(qkv, cache_k, cache_v, sink, rope):
    n_pairs = D_KV_HEADS // 2
    qw = 2 * D_GROUP * D_HEAD_DIM
    k0 = D_HEADS * D_HEAD_DIM // LANES
    v0 = k0 + n_pairs
    kvw = D_KV_HEADS * D_HEAD_DIM
    smem = pl.BlockSpec(memory_space=pltpu.SMEM)
    sink = sink.astype(F32)
    cp = D_CTX_PAIRS
    kv_out = pl.BlockSpec((SEQ * D_KV_HEADS, D_HEAD_DIM), lambda b, p: (b, 0))
    kv_shape = jax.ShapeDtypeStruct((N_PROMPT * D_KV_HEADS, D_HEAD_DIM), F32)
    o_p, k_new, v_new = pl.pallas_call(
        _d_ctx_kernel,
        out_shape=(jax.ShapeDtypeStruct((N_PROMPT, D_MODEL), BF16), kv_shape, kv_shape),
        grid=(BATCH, n_pairs // cp),
        in_specs=[smem,
                  pl.BlockSpec((SEQ, cp * qw), lambda b, p: (b, p)),
                  pl.BlockSpec((SEQ, cp * LANES), lambda b, p: (b, k0 // cp + p)),
                  pl.BlockSpec((SEQ, cp * LANES), lambda b, p: (b, v0 // cp + p))],
        out_specs=(pl.BlockSpec((SEQ, cp * qw), lambda b, p: (b, p)), kv_out, kv_out),
        compiler_params=_params(("arbitrary", "arbitrary"), 40),
        name="gqa_ctx",
    )(sink, qkv, qkv, qkv)

    table = pl.BlockSpec((DEC_SEQ, LANES), lambda b, p: (0, 0))
    past = pl.BlockSpec((PAST_LEN, LANES), lambda b, p: (b, p))
    o_s = pl.pallas_call(
        _d_lat_kernel,
        out_shape=jax.ShapeDtypeStruct((N_SAMPLE, D_MODEL), BF16),
        grid=(DEC_BATCH, n_pairs),
        in_specs=[smem, table, table, table,
                  pl.BlockSpec((DEC_SEQ, qw), lambda b, p: (SAMPLE_ROW0 + b, p)),
                  pl.BlockSpec((DEC_SEQ, LANES), lambda b, p: (SAMPLE_ROW0 + b, k0 + p)),
                  pl.BlockSpec((DEC_SEQ, LANES), lambda b, p: (SAMPLE_ROW0 + b, v0 + p)),
                  past, past],
        out_specs=pl.BlockSpec((DEC_SEQ, qw), lambda b, p: (b, p)),
        scratch_shapes=[pltpu.VMEM((DEC_SEQ, LANES), BF16), pltpu.VMEM((DEC_SEQ, 2 * LANES), BF16),
                        pltpu.VMEM((PAST_LEN, LANES), BF16), pltpu.VMEM((PAST_LEN, 2 * LANES), BF16),
                        pltpu.VMEM((2, 2, D_GROUP * D_QBLOCK, D_BAND + PAST_LEN), BF16),
                        pltpu.VMEM((2, 2, D_GROUP * D_QBLOCK, 1), F32)],
        compiler_params=_params(("arbitrary", "arbitrary"), 40),
        name="gqa_lat",
    )(sink, *rope, qkv, qkv, qkv,
      cache_k.reshape(DEC_BATCH * PAST_LEN, kvw), cache_v.reshape(DEC_BATCH * PAST_LEN, kvw))
    return o_p, o_s, k_new, v_new


def kernel(x_prompt, x_sample, cache_a_k, cache_a_v, state_b_C, state_b_n, state_b_m, cache_c_k, cache_c_v,
           cache_d_k, cache_d_v, c, c_ctx, w_mod, b_mod, g_norm, w_ff1, w_ff2, a_w_in, a_w_out, a_lambda,
           a_subln, b_w_in, b_gate_bias, b_w_out, b_norm, c_w_in, c_w_out, c_rpb, d_w_in, d_w_out, d_sink):
    cond = jnp.concatenate([c_ctx[None, :], c, jnp.zeros((N_COND - 1 - DEC_BATCH, D_MODEL), F32)], axis=0)
    mod = _modulation(cond, w_mod, b_mod).reshape(DEPTH, N_COND, N_MOD, 1, D_MODEL)
    gains = g_norm.reshape(DEPTH, 4, 1, D_MODEL)
    rope = _rope_tables()
    new = {name: [] for name in ("a_k", "a_v", "b_C", "b_n", "b_m", "c_k", "c_v", "d_k", "d_v")}

    x, h = _adaln(x_prompt, x_sample, gains, mod, 0)
    for i in range(DEPTH):
        kind, j = i % N_MIXERS, i // N_MIXERS
        if kind == 0:
            lam_init = 0.8 - 0.6 * math.exp(-0.3 * i)
            qkv = _project(h, a_w_in[j])
            o_p, o_s, k_new, v_new = _mixer_a(qkv, cache_a_k[:, j], cache_a_v[:, j], a_lambda[j], a_subln[j],
                                              lam_init, rope)
            w_out = a_w_out[j]
            new["a_k"].append(k_new.reshape(BATCH, SEQ, A_HEADS, A_V_DIM))
            new["a_v"].append(v_new.reshape(BATCH, SEQ, A_HEADS, A_V_DIM))
        elif kind == 1:
            w_in = b_w_in[j]
            z = _project(h, w_in, n=B_MAIN)
            grow = _gate_rows(h, w_in[:, B_MAIN:])
            o_p, o_s, c_new, n_new, m_new = _mixer_b(z, grow, state_b_C[:, j], state_b_n[:, j], state_b_m[:, j],
                                                     b_gate_bias[j], b_norm[j])
            w_out = b_w_out[j]
            new["b_C"].append(c_new)
            new["b_n"].append(n_new)
            new["b_m"].append(m_new)
        elif kind == 2:
            qkv = _project(h, c_w_in[j])
            o_p, o_s, k_new, v_new = _mixer_c(qkv, cache_c_k[:, j], cache_c_v[:, j], c_rpb[j])
            w_out = c_w_out[j]
            new["c_k"].append(k_new.reshape(BATCH, SEQ, C_HEADS, C_HEAD_DIM))
            new["c_v"].append(v_new.reshape(BATCH, SEQ, C_HEADS, C_HEAD_DIM))
        else:
            qkv = _project(h, d_w_in[j])
            o_p, o_s, k_new, v_new = _mixer_d(qkv, cache_d_k[:, j], cache_d_v[:, j], d_sink[j], rope)
            w_out = d_w_out[j]
            new["d_k"].append(k_new.reshape(BATCH, SEQ, D_KV_HEADS, D_HEAD_DIM))
            new["d_v"].append(v_new.reshape(BATCH, SEQ, D_KV_HEADS, D_HEAD_DIM))
        x, h = _out_project(o_p, o_s, w_out, x, gains, mod, i)
        y = _mlp(h, w_ff1, w_ff2, i)
        x, h = _residual(x, y, gains, mod, i, 5, 3, (i + 1, 0, 0, 1) if i + 1 < DEPTH else None)

    x_p, x_s = x
    stack = lambda name: jnp.stack(new[name], axis=1)
    return (x_p.reshape(BATCH, SEQ, D_MODEL), x_s.reshape(DEC_BATCH, DEC_SEQ, D_MODEL),
            stack("a_k"), stack("a_v"), stack("b_C"), stack("b_n"), stack("b_m"),
            stack("c_k"), stack("c_v"), stack("d_k"), stack("d_v"))
```

```python
import functools
import math

import numpy as np
import jax
import jax.numpy as jnp
from jax import lax
from jax.experimental import pallas as pl
from jax.experimental.pallas import tpu as pltpu

D_MODEL = 2048
BATCH = 16
SEQ = 256
DEPTH = 4
DEC_BATCH = 4
DEC_SEQ = 1024
PAST_LEN = 512
GRID_W = 64
N_MIXERS = 4
D_FF = 4 * D_MODEL
N_MOD = 6
RMS_EPS = 1e-6
ROPE_THETA = 10000.0

A_HEADS = 16
A_HALF_DIM = 64
A_V_DIM = 128
B_HEADS = 8
B_DK = 128
B_DV = 256
B_QK = B_HEADS * B_DK
B_VD = B_HEADS * B_DV
B_MAIN = 2 * B_QK + 2 * B_VD
C_HEADS = 16
C_HEAD_DIM = 128
NA_ROWS = 8
NA_COLS = 16
GRID_H = DEC_SEQ // GRID_W
D_HEADS = 32
D_KV_HEADS = 8
D_GROUP = 4
D_HEAD_DIM = 64
WINDOW = 128

N_PROMPT = BATCH * SEQ
N_SAMPLE = DEC_BATCH * DEC_SEQ
N_TOK = N_PROMPT + N_SAMPLE
N_COND = 8
SAMPLE_ROW0 = N_PROMPT // DEC_SEQ

LANES = 128
HALF = LANES // 2
NEG = -1e30
MIB = 1024 * 1024
F32 = jnp.float32
BF16 = jnp.bfloat16


def _params(sem, vmem_mib):
    return pltpu.CompilerParams(dimension_semantics=sem, vmem_limit_bytes=vmem_mib * MIB)


def _dot(a, b):
    return jnp.dot(a, b, preferred_element_type=F32)


def _dot_nt(a, b):
    return lax.dot_general(a, b, (((1,), (1,)), ((), ())), preferred_element_type=F32)


def _dot_tn(a, b):
    return lax.dot_general(a, b, (((0,), (0,)), ((), ())), preferred_element_type=F32)


def _rms(x, g):
    return x * lax.rsqrt(jnp.mean(x * x, axis=-1, keepdims=True) + RMS_EPS) * g


def _cond_index(row0):
    return jnp.where(row0 < N_PROMPT, 0, 1 + (row0 - N_PROMPT) // DEC_SEQ)


def _mod_kernel(c_ref, w_ref, b_ref, o_ref):
    c = c_ref[...]
    s = (c / (1.0 + jnp.exp(-c))).astype(BF16)
    o_ref[...] = _dot(s, w_ref[...].astype(BF16)) + b_ref[...]


def _modulation(cond, w_mod, b_mod):
    tn = 1024
    n = N_MOD * D_MODEL
    return pl.pallas_call(
        _mod_kernel,
        out_shape=jax.ShapeDtypeStruct((DEPTH, N_COND, n), F32),
        grid=(DEPTH, n // tn),
        in_specs=[
            pl.BlockSpec((N_COND, D_MODEL), lambda l, j: (0, 0)),
            pl.BlockSpec((None, D_MODEL, tn), lambda l, j: (l, 0, j)),
            pl.BlockSpec((None, 1, tn), lambda l, j: (l, 0, j)),
        ],
        out_specs=pl.BlockSpec((None, N_COND, tn), lambda l, j: (l, 0, j)),
        compiler_params=_params(("arbitrary", "arbitrary"), 40),
        name="modulation",
    )(cond, w_mod, b_mod.reshape(DEPTH, 1, n))


ROW_TILE = 512


def _mod_spec(layer, which, tile0=0):
    return pl.BlockSpec((None, None, None, 1, D_MODEL),
                        lambda i: (layer, _cond_index((i + tile0) * ROW_TILE), which, 0, 0))


def _gain_spec(layer, which):
    return pl.BlockSpec((None, None, 1, D_MODEL), lambda i: (layer, which, 0, 0))


def _adaln_kernel(xp_ref, xs_ref, g_ref, shift_ref, scale_ref, x_ref, h_ref):
    def emit(x):
        x_ref[...] = x
        h_ref[...] = (_rms(x, g_ref[...]) * (1.0 + scale_ref[...]) + shift_ref[...]).astype(BF16)

    @pl.when(pl.program_id(0) < N_PROMPT // ROW_TILE)
    def _():
        emit(xp_ref[...])

    @pl.when(pl.program_id(0) >= N_PROMPT // ROW_TILE)
    def _():
        emit(xs_ref[...])


def _adaln(x_prompt, x_sample, gains, mod, layer):
    p_tiles = N_PROMPT // ROW_TILE
    row = pl.BlockSpec((ROW_TILE, D_MODEL), lambda i: (i, 0))
    return pl.pallas_call(
        _adaln_kernel,
        out_shape=(jax.ShapeDtypeStruct((N_TOK, D_MODEL), F32), jax.ShapeDtypeStruct((N_TOK, D_MODEL), BF16)),
        grid=(N_TOK // ROW_TILE,),
        in_specs=[pl.BlockSpec((ROW_TILE, D_MODEL), lambda i: (jnp.minimum(i, p_tiles - 1), 0)),
                  pl.BlockSpec((ROW_TILE, D_MODEL), lambda i: (jnp.maximum(i - p_tiles, 0), 0)),
                  _gain_spec(layer, 0), _mod_spec(layer, 0), _mod_spec(layer, 1)],
        out_specs=(row, row),
        compiler_params=_params(("arbitrary",), 40),
        name="adaln",
    )(x_prompt.reshape(N_PROMPT, D_MODEL), x_sample.reshape(N_SAMPLE, D_MODEL), gains, mod, mod)


def _residual_adaln_kernel(x_ref, y_ref, gate_ref, gpost_ref, gpre_ref, shift_ref, scale_ref, xo_ref, h_ref):
    x = x_ref[...] + gate_ref[...] * _rms(y_ref[...], gpost_ref[...])
    xo_ref[...] = x
    h = _rms(x, gpre_ref[...]) * (1.0 + scale_ref[...]) + shift_ref[...]
    h_ref[...] = h.astype(BF16)


def _residual_kernel(x_ref, y_ref, gate_ref, gpost_ref, xo_ref):
    xo_ref[...] = x_ref[...] + gate_ref[...] * _rms(y_ref[...], gpost_ref[...])


def _residual(x, y, gains, mod, layer, gate_idx, post_idx, nxt):
    row = pl.BlockSpec((ROW_TILE, D_MODEL), lambda i: (i, 0))
    if nxt is None:
        outs = []
        for tile0, n_rows in ((0, N_PROMPT), (N_PROMPT // ROW_TILE, N_SAMPLE)):
            src = pl.BlockSpec((ROW_TILE, D_MODEL), lambda i, t=tile0: (i + t, 0))
            outs.append(pl.pallas_call(
                _residual_kernel, out_shape=jax.ShapeDtypeStruct((n_rows, D_MODEL), F32),
                grid=(n_rows // ROW_TILE,),
                in_specs=[src, src, _mod_spec(layer, gate_idx, tile0), _gain_spec(layer, post_idx)],
                out_specs=row,
                compiler_params=_params(("arbitrary",), 32), name="residual",
            )(x, y, mod, gains))
        return outs, None
    nl, ng, nshift, nscale = nxt
    return pl.pallas_call(
        _residual_adaln_kernel,
        out_shape=(jax.ShapeDtypeStruct((N_TOK, D_MODEL), F32), jax.ShapeDtypeStruct((N_TOK, D_MODEL), BF16)),
        grid=(N_TOK // ROW_TILE,),
        in_specs=[row, row, _mod_spec(layer, gate_idx), _gain_spec(layer, post_idx),
                  _gain_spec(nl, ng), _mod_spec(nl, nshift), _mod_spec(nl, nscale)],
        out_specs=(row, row),
        compiler_params=_params(("arbitrary",), 40), name="residual_adaln",
    )(x, y, mod, gains, gains, mod, mod)


PROJ_TM = 1024


def _proj_kernel(a_ref, w_ref, o_ref, wbf_ref):
    @pl.when(pl.program_id(1) == 0)
    def _():
        wbf_ref[...] = w_ref[...].astype(BF16)

    o_ref[...] = _dot(a_ref[...], wbf_ref[...]).astype(o_ref.dtype)


def _project(a, w, n=None, out_dtype=F32):
    k = w.shape[0]
    n = w.shape[1] if n is None else n
    tm = PROJ_TM
    tn = min(n, 1024)
    return pl.pallas_call(
        _proj_kernel,
        out_shape=jax.ShapeDtypeStruct((N_TOK, n), out_dtype),
        grid=(n // tn, N_TOK // tm),
        in_specs=[pl.BlockSpec((tm, k), lambda j, i: (i, 0)),
                  pl.BlockSpec((k, tn), lambda j, i: (0, j))],
        out_specs=pl.BlockSpec((tm, tn), lambda j, i: (i, j)),
        scratch_shapes=[pltpu.VMEM((k, tn), BF16)],
        compiler_params=_params(("arbitrary", "arbitrary"), 48),
        name="project",
    )(a, w)


OUT_TM = 512


def _out_proj_kernel(ap_ref, as_ref, w_ref, x_ref, gate_ref, gpost_ref, gpre_ref, shift_ref, scale_ref,
                     xo_ref, h_ref):
    def finish(a):
        x = x_ref[...] + gate_ref[...] * _rms(_dot(a, w_ref[...]), gpost_ref[...])
        xo_ref[...] = x
        h_ref[...] = (_rms(x, gpre_ref[...]) * (1.0 + scale_ref[...]) + shift_ref[...]).astype(BF16)

    @pl.when(pl.program_id(0) < N_PROMPT // OUT_TM)
    def _():
        finish(ap_ref[...])

    @pl.when(pl.program_id(0) >= N_PROMPT // OUT_TM)
    def _():
        finish(as_ref[...])


def _out_project(o_p, o_s, w_out, x, gains, mod, layer):
    tm = OUT_TM
    p_tiles = N_PROMPT // tm
    row = pl.BlockSpec((tm, D_MODEL), lambda i: (i, 0))

    def mod_spec(which):
        return pl.BlockSpec((None, None, None, 1, D_MODEL), lambda i: (layer, _cond_index(i * tm), which, 0, 0))

    return pl.pallas_call(
        _out_proj_kernel,
        out_shape=(jax.ShapeDtypeStruct((N_TOK, D_MODEL), F32), jax.ShapeDtypeStruct((N_TOK, D_MODEL), BF16)),
        grid=(N_TOK // tm,),
        in_specs=[pl.BlockSpec((tm, D_MODEL), lambda i: (jnp.minimum(i, p_tiles - 1), 0)),
                  pl.BlockSpec((tm, D_MODEL), lambda i: (jnp.maximum(i - p_tiles, 0), 0)),
                  pl.BlockSpec((D_MODEL, D_MODEL), lambda i: (0, 0)),
                  row, mod_spec(2), _gain_spec(layer, 1), _gain_spec(layer, 2), mod_spec(3), mod_spec(4)],
        out_specs=(row, row),
        compiler_params=_params(("arbitrary",), 56),
        name="out_project",
    )(o_p, o_s, w_out.astype(BF16), x, mod, gains, gains, mod, mod)


def _mlp_kernel(h_ref, w1_ref, w2_ref, o_ref):
    @pl.when(pl.program_id(1) == 0)
    def _():
        o_ref[...] = jnp.zeros_like(o_ref)

    u = jnp.maximum(_dot(h_ref[...], w1_ref[...].astype(BF16)), 0.0)
    o_ref[...] += _dot((u * u).astype(BF16), w2_ref[...].astype(BF16))


def _mlp(h, w1, w2, layer):
    tm, tf = 1024, 512
    return pl.pallas_call(
        _mlp_kernel,
        out_shape=jax.ShapeDtypeStruct((N_TOK, D_MODEL), F32),
        grid=(N_TOK // tm, D_FF // tf),
        in_specs=[pl.BlockSpec((tm, D_MODEL), lambda i, f: (i, 0)),
                  pl.BlockSpec((None, D_MODEL, tf), lambda i, f: (layer, 0, f)),
                  pl.BlockSpec((None, tf, D_MODEL), lambda i, f: (layer, f, 0))],
        out_specs=pl.BlockSpec((tm, D_MODEL), lambda i, f: (i, 0)),
        compiler_params=_params(("arbitrary", "arbitrary"), 56),
        name="mlp",
    )(h, w1, w2)


def _rope_tables():
    t = jnp.arange(DEC_SEQ)
    lane = np.arange(LANES)
    f = lane % 32
    first = f < 16
    inv = ROPE_THETA ** (-jnp.arange(16, dtype=F32) / 16)
    pos = jnp.where((lane % 64 < 32)[None, :], (t // GRID_W)[:, None], (t % GRID_W)[:, None]).astype(F32)
    ang = pos * inv[f % 16][None, :]
    cos, sin = jnp.cos(ang), jnp.sin(ang)
    sin_next = jnp.where(first[None, :], -sin, 0.0)
    sin_prev = jnp.where(first[None, :], 0.0, sin)
    return cos, sin_next, sin_prev


def _rope(x, cos, sin_next, sin_prev):
    return (x * cos + pltpu.roll(x, LANES - 16, axis=1) * sin_next
            + pltpu.roll(x, 16, axis=1) * sin_prev)


def _lane_lo(shape):
    return lax.broadcasted_iota(jnp.int32, shape, 1) < HALF


def _with_ones(v):
    return jnp.concatenate([v.astype(BF16), jnp.ones(v.shape, BF16)], axis=1)


def _attend(scores, vals_ones, sink=None):
    m = None
    for s in scores:
        mi = jnp.max(s, axis=-1, keepdims=True)
        m = mi if m is None else jnp.maximum(m, mi)
    if sink is not None:
        m = jnp.maximum(m, sink)
    acc = None
    for s, v in zip(scores, vals_ones):
        part = _dot(jnp.exp(s - m).astype(BF16), v)
        acc = part if acc is None else acc + part
    den = acc[:, LANES:]
    if sink is not None:
        den = den + jnp.exp(sink - m)
    return acc[:, :LANES] / den


CTX_HEADS = 8


A_SCALE = A_HALF_DIM ** -0.5


def _diff_lambda(lp, lam_init):
    a = jnp.sum(lp[0] * lp[1], axis=-1, keepdims=True)
    b = jnp.sum(lp[2] * lp[3], axis=-1, keepdims=True)
    return jnp.exp(a) - jnp.exp(b) + lam_init


def _diff_finish(o, subln_ref, lam_init):
    return (_rms(o, subln_ref[...]) * (1.0 - lam_init)).astype(BF16)


def _head_rows(head, n_tokens, n_heads):
    return pl.ds(head, n_tokens, stride=n_heads)


def _a_ctx_kernel(lam_ref, subln_ref, q_ref, k_ref, v_ref, o_ref, ko_ref, vo_ref, *, lam_init):
    lo = _lane_lo((SEQ, LANES))
    for hh in range(CTX_HEADS):
        cols = slice(hh * LANES, (hh + 1) * LANES)
        q = q_ref[:, cols] * A_SCALE
        k = k_ref[:, cols]
        v = v_ref[:, cols]
        cache_rows = _head_rows(pl.program_id(1) * CTX_HEADS + hh, SEQ, A_HEADS)
        ko_ref[cache_rows, :] = k
        vo_ref[cache_rows, :] = v
        kb, v1 = k.astype(BF16), _with_ones(v)
        o1 = _attend([_dot_nt(jnp.where(lo, q, 0.0).astype(BF16), kb)], [v1])
        o2 = _attend([_dot_nt(jnp.where(lo, 0.0, q).astype(BF16), kb)], [v1])
        lam = _diff_lambda(lam_ref[:, hh], lam_init)
        o_ref[:, cols] = _diff_finish(o1 - lam * o2, subln_ref, lam_init)


A_LAT_TQ = 256


A_LAT_HEADS = 2


def _a_lat_kernel(lam_ref, subln_ref, cos_ref, nxt_ref, prv_ref, q_ref, k_ref, v_ref, pk_ref, pv_ref, o_ref,
                  kb_ref, v1_ref, e_ref, *, lam_init):
    stages = []
    for hh in range(A_LAT_HEADS):
        cols = slice(hh * LANES, (hh + 1) * LANES)
        stages.append(_a_lat_head(pl.program_id(1) * A_LAT_HEADS + hh, lam_ref[:, hh], subln_ref, cos_ref, nxt_ref,
                                  prv_ref, q_ref.at[:, cols], k_ref.at[:, cols], v_ref.at[:, cols], pk_ref, pv_ref,
                                  o_ref.at[:, cols], kb_ref.at[hh], v1_ref.at[hh], e_ref.at[hh], lam_init))
    n_blocks = DEC_SEQ // A_LAT_TQ
    for weights, _ in stages:
        weights(0)
    for i in range(1, n_blocks):
        for _, values in stages:
            values(i - 1)
        for weights, _ in stages:
            weights(i)
    for _, values in stages:
        values(n_blocks - 1)


def _a_lat_head(head, lam_p, subln_ref, cos_ref, nxt_ref, prv_ref, q_ref, k_ref, v_ref, pk_ref, pv_ref, o_ref,
                kb_ref, v1_ref, e_ref, lam_init):
    past_rows = _head_rows(head, PAST_LEN, A_HEADS)
    kb_ref[:PAST_LEN, :] = pk_ref[past_rows, :].astype(BF16)
    kb_ref[PAST_LEN:, :] = _rope(k_ref[...], cos_ref[...], nxt_ref[...], prv_ref[...]).astype(BF16)
    v1_ref[:PAST_LEN, :] = _with_ones(pv_ref[past_rows, :])
    v1_ref[PAST_LEN:, :] = _with_ones(v_ref[...])
    lam = _diff_lambda(lam_p, lam_init)
    lo = _lane_lo((A_LAT_TQ, LANES))

    def weights(i):
        rows = slice(i * A_LAT_TQ, (i + 1) * A_LAT_TQ)
        q = _rope(q_ref[rows, :], cos_ref[rows, :], nxt_ref[rows, :], prv_ref[rows, :]) * A_SCALE
        for half, keep in enumerate((lo, ~lo)):
            s = _dot_nt(jnp.where(keep, q, 0.0).astype(BF16), kb_ref[...])
            e_ref[i % 2, half] = jnp.exp(s - jnp.max(s, axis=-1, keepdims=True)).astype(BF16)

    def values(i):
        rows = slice(i * A_LAT_TQ, (i + 1) * A_LAT_TQ)
        outs = []
        for half in (0, 1):
            acc = _dot(e_ref[i % 2, half], v1_ref[...])
            outs.append(acc[:, :LANES] / acc[:, LANES:])
        o_ref[rows, :] = _diff_finish(outs[0] - lam * outs[1], subln_ref, lam_init)

    return weights, values


def _mixer_a(qkv, cache_k, cache_v, lam_p, subln, lam_init, rope):
    nh = A_HEADS
    lam4 = lam_p.reshape(4, nh, 1, A_HALF_DIM)
    sub2 = subln.reshape(1, A_V_DIM)
    cw = CTX_HEADS * LANES
    nt = nh // CTX_HEADS
    ctx_out = pl.BlockSpec((SEQ, cw), lambda b, h: (b, h))
    kv_out = pl.BlockSpec((SEQ * nh, A_V_DIM), lambda b, h: (b, 0))
    kv_shape = jax.ShapeDtypeStruct((N_PROMPT * nh, A_V_DIM), F32)
    o_p, k_new, v_new = pl.pallas_call(
        functools.partial(_a_ctx_kernel, lam_init=lam_init),
        out_shape=(jax.ShapeDtypeStruct((N_PROMPT, nh * A_V_DIM), BF16), kv_shape, kv_shape),
        grid=(BATCH, nt),
        in_specs=[pl.BlockSpec((4, CTX_HEADS, 1, A_HALF_DIM), lambda b, h: (0, h, 0, 0)),
                  pl.BlockSpec((1, A_V_DIM), lambda b, h: (0, 0)),
                  pl.BlockSpec((SEQ, cw), lambda b, h: (b, h)),
                  pl.BlockSpec((SEQ, cw), lambda b, h: (b, nt + h)),
                  pl.BlockSpec((SEQ, cw), lambda b, h: (b, 2 * nt + h))],
        out_specs=(ctx_out, kv_out, kv_out),
        compiler_params=_params(("arbitrary", "arbitrary"), 40),
        name="diff_attn_ctx",
    )(lam4, sub2, qkv, qkv, qkv)

    table = pl.BlockSpec((DEC_SEQ, LANES), lambda b, h: (0, 0))
    past = pl.BlockSpec((PAST_LEN * nh, A_V_DIM), lambda b, h: (b, 0))
    lh = A_LAT_HEADS
    lw, nl = lh * LANES, nh // lh
    o_s = pl.pallas_call(
        functools.partial(_a_lat_kernel, lam_init=lam_init),
        out_shape=jax.ShapeDtypeStruct((N_SAMPLE, nh * A_V_DIM), BF16),
        grid=(DEC_BATCH, nl),
        in_specs=[pl.BlockSpec((4, lh, 1, A_HALF_DIM), lambda b, h: (0, h, 0, 0)),
                  pl.BlockSpec((1, A_V_DIM), lambda b, h: (0, 0)),
                  table, table, table,
                  pl.BlockSpec((DEC_SEQ, lw), lambda b, h: (SAMPLE_ROW0 + b, h)),
                  pl.BlockSpec((DEC_SEQ, lw), lambda b, h: (SAMPLE_ROW0 + b, nl + h)),
                  pl.BlockSpec((DEC_SEQ, lw), lambda b, h: (SAMPLE_ROW0 + b, 2 * nl + h)),
                  past, past],
        out_specs=pl.BlockSpec((DEC_SEQ, lw), lambda b, h: (b, h)),
        scratch_shapes=[pltpu.VMEM((lh, PAST_LEN + DEC_SEQ, LANES), BF16),
                        pltpu.VMEM((lh, PAST_LEN + DEC_SEQ, 2 * LANES), BF16),
                        pltpu.VMEM((lh, 2, 2, A_LAT_TQ, PAST_LEN + DEC_SEQ), BF16)],
        compiler_params=_params(("arbitrary", "arbitrary"), 56),
        name="diff_attn_lat",
    )(lam4, sub2, *rope, qkv, qkv, qkv,
      cache_k.reshape(DEC_BATCH * PAST_LEN * nh, A_V_DIM), cache_v.reshape(DEC_BATCH * PAST_LEN * nh, A_V_DIM))
    return o_p, o_s, k_new, v_new


B_CHUNK = 256


def _log_sigmoid(x):
    return jnp.minimum(x, 0.0) - jnp.log(1.0 + jnp.exp(-jnp.abs(x)))


def _mlstm_kernel(*refs, heads, **static):
    def head_view(ref, hh, kind):
        if kind == "cols":
            width = ref.shape[1] // heads
            return ref.at[:, hh * width:(hh + 1) * width]
        if kind == "lead":
            return ref.at[hh]
        return ref.at[:, hh]

    kinds = (["cols"] * 4 + ["lead"] * 2 + ["cols"] + (["state"] * 3 if static["has_init"] else [])
             + ["cols"] + (["state"] * 3 if static["emit_state"] else []) + ["lead", "lead"])
    assert len(kinds) == len(refs)
    for hh in range(heads):
        _mlstm_head(*[head_view(r, hh, kd) for r, kd in zip(refs, kinds)], **static)


def _mlstm_head(*refs, n_chunks, has_init, emit_state):
    it = iter(refs)
    q_ref, k_ref, v_ref, og_ref, grow_ref, bcol_ref, nw_ref = (next(it) for _ in range(7))
    if has_init:
        c0_ref, n0_ref, m0_ref = next(it), next(it), next(it)
    h_ref = next(it)
    if emit_state:
        c_out, n_out, m_out = next(it), next(it), next(it)
    hst_ref, vt_ref = next(it), next(it)

    L = B_CHUNK
    si = lax.broadcasted_iota(jnp.int32, (L, L), 0)
    ti = lax.broadcasted_iota(jnp.int32, (L, L), 1)
    grow = grow_ref[...] + bcol_ref[...]
    grow8 = jnp.concatenate([grow, jnp.zeros_like(grow)], axis=0)
    qscale = B_DK ** -0.5
    chunks = [slice(c * L, (c + 1) * L) for c in range(n_chunks)]
    gcols = {}
    for rows in chunks:
        vt_ref[:, rows] = v_ref[rows, :].T
        gcols[rows.start] = grow8[:, rows].T

    for d in (0, 1):
        feeds = (si <= ti) if d == 0 else (si >= ti)
        before = (ti <= si) if d == 0 else (ti >= si)
        last = L - 1 if d == 0 else 0
        if has_init:
            C, n, m = c0_ref[d], n0_ref[d], m0_ref[d][:, :1]
        else:
            C, n, m = jnp.zeros((B_DV, B_DK), F32), jnp.zeros((1, B_DK), F32), jnp.zeros((1, 1), F32)
        for ci in range(n_chunks):
            rows = chunks[ci if d == 0 else n_chunks - 1 - ci]
            qb = (q_ref[rows, :] * qscale).astype(BF16)
            kb = k_ref[rows, :].astype(BF16)
            vt = vt_ref[:, rows]
            ig_row = grow[2 * d:2 * d + 1, rows]
            lf_row = _log_sigmoid(grow[2 * d + 1:2 * d + 2, rows])
            ig_col = gcols[rows.start][:, 2 * d:2 * d + 1]
            lf_col = _log_sigmoid(gcols[rows.start][:, 2 * d + 1:2 * d + 2])
            b_row = jnp.sum(jnp.where(feeds, lf_col, 0.0), axis=0, keepdims=True)
            b_col = jnp.sum(jnp.where(before, lf_row, 0.0), axis=1, keepdims=True)
            dm = jnp.where(feeds, b_row + (ig_col - b_col), NEG)
            inter = b_row + m
            m_t = jnp.maximum(inter, jnp.max(dm, axis=0, keepdims=True))
            w = jnp.exp(dm - m_t)
            a_in = jnp.exp(inter - m_t)
            sw = _dot_nt(kb, qb) * w
            num = _dot(vt.astype(BF16), sw.astype(BF16)) + a_in * _dot_nt(C.astype(BF16), qb)
            nq = _dot_nt(jnp.broadcast_to(n, (8, B_DK)).astype(BF16), qb)[:1]
            den = jnp.sum(sw, axis=0, keepdims=True) + a_in * nq
            h = num * (1.0 / jnp.maximum(jnp.abs(den), jnp.exp(-m_t)))
            if d == 0:
                hst_ref[:, rows] = h
            else:
                hst_ref[:, rows] += h
            if emit_state or ci + 1 < n_chunks:
                m_last = m_t[:, last:last + 1]
                al = a_in[:, last:last + 1]
                wl = jnp.exp(b_row[:, last:last + 1] - b_row + ig_row - m_last)
                C = al * C + _dot((vt * wl).astype(BF16), kb)
                n = al * n + _dot(jnp.broadcast_to(wl, (8, L)).astype(BF16), kb)[:1]
                m = m_last
        if emit_state:
            c_out[d] = C
            n_out[d] = n
            m_out[d] = jnp.broadcast_to(m, (1, LANES))

    for rows in chunks:
        hsum = hst_ref[:, rows].T
        gate = 1.0 / (1.0 + jnp.exp(-og_ref[rows, :]))
        h_ref[rows, :] = (_rms(hsum, nw_ref[...]) * gate).astype(BF16)


def _mlstm_call(z, grow, bias, norm_w, seq, batch, row0, init, heads):
    nh = B_HEADS
    has_init = init is not None
    emit_state = not has_init
    bcol = bias.reshape(2, 2, nh).transpose(2, 0, 1).reshape(nh, 4, 1)
    hp = heads
    qk_tiles = B_QK // (hp * B_DK)
    vd0 = 2 * B_QK // (hp * B_DV)
    nt = nh // hp
    in_specs = [
        pl.BlockSpec((seq, hp * B_DK), lambda b, h: (row0 + b, h)),
        pl.BlockSpec((seq, hp * B_DK), lambda b, h: (row0 + b, qk_tiles + h)),
        pl.BlockSpec((seq, hp * B_DV), lambda b, h: (row0 + b, vd0 + h)),
        pl.BlockSpec((seq, hp * B_DV), lambda b, h: (row0 + b, vd0 + nt + h)),
        pl.BlockSpec((hp, 4, seq), lambda b, h: (h, 0, row0 + b)),
        pl.BlockSpec((hp, 4, 1), lambda b, h: (h, 0, 0)),
        pl.BlockSpec((1, hp * B_DV), lambda b, h: (0, h)),
    ]
    args = [z, z, z, z, grow, bcol, norm_w.reshape(1, B_VD)]
    state_c = pl.BlockSpec((None, 2, hp, B_DV, B_DK), lambda b, h: (b, 0, h, 0, 0))
    state_v = pl.BlockSpec((None, 2, hp, 1, B_DK), lambda b, h: (b, 0, h, 0, 0))
    if has_init:
        c0, n0, m0 = init
        in_specs += [state_c, state_v, state_v]
        args += [c0, n0.reshape(batch, 2, nh, 1, B_DK),
                 jnp.broadcast_to(m0[..., None, None], (batch, 2, nh, 1, B_DK))]
    h_shape = jax.ShapeDtypeStruct((batch * seq, B_VD), BF16)
    h_spec = pl.BlockSpec((seq, hp * B_DV), lambda b, h: (b, h))
    if emit_state:
        out_shape = (h_shape,
                     jax.ShapeDtypeStruct((batch, 2, nh, B_DV, B_DK), F32),
                     jax.ShapeDtypeStruct((batch, 2, nh, 1, B_DK), F32),
                     jax.ShapeDtypeStruct((batch, 2, nh, 1, B_DK), F32))
        out_specs = (h_spec, state_c, state_v, state_v)
    else:
        out_shape, out_specs = h_shape, h_spec
    return pl.pallas_call(
        functools.partial(_mlstm_kernel, heads=hp, n_chunks=seq // B_CHUNK, has_init=has_init,
                          emit_state=emit_state),
        out_shape=out_shape,
        grid=(batch, nt),
        in_specs=in_specs, out_specs=out_specs,
        scratch_shapes=[pltpu.VMEM((hp, B_DV, seq), F32), pltpu.VMEM((hp, B_DV, seq), F32)],
        compiler_params=_params(("arbitrary", "arbitrary"), 48),
        name="mlstm_ctx" if emit_state else "mlstm_lat",
    )(*args)


def _gates_kernel(w_ref, h_ref, o_ref):
    o_ref[...] = _dot_nt(w_ref[...].astype(BF16), h_ref[...])


def _gate_rows(h, w_gate):
    nh, tm = B_HEADS, 1024
    w_rows = w_gate.T.reshape(4, nh, D_MODEL).transpose(1, 0, 2).reshape(4 * nh, D_MODEL)
    out = pl.pallas_call(
        _gates_kernel,
        out_shape=jax.ShapeDtypeStruct((4 * nh, N_TOK), F32),
        grid=(N_TOK // tm,),
        in_specs=[pl.BlockSpec((4 * nh, D_MODEL), lambda i: (0, 0)),
                  pl.BlockSpec((tm, D_MODEL), lambda i: (i, 0))],
        out_specs=pl.BlockSpec((4 * nh, tm), lambda i: (0, i)),
        compiler_params=_params(("arbitrary",), 32),
        name="gate_rows",
    )(w_rows, h)
    return out.reshape(nh, 4, N_TOK)


def _mixer_b(z, grow, state_c, state_n, state_m, bias, norm_w):
    nh = B_HEADS
    o_p, c_new, n_new, m_new = _mlstm_call(z, grow, bias, norm_w, SEQ, BATCH, 0, None, heads=4)
    o_s = _mlstm_call(z, grow, bias, norm_w, DEC_SEQ, DEC_BATCH, SAMPLE_ROW0,
                      (state_c, state_n, state_m), heads=1)
    return o_p, o_s, c_new, n_new.reshape(BATCH, 2, nh, B_DK), m_new[:, :, :, 0, 0]


C_SCALE = C_HEAD_DIM ** -0.5


def _c_ctx_kernel(q_ref, k_ref, v_ref, o_ref, ko_ref, vo_ref):
    for hh in range(CTX_HEADS):
        cols = slice(hh * LANES, (hh + 1) * LANES)
        k = k_ref[:, cols]
        v = v_ref[:, cols]
        cache_rows = _head_rows(pl.program_id(1) * CTX_HEADS + hh, SEQ, C_HEADS)
        ko_ref[cache_rows, :] = k
        vo_ref[cache_rows, :] = v
        s = _dot_nt(q_ref[:, cols].astype(BF16), k.astype(BF16)) * C_SCALE
        o_ref[:, cols] = _attend([s], [_with_ones(v)]).astype(BF16)


def _na_row_start(r):
    return min(max(r - NA_ROWS // 2, 0), GRID_H - NA_ROWS)


def _na_row_groups():
    groups = []
    for r in range(GRID_H):
        if groups and _na_row_start(groups[-1][0]) == _na_row_start(r):
            groups[-1].append(r)
        else:
            groups.append([r])
    return groups


C_LAT_HEADS = 2


def _c_lat_kernel(bias_ref, q_ref, k_ref, v_ref, pk_ref, pv_ref, o_ref, *scratch):
    for hh in range(C_LAT_HEADS):
        cols = slice(hh * LANES, (hh + 1) * LANES)
        _c_lat_head(pl.program_id(1) * C_LAT_HEADS + hh, bias_ref.at[hh], q_ref.at[:, cols], k_ref.at[:, cols],
                    v_ref.at[:, cols], pk_ref, pv_ref, o_ref.at[:, cols], *[s.at[hh] for s in scratch])


def _c_lat_head(head, bias_ref, q_ref, k_ref, v_ref, pk_ref, pv_ref, o_ref, kb_ref, v1_ref, sc_ref, ec_ref, acc_ref):
    n_win = NA_ROWS * GRID_W
    kb_ref[...] = k_ref[...].astype(BF16)
    v1_ref[...] = _with_ones(v_ref[...])
    past_rows = _head_rows(head, PAST_LEN, C_HEADS)
    sc_ref[...] = _dot_nt(q_ref[...].astype(BF16), pk_ref[past_rows, :].astype(BF16)) * C_SCALE
    for rows_g in _na_row_groups():
        r0 = _na_row_start(rows_g[0])
        rows = slice(rows_g[0] * GRID_W, (rows_g[-1] + 1) * GRID_W)
        win = slice(r0 * GRID_W, r0 * GRID_W + n_win)
        strips = []
        for r in rows_g:
            strip = NA_ROWS - 1 - (r - r0)
            even = strip - strip % 2
            strips.append(bias_ref[strip % 2, :, even * GRID_W:even * GRID_W + n_win])
        bias = strips[0] if len(strips) == 1 else jnp.concatenate(strips, axis=0)
        s_n = _dot_nt(q_ref[rows, :].astype(BF16), kb_ref[win, :]) * C_SCALE + bias
        s_c = sc_ref[rows, :]
        m = jnp.maximum(jnp.max(s_n, axis=-1, keepdims=True), jnp.max(s_c, axis=-1, keepdims=True))
        ec_ref[rows, :] = jnp.exp(s_c - m).astype(BF16)
        acc_ref[rows, :] = _dot(jnp.exp(s_n - m).astype(BF16), v1_ref[win, :])
    acc = acc_ref[...] + _dot(ec_ref[...], _with_ones(pv_ref[past_rows, :]))
    o_ref[...] = (acc[:, :LANES] / acc[:, LANES:]).astype(BF16)


def _na_bias(rpb):
    n_drow, n_dcol = 2 * NA_ROWS - 1, 2 * NA_COLS - 1
    cq = np.arange(GRID_W)[:, None]
    kc = np.arange(GRID_W)[None, :]
    cstart = np.clip(cq - NA_COLS // 2, 0, GRID_W - NA_COLS)
    col_ok = (kc >= cstart) & (kc < cstart + NA_COLS)
    dcol = np.clip(kc - cq, -(NA_COLS - 1), NA_COLS - 1) + NA_COLS - 1
    onehot = (dcol[:, :, None] == np.arange(n_dcol)).astype(np.float32)
    strips = jnp.einsum('hab,qkb->hqak', rpb.astype(F32), onehot, precision=lax.Precision.HIGHEST)
    strips = jnp.where(col_ok[None, :, None, :], strips, NEG).reshape(C_HEADS, GRID_W, n_drow * GRID_W)
    strips = jnp.pad(strips, ((0, 0), (0, 0), (0, 2 * GRID_W)))
    return jnp.stack([strips[:, :, :(n_drow + 1) * GRID_W], strips[:, :, GRID_W:]], axis=1)


def _mixer_c(qkv, cache_k, cache_v, rpb):
    nh = C_HEADS
    cw = CTX_HEADS * LANES
    nt = nh // CTX_HEADS
    ctx_out = pl.BlockSpec((SEQ, cw), lambda b, h: (b, h))
    kv_out = pl.BlockSpec((SEQ * nh, C_HEAD_DIM), lambda b, h: (b, 0))
    kv_shape = jax.ShapeDtypeStruct((N_PROMPT * nh, C_HEAD_DIM), F32)
    o_p, k_new, v_new = pl.pallas_call(
        _c_ctx_kernel,
        out_shape=(jax.ShapeDtypeStruct((N_PROMPT, D_MODEL), BF16), kv_shape, kv_shape),
        grid=(BATCH, nt),
        in_specs=[pl.BlockSpec((SEQ, cw), lambda b, h: (b, h)),
                  pl.BlockSpec((SEQ, cw), lambda b, h: (b, nt + h)),
                  pl.BlockSpec((SEQ, cw), lambda b, h: (b, 2 * nt + h))],
        out_specs=(ctx_out, kv_out, kv_out),
        compiler_params=_params(("arbitrary", "arbitrary"), 40),
        name="na_ctx",
    )(qkv, qkv, qkv)

    past = pl.BlockSpec((PAST_LEN * nh, C_HEAD_DIM), lambda b, h: (b, 0))
    lh = C_LAT_HEADS
    lw, nl = lh * LANES, nh // lh
    o_s = pl.pallas_call(
        _c_lat_kernel,
        out_shape=jax.ShapeDtypeStruct((N_SAMPLE, D_MODEL), BF16),
        grid=(DEC_BATCH, nl),
        in_specs=[pl.BlockSpec((lh, 2, GRID_W, 2 * NA_ROWS * GRID_W), lambda b, h: (h, 0, 0, 0)),
                  pl.BlockSpec((DEC_SEQ, lw), lambda b, h: (SAMPLE_ROW0 + b, h)),
                  pl.BlockSpec((DEC_SEQ, lw), lambda b, h: (SAMPLE_ROW0 + b, nl + h)),
                  pl.BlockSpec((DEC_SEQ, lw), lambda b, h: (SAMPLE_ROW0 + b, 2 * nl + h)),
                  past, past],
        out_specs=pl.BlockSpec((DEC_SEQ, lw), lambda b, h: (b, h)),
        scratch_shapes=[pltpu.VMEM((lh, DEC_SEQ, LANES), BF16), pltpu.VMEM((lh, DEC_SEQ, 2 * LANES), BF16),
                        pltpu.VMEM((lh, DEC_SEQ, PAST_LEN), F32), pltpu.VMEM((lh, DEC_SEQ, PAST_LEN), BF16),
                        pltpu.VMEM((lh, DEC_SEQ, 2 * LANES), F32)],
        compiler_params=_params(("arbitrary", "arbitrary"), 56),
        name="na_lat",
    )(_na_bias(rpb), qkv, qkv, qkv,
      cache_k.reshape(DEC_BATCH * PAST_LEN * nh, C_HEAD_DIM), cache_v.reshape(DEC_BATCH * PAST_LEN * nh, C_HEAD_DIM))
    return o_p, o_s, k_new, v_new


D_SCALE = D_HEAD_DIM ** -0.5
D_QBLOCK = 128
D_BAND = 3 * D_QBLOCK


D_CTX_PAIRS = 2


def _d_ctx_kernel(sink_ref, q_ref, k_ref, v_ref, o_ref, ko_ref, vo_ref):
    lo = _lane_lo((SEQ, LANES))
    for pp in range(D_CTX_PAIRS):
        pair = pl.program_id(1) * D_CTX_PAIRS + pp
        kv_cols = slice(pp * LANES, (pp + 1) * LANES)
        for half in (0, 1):
            cache_rows = _head_rows(2 * pair + half, SEQ, D_KV_HEADS)
            half_cols = slice(pp * LANES + half * HALF, pp * LANES + (half + 1) * HALF)
            ko_ref[cache_rows, :] = k_ref[:, half_cols]
            vo_ref[cache_rows, :] = v_ref[:, half_cols]
        kb, v1 = k_ref[:, kv_cols].astype(BF16), _with_ones(v_ref[:, kv_cols])
        for tile in range(4):
            kv = tile // 2
            keep = lo if kv == 0 else ~lo
            cols = slice((4 * pp + tile) * LANES, (4 * pp + tile + 1) * LANES)
            qt = q_ref[:, cols] * D_SCALE
            halves = []
            for e in (0, 1):
                qe = qt if e == kv else pltpu.roll(qt, HALF, axis=1)
                s = _dot_nt(jnp.where(keep, qe, 0.0).astype(BF16), kb)
                o = _attend([s], [v1], sink=sink_ref[pair * 2 * D_GROUP + tile * 2 + e])
                halves.append(o if e == kv else pltpu.roll(o, HALF, axis=1))
            o_ref[:, cols] = jnp.where(lo, halves[0], halves[1]).astype(BF16)


def _d_lat_kernel(sink_ref, cos_ref, nxt_ref, prv_ref, q_ref, k_ref, v_ref, pk_ref, pv_ref, o_ref,
                  kb_ref, v1_ref, pkb_ref, pv1_ref, e_ref, es_ref):
    pair = pl.program_id(1)
    kb_ref[...] = _rope(k_ref[...], cos_ref[...], nxt_ref[...], prv_ref[...]).astype(BF16)
    v1_ref[...] = _with_ones(v_ref[...])
    pkb_ref[...] = pk_ref[...].astype(BF16)
    pv1_ref[...] = _with_ones(pv_ref[...])
    tq = D_QBLOCK
    lo = _lane_lo((tq, LANES))
    n_blocks = DEC_SEQ // tq

    def band_of(bi):
        start = min(max((bi - 1) * tq, 0), DEC_SEQ - D_BAND)
        return start, slice(start, start + D_BAND)

    def weights(bi):
        rows = slice(bi * tq, (bi + 1) * tq)
        start, band = band_of(bi)
        cos, nxt, prv = cos_ref[rows, :], nxt_ref[rows, :], prv_ref[rows, :]
        qpos = bi * tq + lax.broadcasted_iota(jnp.int32, (tq, D_BAND), 0)
        kpos = start + lax.broadcasted_iota(jnp.int32, (tq, D_BAND), 1)
        mask = jnp.where(jnp.abs(qpos - kpos) <= WINDOW, 0.0, NEG)
        mask4 = jnp.concatenate([mask] * D_GROUP, axis=0)
        tiles = [_rope(q_ref[rows, t * LANES:(t + 1) * LANES], cos, nxt, prv) * D_SCALE for t in range(4)]
        for kv in (0, 1):
            keep = lo if kv == 0 else ~lo
            stack, sinks = [], []
            for t in (2 * kv, 2 * kv + 1):
                for e in (0, 1):
                    qe = tiles[t] if e == kv else pltpu.roll(tiles[t], HALF, axis=1)
                    stack.append(jnp.where(keep, qe, 0.0))
                    sinks.append(jnp.full((tq, 1), sink_ref[pair * 2 * D_GROUP + t * 2 + e], F32))
            qs = jnp.concatenate(stack, axis=0).astype(BF16)
            sink = jnp.concatenate(sinks, axis=0)
            s_band = _dot_nt(qs, kb_ref[band, :]) + mask4
            s_past = _dot_nt(qs, pkb_ref[...])
            m = jnp.maximum(jnp.maximum(jnp.max(s_band, axis=-1, keepdims=True),
                                        jnp.max(s_past, axis=-1, keepdims=True)), sink)
            e_ref[bi % 2, kv, :, :D_BAND] = jnp.exp(s_band - m).astype(BF16)
            e_ref[bi % 2, kv, :, D_BAND:] = jnp.exp(s_past - m).astype(BF16)
            es_ref[bi % 2, kv] = jnp.exp(sink - m)

    def values(bi):
        rows = slice(bi * tq, (bi + 1) * tq)
        _, band = band_of(bi)
        outs = {}
        for kv in (0, 1):
            acc = (_dot(e_ref[bi % 2, kv, :, :D_BAND], v1_ref[band, :])
                   + _dot(e_ref[bi % 2, kv, :, D_BAND:], pv1_ref[...]))
            o = acc[:, :LANES] / (acc[:, LANES:] + es_ref[bi % 2, kv])
            for i, t in enumerate((2 * kv, 2 * kv + 1)):
                for e in (0, 1):
                    piece = o[(2 * i + e) * tq:(2 * i + e + 1) * tq, :]
                    outs[(t, e)] = piece if e == kv else pltpu.roll(piece, HALF, axis=1)
        for t in range(4):
            o_ref[rows, t * LANES:(t + 1) * LANES] = jnp.where(lo, outs[(t, 0)], outs[(t, 1)]).astype(BF16)

    weights(0)
    for bi in range(1, n_blocks):
        values(bi - 1)
        weights(bi)
    values(n_blocks - 1)


def _mixer_d(qkv, cache_k, cache_v, sink, rope):
    n_pairs = D_KV_HEADS // 2
    qw = 2 * D_GROUP * D_HEAD_DIM
    k0 = D_HEADS * D_HEAD_DIM // LANES
    v0 = k0 + n_pairs
    kvw = D_KV_HEADS * D_HEAD_DIM
    smem = pl.BlockSpec(memory_space=pltpu.SMEM)
    sink = sink.astype(F32)
    cp = D_CTX_PAIRS
    kv_out = pl.BlockSpec((SEQ * D_KV_HEADS, D_HEAD_DIM), lambda b, p: (b, 0))
    kv_shape = jax.ShapeDtypeStruct((N_PROMPT * D_KV_HEADS, D_HEAD_DIM), F32)
    o_p, k_new, v_new = pl.pallas_call(
        _d_ctx_kernel,
        out_shape=(jax.ShapeDtypeStruct((N_PROMPT, D_MODEL), BF16), kv_shape, kv_shape),
        grid=(BATCH, n_pairs // cp),
        in_specs=[smem,
                  pl.BlockSpec((SEQ, cp * qw), lambda b, p: (b, p)),
                  pl.BlockSpec((SEQ, cp * LANES), lambda b, p: (b, k0 // cp + p)),
                  pl.BlockSpec((SEQ, cp * LANES), lambda b, p: (b, v0 // cp + p))],
        out_specs=(pl.BlockSpec((SEQ, cp * qw), lambda b, p: (b, p)), kv_out, kv_out),
        compiler_params=_params(("arbitrary", "arbitrary"), 40),
        name="gqa_ctx",
    )(sink, qkv, qkv, qkv)

    table = pl.BlockSpec((DEC_SEQ, LANES), lambda b, p: (0, 0))
    past = pl.BlockSpec((PAST_LEN, LANES), lambda b, p: (b, p))
    o_s = pl.pallas_call(
        _d_lat_kernel,
        out_shape=jax.ShapeDtypeStruct((N_SAMPLE, D_MODEL), BF16),
        grid=(DEC_BATCH, n_pairs),
        in_specs=[smem, table, table, table,
                  pl.BlockSpec((DEC_SEQ, qw), lambda b, p: (SAMPLE_ROW0 + b, p)),
                  pl.BlockSpec((DEC_SEQ, LANES), lambda b, p: (SAMPLE_ROW0 + b, k0 + p)),
                  pl.BlockSpec((DEC_SEQ, LANES), lambda b, p: (SAMPLE_ROW0 + b, v0 + p)),
                  past, past],
        out_specs=pl.BlockSpec((DEC_SEQ, qw), lambda b, p: (b, p)),
        scratch_shapes=[pltpu.VMEM((DEC_SEQ, LANES), BF16), pltpu.VMEM((DEC_SEQ, 2 * LANES), BF16),
                        pltpu.VMEM((PAST_LEN, LANES), BF16), pltpu.VMEM((PAST_LEN, 2 * LANES), BF16),
                        pltpu.VMEM((2, 2, D_GROUP * D_QBLOCK, D_BAND + PAST_LEN), BF16),
                        pltpu.VMEM((2, 2, D_GROUP * D_QBLOCK, 1), F32)],
        compiler_params=_params(("arbitrary", "arbitrary"), 40),
        name="gqa_lat",
    )(sink, *rope, qkv, qkv, qkv,
      cache_k.reshape(DEC_BATCH * PAST_LEN, kvw), cache_v.reshape(DEC_BATCH * PAST_LEN, kvw))
    return o_p, o_s, k_new, v_new


def kernel(x_prompt, x_sample, cache_a_k, cache_a_v, state_b_C, state_b_n, state_b_m, cache_c_k, cache_c_v,
           cache_d_k, cache_d_v, c, c_ctx, w_mod, b_mod, g_norm, w_ff1, w_ff2, a_w_in, a_w_out, a_lambda,
           a_subln, b_w_in, b_gate_bias, b_w_out, b_norm, c_w_in, c_w_out, c_rpb, d_w_in, d_w_out, d_sink):
    cond = jnp.concatenate([c_ctx[None, :], c, jnp.zeros((N_COND - 1 - DEC_BATCH, D_MODEL), F32)], axis=0)
    mod = _modulation(cond, w_mod, b_mod).reshape(DEPTH, N_COND, N_MOD, 1, D_MODEL)
    gains = g_norm.reshape(DEPTH, 4, 1, D_MODEL)
    rope = _rope_tables()
    new = {name: [] for name in ("a_k", "a_v", "b_C", "b_n", "b_m", "c_k", "c_v", "d_k", "d_v")}

    x, h = _adaln(x_prompt, x_sample, gains, mod, 0)
    for i in range(DEPTH):
        kind, j = i % N_MIXERS, i // N_MIXERS
        if kind == 0:
            lam_init = 0.8 - 0.6 * math.exp(-0.3 * i)
            qkv = _project(h, a_w_in[j])
            o_p, o_s, k_new, v_new = _mixer_a(qkv, cache_a_k[:, j], cache_a_v[:, j], a_lambda[j], a_subln[j],
                                              lam_init, rope)
            w_out = a_w_out[j]
            new["a_k"].append(k_new.reshape(BATCH, SEQ, A_HEADS, A_V_DIM))
            new["a_v"].append(v_new.reshape(BATCH, SEQ, A_HEADS, A_V_DIM))
        elif kind == 1:
            w_in = b_w_in[j]
            z = _project(h, w_in, n=B_MAIN)
            grow = _gate_rows(h, w_in[:, B_MAIN:])
            o_p, o_s, c_new, n_new, m_new = _mixer_b(z, grow, state_b_C[:, j], state_b_n[:, j], state_b_m[:, j],
                                                     b_gate_bias[j], b_norm[j])
            w_out = b_w_out[j]
            new["b_C"].append(c_new)
            new["b_n"].append(n_new)
            new["b_m"].append(m_new)
        elif kind == 2:
            qkv = _project(h, c_w_in[j])
            o_p, o_s, k_new, v_new = _mixer_c(qkv, cache_c_k[:, j], cache_c_v[:, j], c_rpb[j])
            w_out = c_w_out[j]
            new["c_k"].append(k_new.reshape(BATCH, SEQ, C_HEADS, C_HEAD_DIM))
            new["c_v"].append(v_new.reshape(BATCH, SEQ, C_HEADS, C_HEAD_DIM))
        else:
            qkv = _project(h, d_w_in[j])
            o_p, o_s, k_new, v_new = _mixer_d(qkv, cache_d_k[:, j], cache_d_v[:, j], d_sink[j], rope)
            w_out = d_w_out[j]
            new["d_k"].append(k_new.reshape(BATCH, SEQ, D_KV_HEADS, D_HEAD_DIM))
            new["d_v"].append(v_new.reshape(BATCH, SEQ, D_KV_HEADS, D_HEAD_DIM))
        x, h = _out_project(o_p, o_s, w_out, x, gains, mod, i)
        y = _mlp(h, w_ff1, w_ff2, i)
        x, h = _residual(x, y, gains, mod, i, 5, 3, (i + 1, 0, 0, 1) if i + 1 < DEPTH else None)

    x_p, x_s = x
    stack = lambda name: jnp.stack(new[name], axis=1)
    return (x_p.reshape(BATCH, SEQ, D_MODEL), x_s.reshape(DEC_BATCH, DEC_SEQ, D_MODEL),
            stack("a_k"), stack("a_v"), stack("b_C"), stack("b_n"), stack("b_m"),
            stack("c_k"), stack("c_v"), stack("d_k"), stack("d_v"))
```

```python
import functools
import math

import numpy as np
import jax
import jax.numpy as jnp
from jax import lax
from jax.experimental import pallas as pl
from jax.experimental.pallas import tpu as pltpu

D_MODEL = 2048
BATCH = 16
SEQ = 256
DEPTH = 4
DEC_BATCH = 4
DEC_SEQ = 1024
PAST_LEN = 512
GRID_W = 64
N_MIXERS = 4
D_FF = 4 * D_MODEL
N_MOD = 6
RMS_EPS = 1e-6
ROPE_THETA = 10000.0

A_HEADS = 16
A_HALF_DIM = 64
A_V_DIM = 128
B_HEADS = 8
B_DK = 128
B_DV = 256
B_QK = B_HEADS * B_DK
B_VD = B_HEADS * B_DV
B_MAIN = 2 * B_QK + 2 * B_VD
C_HEADS = 16
C_HEAD_DIM = 128
NA_ROWS = 8
NA_COLS = 16
GRID_H = DEC_SEQ // GRID_W
D_HEADS = 32
D_KV_HEADS = 8
D_GROUP = 4
D_HEAD_DIM = 64
WINDOW = 128

N_PROMPT = BATCH * SEQ
N_SAMPLE = DEC_BATCH * DEC_SEQ
N_TOK = N_PROMPT + N_SAMPLE
N_COND = 8
SAMPLE_ROW0 = N_PROMPT // DEC_SEQ

LANES = 128
HALF = LANES // 2
NEG = -1e30
MIB = 1024 * 1024
F32 = jnp.float32
BF16 = jnp.bfloat16


def _params(sem, vmem_mib):
    return pltpu.CompilerParams(dimension_semantics=sem, vmem_limit_bytes=vmem_mib * MIB)


def _dot(a, b):
    return jnp.dot(a, b, preferred_element_type=F32)


def _dot_nt(a, b):
    return lax.dot_general(a, b, (((1,), (1,)), ((), ())), preferred_element_type=F32)


def _dot_tn(a, b):
    return lax.dot_general(a, b, (((0,), (0,)), ((), ())), preferred_element_type=F32)


def _rms(x, g):
    return x * lax.rsqrt(jnp.mean(x * x, axis=-1, keepdims=True) + RMS_EPS) * g


def _cond_index(row0):
    return jnp.where(row0 < N_PROMPT, 0, 1 + (row0 - N_PROMPT) // DEC_SEQ)


def _mod_kernel(c_ref, w_ref, b_ref, o_ref):
    c = c_ref[...]
    s = (c / (1.0 + jnp.exp(-c))).astype(BF16)
    o_ref[...] = _dot(s, w_ref[...].astype(BF16)) + b_ref[...]


def _modulation(cond, w_mod, b_mod):
    tn = 1024
    n = N_MOD * D_MODEL
    return pl.pallas_call(
        _mod_kernel,
        out_shape=jax.ShapeDtypeStruct((DEPTH, N_COND, n), F32),
        grid=(DEPTH, n // tn),
        in_specs=[
            pl.BlockSpec((N_COND, D_MODEL), lambda l, j: (0, 0)),
            pl.BlockSpec((None, D_MODEL, tn), lambda l, j: (l, 0, j)),
            pl.BlockSpec((None, 1, tn), lambda l, j: (l, 0, j)),
        ],
        out_specs=pl.BlockSpec((None, N_COND, tn), lambda l, j: (l, 0, j)),
        compiler_params=_params(("arbitrary", "arbitrary"), 40),
        name="modulation",
    )(cond, w_mod, b_mod.reshape(DEPTH, 1, n))


ROW_TILE = 512


def _mod_spec(layer, which, tile0=0):
    return pl.BlockSpec((None, None, None, 1, D_MODEL),
                        lambda i: (layer, _cond_index((i + tile0) * ROW_TILE), which, 0, 0))


def _gain_spec(layer, which):
    return pl.BlockSpec((None, None, 1, D_MODEL), lambda i: (layer, which, 0, 0))


def _adaln_kernel(xp_ref, xs_ref, g_ref, shift_ref, scale_ref, x_ref, h_ref):
    def emit(x):
        x_ref[...] = x
        h_ref[...] = (_rms(x, g_ref[...]) * (1.0 + scale_ref[...]) + shift_ref[...]).astype(BF16)

    @pl.when(pl.program_id(0) < N_PROMPT // ROW_TILE)
    def _():
        emit(xp_ref[...])

    @pl.when(pl.program_id(0) >= N_PROMPT // ROW_TILE)
    def _():
        emit(xs_ref[...])


def _adaln(x_prompt, x_sample, gains, mod, layer):
    p_tiles = N_PROMPT // ROW_TILE
    row = pl.BlockSpec((ROW_TILE, D_MODEL), lambda i: (i, 0))
    return pl.pallas_call(
        _adaln_kernel,
        out_shape=(jax.ShapeDtypeStruct((N_TOK, D_MODEL), F32), jax.ShapeDtypeStruct((N_TOK, D_MODEL), BF16)),
        grid=(N_TOK // ROW_TILE,),
        in_specs=[pl.BlockSpec((ROW_TILE, D_MODEL), lambda i: (jnp.minimum(i, p_tiles - 1), 0)),
                  pl.BlockSpec((ROW_TILE, D_MODEL), lambda i: (jnp.maximum(i - p_tiles, 0), 0)),
                  _gain_spec(layer, 0), _mod_spec(layer, 0), _mod_spec(layer, 1)],
        out_specs=(row, row),
        compiler_params=_params(("arbitrary",), 40),
        name="adaln",
    )(x_prompt.reshape(N_PROMPT, D_MODEL), x_sample.reshape(N_SAMPLE, D_MODEL), gains, mod, mod)


def _residual_adaln_kernel(x_ref, y_ref, gate_ref, gpost_ref, gpre_ref, shift_ref, scale_ref, xo_ref, h_ref):
    x = x_ref[...] + gate_ref[...] * _rms(y_ref[...], gpost_ref[...])
    xo_ref[...] = x
    h = _rms(x, gpre_ref[...]) * (1.0 + scale_ref[...]) + shift_ref[...]
    h_ref[...] = h.astype(BF16)


def _residual_kernel(x_ref, y_ref, gate_ref, gpost_ref, xo_ref):
    xo_ref[...] = x_ref[...] + gate_ref[...] * _rms(y_ref[...], gpost_ref[...])


def _residual(x, y, gains, mod, layer, gate_idx, post_idx, nxt):
    row = pl.BlockSpec((ROW_TILE, D_MODEL), lambda i: (i, 0))
    if nxt is None:
        outs = []
        for tile0, n_rows in ((0, N_PROMPT), (N_PROMPT // ROW_TILE, N_SAMPLE)):
            src = pl.BlockSpec((ROW_TILE, D_MODEL), lambda i, t=tile0: (i + t, 0))
            outs.append(pl.pallas_call(
                _residual_kernel, out_shape=jax.ShapeDtypeStruct((n_rows, D_MODEL), F32),
                grid=(n_rows // ROW_TILE,),
                in_specs=[src, src, _mod_spec(layer, gate_idx, tile0), _gain_spec(layer, post_idx)],
                out_specs=row,
                compiler_params=_params(("arbitrary",), 32), name="residual",
            )(x, y, mod, gains))
        return outs, None
    nl, ng, nshift, nscale = nxt
    return pl.pallas_call(
        _residual_adaln_kernel,
        out_shape=(jax.ShapeDtypeStruct((N_TOK, D_MODEL), F32), jax.ShapeDtypeStruct((N_TOK, D_MODEL), BF16)),
        grid=(N_TOK // ROW_TILE,),
        in_specs=[row, row, _mod_spec(layer, gate_idx), _gain_spec(layer, post_idx),
                  _gain_spec(nl, ng), _mod_spec(nl, nshift), _mod_spec(nl, nscale)],
        out_specs=(row, row),
        compiler_params=_params(("arbitrary",), 40), name="residual_adaln",
    )(x, y, mod, gains, gains, mod, mod)


PROJ_TM = 1024


def _proj_kernel(a_ref, w_ref, o_ref, wbf_ref):
    @pl.when(pl.program_id(1) == 0)
    def _():
        wbf_ref[...] = w_ref[...].astype(BF16)

    o_ref[...] = _dot(a_ref[...], wbf_ref[...]).astype(o_ref.dtype)


def _project(a, w, layer, n=None, out_dtype=F32):
    k = w.shape[1]
    n = w.shape[2] if n is None else n
    tm = PROJ_TM
    tn = min(n, 1024)
    return pl.pallas_call(
        _proj_kernel,
        out_shape=jax.ShapeDtypeStruct((N_TOK, n), out_dtype),
        grid=(n // tn, N_TOK // tm),
        in_specs=[pl.BlockSpec((tm, k), lambda j, i: (i, 0)),
                  pl.BlockSpec((None, k, tn), lambda j, i: (layer, 0, j))],
        out_specs=pl.BlockSpec((tm, tn), lambda j, i: (i, j)),
        scratch_shapes=[pltpu.VMEM((k, tn), BF16)],
        compiler_params=_params(("arbitrary", "arbitrary"), 48),
        name="project",
    )(a, w)


def _proj_t_kernel(a_ref, wt_ref, o_ref, wbf_ref):
    @pl.when(pl.program_id(1) == 0)
    def _():
        wbf_ref[...] = wt_ref[...].astype(BF16)

    o_ref[...] = _dot_nt(a_ref[...], wbf_ref[...])


def _project_t(a, w_t, layer, n):
    k = w_t.shape[2]
    tm, tn = PROJ_TM, 1024
    return pl.pallas_call(
        _proj_t_kernel,
        out_shape=jax.ShapeDtypeStruct((N_TOK, n), F32),
        grid=(n // tn, N_TOK // tm),
        in_specs=[pl.BlockSpec((tm, k), lambda j, i: (i, 0)),
                  pl.BlockSpec((None, tn, k), lambda j, i: (layer, j, 0))],
        out_specs=pl.BlockSpec((tm, tn), lambda j, i: (i, j)),
        scratch_shapes=[pltpu.VMEM((tn, k), BF16)],
        compiler_params=_params(("arbitrary", "arbitrary"), 48),
        name="project_t",
    )(a, w_t)


OUT_TM = 512


def _out_proj_kernel(ap_ref, as_ref, w_ref, x_ref, gate_ref, gpost_ref, gpre_ref, shift_ref, scale_ref,
                     xo_ref, h_ref):
    def finish(a):
        x = x_ref[...] + gate_ref[...] * _rms(_dot(a, w_ref[...]), gpost_ref[...])
        xo_ref[...] = x
        h_ref[...] = (_rms(x, gpre_ref[...]) * (1.0 + scale_ref[...]) + shift_ref[...]).astype(BF16)

    @pl.when(pl.program_id(0) < N_PROMPT // OUT_TM)
    def _():
        finish(ap_ref[...])

    @pl.when(pl.program_id(0) >= N_PROMPT // OUT_TM)
    def _():
        finish(as_ref[...])


def _out_project(o_p, o_s, w_out, x, gains, mod, layer):
    tm = OUT_TM
    p_tiles = N_PROMPT // tm
    row = pl.BlockSpec((tm, D_MODEL), lambda i: (i, 0))

    def mod_spec(which):
        return pl.BlockSpec((None, None, None, 1, D_MODEL), lambda i: (layer, _cond_index(i * tm), which, 0, 0))

    return pl.pallas_call(
        _out_proj_kernel,
        out_shape=(jax.ShapeDtypeStruct((N_TOK, D_MODEL), F32), jax.ShapeDtypeStruct((N_TOK, D_MODEL), BF16)),
        grid=(N_TOK // tm,),
        in_specs=[pl.BlockSpec((tm, D_MODEL), lambda i: (jnp.minimum(i, p_tiles - 1), 0)),
                  pl.BlockSpec((tm, D_MODEL), lambda i: (jnp.maximum(i - p_tiles, 0), 0)),
                  pl.BlockSpec((D_MODEL, D_MODEL), lambda i: (0, 0)),
                  row, mod_spec(2), _gain_spec(layer, 1), _gain_spec(layer, 2), mod_spec(3), mod_spec(4)],
        out_specs=(row, row),
        compiler_params=_params(("arbitrary",), 56),
        name="out_project",
    )(o_p, o_s, w_out.astype(BF16), x, mod, gains, gains, mod, mod)


def _mlp_kernel(h_ref, w1_ref, w2_ref, o_ref):
    @pl.when(pl.program_id(1) == 0)
    def _():
        o_ref[...] = jnp.zeros_like(o_ref)

    u = jnp.maximum(_dot(h_ref[...], w1_ref[...].astype(BF16)), 0.0)
    o_ref[...] += _dot((u * u).astype(BF16), w2_ref[...].astype(BF16))


def _mlp(h, w1, w2, layer):
    tm, tf = 1024, 512
    return pl.pallas_call(
        _mlp_kernel,
        out_shape=jax.ShapeDtypeStruct((N_TOK, D_MODEL), F32),
        grid=(N_TOK // tm, D_FF // tf),
        in_specs=[pl.BlockSpec((tm, D_MODEL), lambda i, f: (i, 0)),
                  pl.BlockSpec((None, D_MODEL, tf), lambda i, f: (layer, 0, f)),
                  pl.BlockSpec((None, tf, D_MODEL), lambda i, f: (layer, f, 0))],
        out_specs=pl.BlockSpec((tm, D_MODEL), lambda i, f: (i, 0)),
        compiler_params=_params(("arbitrary", "arbitrary"), 56),
        name="mlp",
    )(h, w1, w2)


def _rope_tables():
    t = jnp.arange(DEC_SEQ)
    lane = np.arange(LANES)
    f = lane % 32
    first = f < 16
    inv = ROPE_THETA ** (-jnp.arange(16, dtype=F32) / 16)
    pos = jnp.where((lane % 64 < 32)[None, :], (t // GRID_W)[:, None], (t % GRID_W)[:, None]).astype(F32)
    ang = pos * inv[f % 16][None, :]
    cos, sin = jnp.cos(ang), jnp.sin(ang)
    sin_next = jnp.where(first[None, :], -sin, 0.0)
    sin_prev = jnp.where(first[None, :], 0.0, sin)
    return cos, sin_next, sin_prev


def _rope(x, cos, sin_next, sin_prev):
    return (x * cos + pltpu.roll(x, LANES - 16, axis=1) * sin_next
            + pltpu.roll(x, 16, axis=1) * sin_prev)


def _lane_lo(shape):
    return lax.broadcasted_iota(jnp.int32, shape, 1) < HALF


def _with_ones(v):
    return jnp.concatenate([v.astype(BF16), jnp.ones(v.shape, BF16)], axis=1)


def _attend(scores, vals_ones, sink=None):
    m = None
    for s in scores:
        mi = jnp.max(s, axis=-1, keepdims=True)
        m = mi if m is None else jnp.maximum(m, mi)
    if sink is not None:
        m = jnp.maximum(m, sink)
    acc = None
    for s, v in zip(scores, vals_ones):
        part = _dot(jnp.exp(s - m).astype(BF16), v)
        acc = part if acc is None else acc + part
    den = acc[:, LANES:]
    if sink is not None:
        den = den + jnp.exp(sink - m)
    return acc[:, :LANES] / den


CTX_HEADS = 8


A_SCALE = A_HALF_DIM ** -0.5


def _diff_lambda(lp, lam_init):
    a = jnp.sum(lp[0] * lp[1], axis=-1, keepdims=True)
    b = jnp.sum(lp[2] * lp[3], axis=-1, keepdims=True)
    return jnp.exp(a) - jnp.exp(b) + lam_init


def _diff_finish(o, subln_ref, lam_init):
    return (_rms(o, subln_ref[...]) * (1.0 - lam_init)).astype(BF16)


def _head_rows(head, n_tokens, n_heads):
    return pl.ds(head, n_tokens, stride=n_heads)


def _a_ctx_kernel(lam_ref, subln_ref, q_ref, k_ref, v_ref, o_ref, ko_ref, vo_ref, *, lam_init):
    lo = _lane_lo((SEQ, LANES))
    for hh in range(CTX_HEADS):
        cols = slice(hh * LANES, (hh + 1) * LANES)
        q = q_ref[:, cols] * A_SCALE
        k = k_ref[:, cols]
        v = v_ref[:, cols]
        cache_rows = _head_rows(pl.program_id(1) * CTX_HEADS + hh, SEQ, A_HEADS)
        ko_ref[cache_rows, :] = k
        vo_ref[cache_rows, :] = v
        kb, v1 = k.astype(BF16), _with_ones(v)
        o1 = _attend([_dot_nt(jnp.where(lo, q, 0.0).astype(BF16), kb)], [v1])
        o2 = _attend([_dot_nt(jnp.where(lo, 0.0, q).astype(BF16), kb)], [v1])
        lam = _diff_lambda(lam_ref[:, hh], lam_init)
        o_ref[:, cols] = _diff_finish(o1 - lam * o2, subln_ref, lam_init)


A_LAT_TQ = 256


A_LAT_HEADS = 2


def _a_lat_kernel(lam_ref, subln_ref, cos_ref, nxt_ref, prv_ref, q_ref, k_ref, v_ref, pk_ref, pv_ref, o_ref,
                  kb_ref, v1_ref, e_ref, *, lam_init):
    stages = []
    for hh in range(A_LAT_HEADS):
        cols = slice(hh * LANES, (hh + 1) * LANES)
        stages.append(_a_lat_head(pl.program_id(1) * A_LAT_HEADS + hh, lam_ref[:, hh], subln_ref, cos_ref, nxt_ref,
                                  prv_ref, q_ref.at[:, cols], k_ref.at[:, cols], v_ref.at[:, cols], pk_ref, pv_ref,
                                  o_ref.at[:, cols], kb_ref.at[hh], v1_ref.at[hh], e_ref.at[hh], lam_init))
    n_blocks = DEC_SEQ // A_LAT_TQ
    for weights, _ in stages:
        weights(0)
    for i in range(1, n_blocks):
        for _, values in stages:
            values(i - 1)
        for weights, _ in stages:
            weights(i)
    for _, values in stages:
        values(n_blocks - 1)


def _a_lat_head(head, lam_p, subln_ref, cos_ref, nxt_ref, prv_ref, q_ref, k_ref, v_ref, pk_ref, pv_ref, o_ref,
                kb_ref, v1_ref, e_ref, lam_init):
    past_rows = _head_rows(head, PAST_LEN, A_HEADS)
    kb_ref[:PAST_LEN, :] = pk_ref[past_rows, :].astype(BF16)
    kb_ref[PAST_LEN:, :] = _rope(k_ref[...], cos_ref[...], nxt_ref[...], prv_ref[...]).astype(BF16)
    v1_ref[:PAST_LEN, :] = _with_ones(pv_ref[past_rows, :])
    v1_ref[PAST_LEN:, :] = _with_ones(v_ref[...])
    lam = _diff_lambda(lam_p, lam_init)
    lo = _lane_lo((A_LAT_TQ, LANES))

    def weights(i):
        rows = slice(i * A_LAT_TQ, (i + 1) * A_LAT_TQ)
        q = _rope(q_ref[rows, :], cos_ref[rows, :], nxt_ref[rows, :], prv_ref[rows, :]) * A_SCALE
        for half, keep in enumerate((lo, ~lo)):
            s = _dot_nt(jnp.where(keep, q, 0.0).astype(BF16), kb_ref[...])
            e_ref[i % 2, half] = jnp.exp(s - jnp.max(s, axis=-1, keepdims=True)).astype(BF16)

    def values(i):
        rows = slice(i * A_LAT_TQ, (i + 1) * A_LAT_TQ)
        outs = []
        for half in (0, 1):
            acc = _dot(e_ref[i % 2, half], v1_ref[...])
            outs.append(acc[:, :LANES] / acc[:, LANES:])
        o_ref[rows, :] = _diff_finish(outs[0] - lam * outs[1], subln_ref, lam_init)

    return weights, values


def _mixer_a(qkv, cache_k, cache_v, lam_p, subln, lam_init, rope):
    nh = A_HEADS
    lam4 = lam_p.reshape(4, nh, 1, A_HALF_DIM)
    sub2 = subln.reshape(1, A_V_DIM)
    cw = CTX_HEADS * LANES
    nt = nh // CTX_HEADS
    ctx_out = pl.BlockSpec((SEQ, cw), lambda b, h: (b, h))
    kv_out = pl.BlockSpec((SEQ * nh, A_V_DIM), lambda b, h: (b, 0))
    kv_shape = jax.ShapeDtypeStruct((N_PROMPT * nh, A_V_DIM), F32)
    o_p, k_new, v_new = pl.pallas_call(
        functools.partial(_a_ctx_kernel, lam_init=lam_init),
        out_shape=(jax.ShapeDtypeStruct((N_PROMPT, nh * A_V_DIM), BF16), kv_shape, kv_shape),
        grid=(BATCH, nt),
        in_specs=[pl.BlockSpec((4, CTX_HEADS, 1, A_HALF_DIM), lambda b, h: (0, h, 0, 0)),
                  pl.BlockSpec((1, A_V_DIM), lambda b, h: (0, 0)),
                  pl.BlockSpec((SEQ, cw), lambda b, h: (b, h)),
                  pl.BlockSpec((SEQ, cw), lambda b, h: (b, nt + h)),
                  pl.BlockSpec((SEQ, cw), lambda b, h: (b, 2 * nt + h))],
        out_specs=(ctx_out, kv_out, kv_out),
        compiler_params=_params(("arbitrary", "arbitrary"), 40),
        name="diff_attn_ctx",
    )(lam4, sub2, qkv, qkv, qkv)

    table = pl.BlockSpec((DEC_SEQ, LANES), lambda b, h: (0, 0))
    past = pl.BlockSpec((PAST_LEN * nh, A_V_DIM), lambda b, h: (b, 0))
    lh = A_LAT_HEADS
    lw, nl = lh * LANES, nh // lh
    o_s = pl.pallas_call(
        functools.partial(_a_lat_kernel, lam_init=lam_init),
        out_shape=jax.ShapeDtypeStruct((N_SAMPLE, nh * A_V_DIM), BF16),
        grid=(DEC_BATCH, nl),
        in_specs=[pl.BlockSpec((4, lh, 1, A_HALF_DIM), lambda b, h: (0, h, 0, 0)),
                  pl.BlockSpec((1, A_V_DIM), lambda b, h: (0, 0)),
                  table, table, table,
                  pl.BlockSpec((DEC_SEQ, lw), lambda b, h: (SAMPLE_ROW0 + b, h)),
                  pl.BlockSpec((DEC_SEQ, lw), lambda b, h: (SAMPLE_ROW0 + b, nl + h)),
                  pl.BlockSpec((DEC_SEQ, lw), lambda b, h: (SAMPLE_ROW0 + b, 2 * nl + h)),
                  past, past],
        out_specs=pl.BlockSpec((DEC_SEQ, lw), lambda b, h: (b, h)),
        scratch_shapes=[pltpu.VMEM((lh, PAST_LEN + DEC_SEQ, LANES), BF16),
                        pltpu.VMEM((lh, PAST_LEN + DEC_SEQ, 2 * LANES), BF16),
                        pltpu.VMEM((lh, 2, 2, A_LAT_TQ, PAST_LEN + DEC_SEQ), BF16)],
        compiler_params=_params(("arbitrary", "arbitrary"), 56),
        name="diff_attn_lat",
    )(lam4, sub2, *rope, qkv, qkv, qkv,
      cache_k.reshape(DEC_BATCH * PAST_LEN * nh, A_V_DIM), cache_v.reshape(DEC_BATCH * PAST_LEN * nh, A_V_DIM))
    return o_p, o_s, k_new, v_new


B_CHUNK = 256


def _log_sigmoid(x):
    return jnp.minimum(x, 0.0) - jnp.log(1.0 + jnp.exp(-jnp.abs(x)))


def _mlstm_kernel(*refs, heads, **static):
    def head_view(ref, hh, kind):
        if kind == "cols":
            width = ref.shape[1] // heads
            return ref.at[:, hh * width:(hh + 1) * width]
        if kind == "lead":
            return ref.at[hh]
        return ref.at[:, hh]

    kinds = (["cols"] * 4 + ["lead"] * 2 + ["cols"] + (["state"] * 3 if static["has_init"] else [])
             + ["cols"] + (["state"] * 3 if static["emit_state"] else []) + ["lead", "lead"])
    assert len(kinds) == len(refs)
    for hh in range(heads):
        _mlstm_head(*[head_view(r, hh, kd) for r, kd in zip(refs, kinds)], **static)


def _mlstm_head(*refs, n_chunks, has_init, emit_state):
    it = iter(refs)
    q_ref, k_ref, v_ref, og_ref, grow_ref, bcol_ref, nw_ref = (next(it) for _ in range(7))
    if has_init:
        c0_ref, n0_ref, m0_ref = next(it), next(it), next(it)
    h_ref = next(it)
    if emit_state:
        c_out, n_out, m_out = next(it), next(it), next(it)
    hst_ref, vt_ref = next(it), next(it)

    L = B_CHUNK
    si = lax.broadcasted_iota(jnp.int32, (L, L), 0)
    ti = lax.broadcasted_iota(jnp.int32, (L, L), 1)
    grow = grow_ref[...] + bcol_ref[...]
    grow8 = jnp.concatenate([grow, jnp.zeros_like(grow)], axis=0)
    qscale = B_DK ** -0.5
    chunks = [slice(c * L, (c + 1) * L) for c in range(n_chunks)]
    gcols = {}
    for rows in chunks:
        vt_ref[:, rows] = v_ref[rows, :].T
        gcols[rows.start] = grow8[:, rows].T

    for d in (0, 1):
        feeds = (si <= ti) if d == 0 else (si >= ti)
        before = (ti <= si) if d == 0 else (ti >= si)
        last = L - 1 if d == 0 else 0
        if has_init:
            C, n, m = c0_ref[d], n0_ref[d], m0_ref[d][:, :1]
        else:
            C, n, m = jnp.zeros((B_DV, B_DK), F32), jnp.zeros((1, B_DK), F32), jnp.zeros((1, 1), F32)
        for ci in range(n_chunks):
            rows = chunks[ci if d == 0 else n_chunks - 1 - ci]
            qb = (q_ref[rows, :] * qscale).astype(BF16)
            kb = k_ref[rows, :].astype(BF16)
            vt = vt_ref[:, rows]
            ig_row = grow[2 * d:2 * d + 1, rows]
            lf_row = _log_sigmoid(grow[2 * d + 1:2 * d + 2, rows])
            ig_col = gcols[rows.start][:, 2 * d:2 * d + 1]
            lf_col = _log_sigmoid(gcols[rows.start][:, 2 * d + 1:2 * d + 2])
            b_row = jnp.sum(jnp.where(feeds, lf_col, 0.0), axis=0, keepdims=True)
            b_col = jnp.sum(jnp.where(before, lf_row, 0.0), axis=1, keepdims=True)
            dm = jnp.where(feeds, b_row + (ig_col - b_col), NEG)
            inter = b_row + m
            m_t = jnp.maximum(inter, jnp.max(dm, axis=0, keepdims=True))
            w = jnp.exp(dm - m_t)
            a_in = jnp.exp(inter - m_t)
            sw = _dot_nt(kb, qb) * w
            num = _dot(vt.astype(BF16), sw.astype(BF16)) + a_in * _dot_nt(C.astype(BF16), qb)
            nq = _dot_nt(jnp.broadcast_to(n, (8, B_DK)).astype(BF16), qb)[:1]
            den = jnp.sum(sw, axis=0, keepdims=True) + a_in * nq
            h = num * (1.0 / jnp.maximum(jnp.abs(den), jnp.exp(-m_t)))
            if d == 0:
                hst_ref[:, rows] = h
            else:
                hst_ref[:, rows] += h
            if emit_state or ci + 1 < n_chunks:
                m_last = m_t[:, last:last + 1]
                al = a_in[:, last:last + 1]
                wl = jnp.exp(b_row[:, last:last + 1] - b_row + ig_row - m_last)
                C = al * C + _dot((vt * wl).astype(BF16), kb)
                n = al * n + _dot(jnp.broadcast_to(wl, (8, L)).astype(BF16), kb)[:1]
                m = m_last
        if emit_state:
            c_out[d] = C
            n_out[d] = n
            m_out[d] = jnp.broadcast_to(m, (1, LANES))

    for rows in chunks:
        hsum = hst_ref[:, rows].T
        gate = 1.0 / (1.0 + jnp.exp(-og_ref[rows, :]))
        h_ref[rows, :] = (_rms(hsum, nw_ref[...]) * gate).astype(BF16)


def _mlstm_call(z, grow, bias, norm_w, seq, batch, row0, init, heads):
    nh = B_HEADS
    has_init = init is not None
    emit_state = not has_init
    bcol = bias.reshape(2, 2, nh).transpose(2, 0, 1).reshape(nh, 4, 1)
    hp = heads
    qk_tiles = B_QK // (hp * B_DK)
    vd0 = 2 * B_QK // (hp * B_DV)
    nt = nh // hp
    in_specs = [
        pl.BlockSpec((seq, hp * B_DK), lambda b, h: (row0 + b, h)),
        pl.BlockSpec((seq, hp * B_DK), lambda b, h: (row0 + b, qk_tiles + h)),
        pl.BlockSpec((seq, hp * B_DV), lambda b, h: (row0 + b, vd0 + h)),
        pl.BlockSpec((seq, hp * B_DV), lambda b, h: (row0 + b, vd0 + nt + h)),
        pl.BlockSpec((hp, 4, seq), lambda b, h: (h, 0, row0 + b)),
        pl.BlockSpec((hp, 4, 1), lambda b, h: (h, 0, 0)),
        pl.BlockSpec((1, hp * B_DV), lambda b, h: (0, h)),
    ]
    args = [z, z, z, z, grow, bcol, norm_w.reshape(1, B_VD)]
    state_c = pl.BlockSpec((None, 2, hp, B_DV, B_DK), lambda b, h: (b, 0, h, 0, 0))
    state_v = pl.BlockSpec((None, 2, hp, 1, B_DK), lambda b, h: (b, 0, h, 0, 0))
    if has_init:
        c0, n0, m0 = init
        in_specs += [state_c, state_v, state_v]
        args += [c0, n0.reshape(batch, 2, nh, 1, B_DK),
                 jnp.broadcast_to(m0[..., None, None], (batch, 2, nh, 1, B_DK))]
    h_shape = jax.ShapeDtypeStruct((batch * seq, B_VD), BF16)
    h_spec = pl.BlockSpec((seq, hp * B_DV), lambda b, h: (b, h))
    if emit_state:
        out_shape = (h_shape,
                     jax.ShapeDtypeStruct((batch, 2, nh, B_DV, B_DK), F32),
                     jax.ShapeDtypeStruct((batch, 2, nh, 1, B_DK), F32),
                     jax.ShapeDtypeStruct((batch, 2, nh, 1, B_DK), F32))
        out_specs = (h_spec, state_c, state_v, state_v)
    else:
        out_shape, out_specs = h_shape, h_spec
    return pl.pallas_call(
        functools.partial(_mlstm_kernel, heads=hp, n_chunks=seq // B_CHUNK, has_init=has_init,
                          emit_state=emit_state),
        out_shape=out_shape,
        grid=(batch, nt),
        in_specs=in_specs, out_specs=out_specs,
        scratch_shapes=[pltpu.VMEM((hp, B_DV, seq), F32), pltpu.VMEM((hp, B_DV, seq), F32)],
        compiler_params=_params(("arbitrary", "arbitrary"), 48),
        name="mlstm_ctx" if emit_state else "mlstm_lat",
    )(*args)


def _gates_kernel(w_ref, h_ref, o_ref):
    o_ref[...] = _dot_nt(w_ref[...].astype(BF16), h_ref[...])


def _gate_rows(h, w_gate_t):
    nh, tm = B_HEADS, 1024
    w_rows = w_gate_t.reshape(4, nh, D_MODEL).transpose(1, 0, 2).reshape(4 * nh, D_MODEL)
    out = pl.pallas_call(
        _gates_kernel,
        out_shape=jax.ShapeDtypeStruct((4 * nh, N_TOK), F32),
        grid=(N_TOK // tm,),
        in_specs=[pl.BlockSpec((4 * nh, D_MODEL), lambda i: (0, 0)),
                  pl.BlockSpec((tm, D_MODEL), lambda i: (i, 0))],
        out_specs=pl.BlockSpec((4 * nh, tm), lambda i: (0, i)),
        compiler_params=_params(("arbitrary",), 32),
        name="gate_rows",
    )(w_rows, h)
    return out.reshape(nh, 4, N_TOK)


def _mixer_b(z, grow, state_c, state_n, state_m, bias, norm_w):
    nh = B_HEADS
    o_p, c_new, n_new, m_new = _mlstm_call(z, grow, bias, norm_w, SEQ, BATCH, 0, None, heads=4)
    o_s = _mlstm_call(z, grow, bias, norm_w, DEC_SEQ, DEC_BATCH, SAMPLE_ROW0,
                      (state_c, state_n, state_m), heads=1)
    return o_p, o_s, c_new, n_new.reshape(BATCH, 2, nh, B_DK), m_new[:, :, :, 0, 0]


C_SCALE = C_HEAD_DIM ** -0.5


def _c_ctx_kernel(q_ref, k_ref, v_ref, o_ref, ko_ref, vo_ref):
    for hh in range(CTX_HEADS):
        cols = slice(hh * LANES, (hh + 1) * LANES)
        k = k_ref[:, cols]
        v = v_ref[:, cols]
        cache_rows = _head_rows(pl.program_id(1) * CTX_HEADS + hh, SEQ, C_HEADS)
        ko_ref[cache_rows, :] = k
        vo_ref[cache_rows, :] = v
        s = _dot_nt(q_ref[:, cols].astype(BF16), k.astype(BF16)) * C_SCALE
        o_ref[:, cols] = _attend([s], [_with_ones(v)]).astype(BF16)


def _na_row_start(r):
    return min(max(r - NA_ROWS // 2, 0), GRID_H - NA_ROWS)


def _na_row_groups():
    groups = []
    for r in range(GRID_H):
        if groups and _na_row_start(groups[-1][0]) == _na_row_start(r):
            groups[-1].append(r)
        else:
            groups.append([r])
    return groups


C_LAT_HEADS = 2


def _c_lat_kernel(bias_ref, q_ref, k_ref, v_ref, pk_ref, pv_ref, o_ref, *scratch):
    for hh in range(C_LAT_HEADS):
        cols = slice(hh * LANES, (hh + 1) * LANES)
        _c_lat_head(pl.program_id(1) * C_LAT_HEADS + hh, bias_ref.at[hh], q_ref.at[:, cols], k_ref.at[:, cols],
                    v_ref.at[:, cols], pk_ref, pv_ref, o_ref.at[:, cols], *[s.at[hh] for s in scratch])


def _c_lat_head(head, bias_ref, q_ref, k_ref, v_ref, pk_ref, pv_ref, o_ref, kb_ref, v1_ref, sc_ref, ec_ref, acc_ref):
    n_win = NA_ROWS * GRID_W
    kb_ref[...] = k_ref[...].astype(BF16)
    v1_ref[...] = _with_ones(v_ref[...])
    past_rows = _head_rows(head, PAST_LEN, C_HEADS)
    sc_ref[...] = _dot_nt(q_ref[...].astype(BF16), pk_ref[past_rows, :].astype(BF16)) * C_SCALE
    for rows_g in _na_row_groups():
        r0 = _na_row_start(rows_g[0])
        rows = slice(rows_g[0] * GRID_W, (rows_g[-1] + 1) * GRID_W)
        win = slice(r0 * GRID_W, r0 * GRID_W + n_win)
        strips = []
        for r in rows_g:
            strip = NA_ROWS - 1 - (r - r0)
            even = strip - strip % 2
            strips.append(bias_ref[strip % 2, :, even * GRID_W:even * GRID_W + n_win])
        bias = strips[0] if len(strips) == 1 else jnp.concatenate(strips, axis=0)
        s_n = _dot_nt(q_ref[rows, :].astype(BF16), kb_ref[win, :]) * C_SCALE + bias
        s_c = sc_ref[rows, :]
        m = jnp.maximum(jnp.max(s_n, axis=-1, keepdims=True), jnp.max(s_c, axis=-1, keepdims=True))
        ec_ref[rows, :] = jnp.exp(s_c - m).astype(BF16)
        acc_ref[rows, :] = _dot(jnp.exp(s_n - m).astype(BF16), v1_ref[win, :])
    acc = acc_ref[...] + _dot(ec_ref[...], _with_ones(pv_ref[past_rows, :]))
    o_ref[...] = (acc[:, :LANES] / acc[:, LANES:]).astype(BF16)


def _na_bias(rpb):
    n_drow, n_dcol = 2 * NA_ROWS - 1, 2 * NA_COLS - 1
    cq = np.arange(GRID_W)[:, None]
    kc = np.arange(GRID_W)[None, :]
    cstart = np.clip(cq - NA_COLS // 2, 0, GRID_W - NA_COLS)
    col_ok = (kc >= cstart) & (kc < cstart + NA_COLS)
    dcol = np.clip(kc - cq, -(NA_COLS - 1), NA_COLS - 1) + NA_COLS - 1
    onehot = (dcol[:, :, None] == np.arange(n_dcol)).astype(np.float32)
    strips = jnp.einsum('hab,qkb->hqak', rpb.astype(F32), onehot, precision=lax.Precision.HIGHEST)
    strips = jnp.where(col_ok[None, :, None, :], strips, NEG).reshape(C_HEADS, GRID_W, n_drow * GRID_W)
    strips = jnp.pad(strips, ((0, 0), (0, 0), (0, 2 * GRID_W)))
    return jnp.stack([strips[:, :, :(n_drow + 1) * GRID_W], strips[:, :, GRID_W:]], axis=1)


def _mixer_c(qkv, cache_k, cache_v, rpb):
    nh = C_HEADS
    cw = CTX_HEADS * LANES
    nt = nh // CTX_HEADS
    ctx_out = pl.BlockSpec((SEQ, cw), lambda b, h: (b, h))
    kv_out = pl.BlockSpec((SEQ * nh, C_HEAD_DIM), lambda b, h: (b, 0))
    kv_shape = jax.ShapeDtypeStruct((N_PROMPT * nh, C_HEAD_DIM), F32)
    o_p, k_new, v_new = pl.pallas_call(
        _c_ctx_kernel,
        out_shape=(jax.ShapeDtypeStruct((N_PROMPT, D_MODEL), BF16), kv_shape, kv_shape),
        grid=(BATCH, nt),
        in_specs=[pl.BlockSpec((SEQ, cw), lambda b, h: (b, h)),
                  pl.BlockSpec((SEQ, cw), lambda b, h: (b, nt + h)),
                  pl.BlockSpec((SEQ, cw), lambda b, h: (b, 2 * nt + h))],
        out_specs=(ctx_out, kv_out, kv_out),
        compiler_params=_params(("arbitrary", "arbitrary"), 40),
        name="na_ctx",
    )(qkv, qkv, qkv)

    past = pl.BlockSpec((PAST_LEN * nh, C_HEAD_DIM), lambda b, h: (b, 0))
    lh = C_LAT_HEADS
    lw, nl = lh * LANES, nh // lh
    o_s = pl.pallas_call(
        _c_lat_kernel,
        out_shape=jax.ShapeDtypeStruct((N_SAMPLE, D_MODEL), BF16),
        grid=(DEC_BATCH, nl),
        in_specs=[pl.BlockSpec((lh, 2, GRID_W, 2 * NA_ROWS * GRID_W), lambda b, h: (h, 0, 0, 0)),
                  pl.BlockSpec((DEC_SEQ, lw), lambda b, h: (SAMPLE_ROW0 + b, h)),
                  pl.BlockSpec((DEC_SEQ, lw), lambda b, h: (SAMPLE_ROW0 + b, nl + h)),
                  pl.BlockSpec((DEC_SEQ, lw), lambda b, h: (SAMPLE_ROW0 + b, 2 * nl + h)),
                  past, past],
        out_specs=pl.BlockSpec((DEC_SEQ, lw), lambda b, h: (b, h)),
        scratch_shapes=[pltpu.VMEM((lh, DEC_SEQ, LANES), BF16), pltpu.VMEM((lh, DEC_SEQ, 2 * LANES), BF16),
                        pltpu.VMEM((lh, DEC_SEQ, PAST_LEN), F32), pltpu.VMEM((lh, DEC_SEQ, PAST_LEN), BF16),
                        pltpu.VMEM((lh, DEC_SEQ, 2 * LANES), F32)],
        compiler_params=_params(("arbitrary", "arbitrary"), 56),
        name="na_lat",
    )(_na_bias(rpb), qkv, qkv, qkv,
      cache_k.reshape(DEC_BATCH * PAST_LEN * nh, C_HEAD_DIM), cache_v.reshape(DEC_BATCH * PAST_LEN * nh, C_HEAD_DIM))
    return o_p, o_s, k_new, v_new


D_SCALE = D_HEAD_DIM ** -0.5
D_QBLOCK = 128
D_BAND = 3 * D_QBLOCK


D_CTX_PAIRS = 2


def _d_ctx_kernel(sink_ref, q_ref, k_ref, v_ref, o_ref, ko_ref, vo_ref):
    lo = _lane_lo((SEQ, LANES))
    for pp in range(D_CTX_PAIRS):
        pair = pl.program_id(1) * D_CTX_PAIRS + pp
        kv_cols = slice(pp * LANES, (pp + 1) * LANES)
        for half in (0, 1):
            cache_rows = _head_rows(2 * pair + half, SEQ, D_KV_HEADS)
            half_cols = slice(pp * LANES + half * HALF, pp * LANES + (half + 1) * HALF)
            ko_ref[cache_rows, :] = k_ref[:, half_cols]
            vo_ref[cache_rows, :] = v_ref[:, half_cols]
        kb, v1 = k_ref[:, kv_cols].astype(BF16), _with_ones(v_ref[:, kv_cols])
        for tile in range(4):
            kv = tile // 2
            keep = lo if kv == 0 else ~lo
            cols = slice((4 * pp + tile) * LANES, (4 * pp + tile + 1) * LANES)
            qt = q_ref[:, cols] * D_SCALE
            halves = []
            for e in (0, 1):
                qe = qt if e == kv else pltpu.roll(qt, HALF, axis=1)
                s = _dot_nt(jnp.where(keep, qe, 0.0).astype(BF16), kb)
                o = _attend([s], [v1], sink=sink_ref[pair * 2 * D_GROUP + tile * 2 + e])
                halves.append(o if e == kv else pltpu.roll(o, HALF, axis=1))
            o_ref[:, cols] = jnp.where(lo, halves[0], halves[1]).astype(BF16)


def _d_lat_kernel(sink_ref, cos_ref, nxt_ref, prv_ref, q_ref, k_ref, v_ref, pk_ref, pv_ref, o_ref,
                  kb_ref, v1_ref, pkb_ref, pv1_ref, e_ref, es_ref):
    pair = pl.program_id(1)
    kb_ref[...] = _rope(k_ref[...], cos_ref[...], nxt_ref[...], prv_ref[...]).astype(BF16)
    v1_ref[...] = _with_ones(v_ref[...])
    pkb_ref[...] = pk_ref[...].astype(BF16)
    pv1_ref[...] = _with_ones(pv_ref[...])
    tq = D_QBLOCK
    lo = _lane_lo((tq, LANES))
    n_blocks = DEC_SEQ // tq

    def band_of(bi):
        start = min(max((bi - 1) * tq, 0), DEC_SEQ - D_BAND)
        return start, slice(start, start + D_BAND)

    def weights(bi):
        rows = slice(bi * tq, (bi + 1) * tq)
        start, band = band_of(bi)
        cos, nxt, prv = cos_ref[rows, :], nxt_ref[rows, :], prv_ref[rows, :]
        qpos = bi * tq + lax.broadcasted_iota(jnp.int32, (tq, D_BAND), 0)
        kpos = start + lax.broadcasted_iota(jnp.int32, (tq, D_BAND), 1)
        mask = jnp.where(jnp.abs(qpos - kpos) <= WINDOW, 0.0, NEG)
        mask4 = jnp.concatenate([mask] * D_GROUP, axis=0)
        tiles = [_rope(q_ref[rows, t * LANES:(t + 1) * LANES], cos, nxt, prv) * D_SCALE for t in range(4)]
        for kv in (0, 1):
            keep = lo if kv == 0 else ~lo
            stack, sinks = [], []
            for t in (2 * kv, 2 * kv + 1):
                for e in (0, 1):
                    qe = tiles[t] if e == kv else pltpu.roll(tiles[t], HALF, axis=1)
                    stack.append(jnp.where(keep, qe, 0.0))
                    sinks.append(jnp.full((tq, 1), sink_ref[pair * 2 * D_GROUP + t * 2 + e], F32))
            qs = jnp.concatenate(stack, axis=0).astype(BF16)
            sink = jnp.concatenate(sinks, axis=0)
            s_band = _dot_nt(qs, kb_ref[band, :]) + mask4
            s_past = _dot_nt(qs, pkb_ref[...])
            m = jnp.maximum(jnp.maximum(jnp.max(s_band, axis=-1, keepdims=True),
                                        jnp.max(s_past, axis=-1, keepdims=True)), sink)
            e_ref[bi % 2, kv, :, :D_BAND] = jnp.exp(s_band - m).astype(BF16)
            e_ref[bi % 2, kv, :, D_BAND:] = jnp.exp(s_past - m).astype(BF16)
            es_ref[bi % 2, kv] = jnp.exp(sink - m)

    def values(bi):
        rows = slice(bi * tq, (bi + 1) * tq)
        _, band = band_of(bi)
        outs = {}
        for kv in (0, 1):
            acc = (_dot(e_ref[bi % 2, kv, :, :D_BAND], v1_ref[band, :])
                   + _dot(e_ref[bi % 2, kv, :, D_BAND:], pv1_ref[...]))
            o = acc[:, :LANES] / (acc[:, LANES:] + es_ref[bi % 2, kv])
            for i, t in enumerate((2 * kv, 2 * kv + 1)):
                for e in (0, 1):
                    piece = o[(2 * i + e) * tq:(2 * i + e + 1) * tq, :]
                    outs[(t, e)] = piece if e == kv else pltpu.roll(piece, HALF, axis=1)
        for t in range(4):
            o_ref[rows, t * LANES:(t + 1) * LANES] = jnp.where(lo, outs[(t, 0)], outs[(t, 1)]).astype(BF16)

    weights(0)
    for bi in range(1, n_blocks):
        values(bi - 1)
        weights(bi)
    values(n_blocks - 1)


def _mixer_d(qkv, cache_k, cache_v, sink, rope):
    n_pairs = D_KV_HEADS // 2
    qw = 2 * D_GROUP * D_HEAD_DIM
    k0 = D_HEADS * D_HEAD_DIM // LANES
    v0 = k0 + n_pairs
    kvw = D_KV_HEADS * D_HEAD_DIM
    smem = pl.BlockSpec(memory_space=pltpu.SMEM)
    sink = sink.astype(F32)
    cp = D_CTX_PAIRS
    kv_out = pl.BlockSpec((SEQ * D_KV_HEADS, D_HEAD_DIM), lambda b, p: (b, 0))
    kv_shape = jax.ShapeDtypeStruct((N_PROMPT * D_KV_HEADS, D_HEAD_DIM), F32)
    o_p, k_new, v_new = pl.pallas_call(
        _d_ctx_kernel,
        out_shape=(jax.ShapeDtypeStruct((N_PROMPT, D_MODEL), BF16), kv_shape, kv_shape),
        grid=(BATCH, n_pairs // cp),
        in_specs=[smem,
                  pl.BlockSpec((SEQ, cp * qw), lambda b, p: (b, p)),
                  pl.BlockSpec((SEQ, cp * LANES), lambda b, p: (b, k0 // cp + p)),
                  pl.BlockSpec((SEQ, cp * LANES), lambda b, p: (b, v0 // cp + p))],
        out_specs=(pl.BlockSpec((SEQ, cp * qw), lambda b, p: (b, p)), kv_out, kv_out),
        compiler_params=_params(("arbitrary", "arbitrary"), 40),
        name="gqa_ctx",
    )(sink, qkv, qkv, qkv)

    table = pl.BlockSpec((DEC_SEQ, LANES), lambda b, p: (0, 0))
    past = pl.BlockSpec((PAST_LEN, LANES), lambda b, p: (b, p))
    o_s = pl.pallas_call(
        _d_lat_kernel,
        out_shape=jax.ShapeDtypeStruct((N_SAMPLE, D_MODEL), BF16),
        grid=(DEC_BATCH, n_pairs),
        in_specs=[smem, table, table, table,
                  pl.BlockSpec((DEC_SEQ, qw), lambda b, p: (SAMPLE_ROW0 + b, p)),
                  pl.BlockSpec((DEC_SEQ, LANES), lambda b, p: (SAMPLE_ROW0 + b, k0 + p)),
                  pl.BlockSpec((DEC_SEQ, LANES), lambda b, p: (SAMPLE_ROW0 + b, v0 + p)),
                  past, past],
        out_specs=pl.BlockSpec((DEC_SEQ, qw), lambda b, p: (b, p)),
        scratch_shapes=[pltpu.VMEM((DEC_SEQ, LANES), BF16), pltpu.VMEM((DEC_SEQ, 2 * LANES), BF16),
                        pltpu.VMEM((PAST_LEN, LANES), BF16), pltpu.VMEM((PAST_LEN, 2 * LANES), BF16),
                        pltpu.VMEM((2, 2, D_GROUP * D_QBLOCK, D_BAND + PAST_LEN), BF16),
                        pltpu.VMEM((2, 2, D_GROUP * D_QBLOCK, 1), F32)],
        compiler_params=_params(("arbitrary", "arbitrary"), 40),
        name="gqa_lat",
    )(sink, *rope, qkv, qkv, qkv,
      cache_k.reshape(DEC_BATCH * PAST_LEN, kvw), cache_v.reshape(DEC_BATCH * PAST_LEN, kvw))
    return o_p, o_s, k_new, v_new


def kernel(x_prompt, x_sample, cache_a_k, cache_a_v, state_b_C, state_b_n, state_b_m, cache_c_k, cache_c_v,
           cache_d_k, cache_d_v, c, c_ctx, w_mod, b_mod, g_norm, w_ff1, w_ff2, a_w_in, a_w_out, a_lambda,
           a_subln, b_w_in, b_gate_bias, b_w_out, b_norm, c_w_in, c_w_out, c_rpb, d_w_in, d_w_out, d_sink):
    cond = jnp.concatenate([c_ctx[None, :], c, jnp.zeros((N_COND - 1 - DEC_BATCH, D_MODEL), F32)], axis=0)
    mod = _modulation(cond, w_mod, b_mod).reshape(DEPTH, N_COND, N_MOD, 1, D_MODEL)
    gains = g_norm.reshape(DEPTH, 4, 1, D_MODEL)
    rope = _rope_tables()
    new = {name: [] for name in ("a_k", "a_v", "b_C", "b_n", "b_m", "c_k", "c_v", "d_k", "d_v")}

    x, h = _adaln(x_prompt, x_sample, gains, mod, 0)
    for i in range(DEPTH):
        kind, j = i % N_MIXERS, i // N_MIXERS
        if kind == 0:
            lam_init = 0.8 - 0.6 * math.exp(-0.3 * i)
            qkv = _project(h, a_w_in, j)
            o_p, o_s, k_new, v_new = _mixer_a(qkv, cache_a_k[:, j], cache_a_v[:, j], a_lambda[j], a_subln[j],
                                              lam_init, rope)
            w_out = a_w_out[j]
            new["a_k"].append(k_new.reshape(BATCH, SEQ, A_HEADS, A_V_DIM))
            new["a_v"].append(v_new.reshape(BATCH, SEQ, A_HEADS, A_V_DIM))
        elif kind == 1:
            w_in_t = jnp.swapaxes(b_w_in, 1, 2)
            z = _project_t(h, w_in_t, j, B_MAIN)
            grow = _gate_rows(h, w_in_t[j, B_MAIN:])
            o_p, o_s, c_new, n_new, m_new = _mixer_b(z, grow, state_b_C[:, j], state_b_n[:, j], state_b_m[:, j],
                                                     b_gate_bias[j], b_norm[j])
            w_out = b_w_out[j]
            new["b_C"].append(c_new)
            new["b_n"].append(n_new)
            new["b_m"].append(m_new)
        elif kind == 2:
            qkv = _project(h, c_w_in, j)
            o_p, o_s, k_new, v_new = _mixer_c(qkv, cache_c_k[:, j], cache_c_v[:, j], c_rpb[j])
            w_out = c_w_out[j]
            new["c_k"].append(k_new.reshape(BATCH, SEQ, C_HEADS, C_HEAD_DIM))
            new["c_v"].append(v_new.reshape(BATCH, SEQ, C_HEADS, C_HEAD_DIM))
        else:
            qkv = _project(h, d_w_in, j)
            o_p, o_s, k_new, v_new = _mixer_d(qkv, cache_d_k[:, j], cache_d_v[:, j], d_sink[j], rope)
            w_out = d_w_out[j]
            new["d_k"].append(k_new.reshape(BATCH, SEQ, D_KV_HEADS, D_HEAD_DIM))
            new["d_v"].append(v_new.reshape(BATCH, SEQ, D_KV_HEADS, D_HEAD_DIM))
        x, h = _out_project(o_p, o_s, w_out, x, gains, mod, i)
        y = _mlp(h, w_ff1, w_ff2, i)
        x, h = _residual(x, y, gains, mod, i, 5, 3, (i + 1, 0, 0, 1) if i + 1 < DEPTH else None)

    x_p, x_s = x
    stack = lambda name: jnp.stack(new[name], axis=1)
    return (x_p.reshape(BATCH, SEQ, D_MODEL), x_s.reshape(DEC_BATCH, DEC_SEQ, D_MODEL),
            stack("a_k"), stack("a_v"), stack("b_C"), stack("b_n"), stack("b_m"),
            stack("c_k"), stack("c_v"), stack("d_k"), stack("d_v"))
```

```python
import functools
import math

import numpy as np
import jax
import jax.numpy as jnp
from jax import lax
from jax.experimental import pallas as pl
from jax.experimental.pallas import tpu as pltpu

D_MODEL = 2048
BATCH = 16
SEQ = 256
DEPTH = 4
DEC_BATCH = 4
DEC_SEQ = 1024
PAST_LEN = 512
GRID_W = 64
N_MIXERS = 4
D_FF = 4 * D_MODEL
N_MOD = 6
RMS_EPS = 1e-6
ROPE_THETA = 10000.0

A_HEADS = 16
A_HALF_DIM = 64
A_V_DIM = 128
B_HEADS = 8
B_DK = 128
B_DV = 256
B_QK = B_HEADS * B_DK
B_VD = B_HEADS * B_DV
B_MAIN = 2 * B_QK + 2 * B_VD
C_HEADS = 16
C_HEAD_DIM = 128
NA_ROWS = 8
NA_COLS = 16
GRID_H = DEC_SEQ // GRID_W
D_HEADS = 32
D_KV_HEADS = 8
D_GROUP = 4
D_HEAD_DIM = 64
WINDOW = 128

N_PROMPT = BATCH * SEQ
N_SAMPLE = DEC_BATCH * DEC_SEQ
N_TOK = N_PROMPT + N_SAMPLE
N_COND = 8
SAMPLE_ROW0 = N_PROMPT // DEC_SEQ

LANES = 128
HALF = LANES // 2
NEG = -1e30
MIB = 1024 * 1024
F32 = jnp.float32
BF16 = jnp.bfloat16


def _params(sem, vmem_mib):
    return pltpu.CompilerParams(dimension_semantics=sem, vmem_limit_bytes=vmem_mib * MIB)


def _dot(a, b):
    return jnp.dot(a, b, preferred_element_type=F32)


def _dot_nt(a, b):
    return lax.dot_general(a, b, (((1,), (1,)), ((), ())), preferred_element_type=F32)


def _dot_tn(a, b):
    return lax.dot_general(a, b, (((0,), (0,)), ((), ())), preferred_element_type=F32)


def _rms(x, g):
    return x * lax.rsqrt(jnp.mean(x * x, axis=-1, keepdims=True) + RMS_EPS) * g


def _cond_index(row0):
    return jnp.where(row0 < N_PROMPT, 0, 1 + (row0 - N_PROMPT) // DEC_SEQ)


def _mod_kernel(c_ref, w_ref, b_ref, o_ref):
    c = c_ref[...]
    s = (c / (1.0 + jnp.exp(-c))).astype(BF16)
    o_ref[...] = _dot(s, w_ref[...].astype(BF16)) + b_ref[...]


def _modulation(cond, w_mod, b_mod):
    tn = 1024
    n = N_MOD * D_MODEL
    return pl.pallas_call(
        _mod_kernel,
        out_shape=jax.ShapeDtypeStruct((DEPTH, N_COND, n), F32),
        grid=(DEPTH, n // tn),
        in_specs=[
            pl.BlockSpec((N_COND, D_MODEL), lambda l, j: (0, 0)),
            pl.BlockSpec((None, D_MODEL, tn), lambda l, j: (l, 0, j)),
            pl.BlockSpec((None, 1, tn), lambda l, j: (l, 0, j)),
        ],
        out_specs=pl.BlockSpec((None, N_COND, tn), lambda l, j: (l, 0, j)),
        compiler_params=_params(("arbitrary", "arbitrary"), 40),
        name="modulation",
    )(cond, w_mod, b_mod.reshape(DEPTH, 1, n))


ROW_TILE = 512


def _mod_spec(layer, which, tile0=0):
    return pl.BlockSpec((None, None, None, 1, D_MODEL),
                        lambda i: (layer, _cond_index((i + tile0) * ROW_TILE), which, 0, 0))


def _gain_spec(layer, which):
    return pl.BlockSpec((None, None, 1, D_MODEL), lambda i: (layer, which, 0, 0))


def _adaln_kernel(xp_ref, xs_ref, g_ref, shift_ref, scale_ref, x_ref, h_ref):
    def emit(x):
        x_ref[...] = x
        h_ref[...] = (_rms(x, g_ref[...]) * (1.0 + scale_ref[...]) + shift_ref[...]).astype(BF16)

    @pl.when(pl.program_id(0) < N_PROMPT // ROW_TILE)
    def _():
        emit(xp_ref[...])

    @pl.when(pl.program_id(0) >= N_PROMPT // ROW_TILE)
    def _():
        emit(xs_ref[...])


def _adaln(x_prompt, x_sample, gains, mod, layer):
    p_tiles = N_PROMPT // ROW_TILE
    row = pl.BlockSpec((ROW_TILE, D_MODEL), lambda i: (i, 0))
    return pl.pallas_call(
        _adaln_kernel,
        out_shape=(jax.ShapeDtypeStruct((N_TOK, D_MODEL), F32), jax.ShapeDtypeStruct((N_TOK, D_MODEL), BF16)),
        grid=(N_TOK // ROW_TILE,),
        in_specs=[pl.BlockSpec((ROW_TILE, D_MODEL), lambda i: (jnp.minimum(i, p_tiles - 1), 0)),
                  pl.BlockSpec((ROW_TILE, D_MODEL), lambda i: (jnp.maximum(i - p_tiles, 0), 0)),
                  _gain_spec(layer, 0), _mod_spec(layer, 0), _mod_spec(layer, 1)],
        out_specs=(row, row),
        compiler_params=_params(("arbitrary",), 40),
        name="adaln",
    )(x_prompt.reshape(N_PROMPT, D_MODEL), x_sample.reshape(N_SAMPLE, D_MODEL), gains, mod, mod)


def _residual_adaln_kernel(x_ref, y_ref, gate_ref, gpost_ref, gpre_ref, shift_ref, scale_ref, xo_ref, h_ref):
    x = x_ref[...] + gate_ref[...] * _rms(y_ref[...], gpost_ref[...])
    xo_ref[...] = x
    h = _rms(x, gpre_ref[...]) * (1.0 + scale_ref[...]) + shift_ref[...]
    h_ref[...] = h.astype(BF16)


def _residual_kernel(x_ref, y_ref, gate_ref, gpost_ref, xo_ref):
    xo_ref[...] = x_ref[...] + gate_ref[...] * _rms(y_ref[...], gpost_ref[...])


def _residual(x, y, gains, mod, layer, gate_idx, post_idx, nxt):
    row = pl.BlockSpec((ROW_TILE, D_MODEL), lambda i: (i, 0))
    if nxt is None:
        outs = []
        for tile0, n_rows in ((0, N_PROMPT), (N_PROMPT // ROW_TILE, N_SAMPLE)):
            src = pl.BlockSpec((ROW_TILE, D_MODEL), lambda i, t=tile0: (i + t, 0))
            outs.append(pl.pallas_call(
                _residual_kernel, out_shape=jax.ShapeDtypeStruct((n_rows, D_MODEL), F32),
                grid=(n_rows // ROW_TILE,),
                in_specs=[src, src, _mod_spec(layer, gate_idx, tile0), _gain_spec(layer, post_idx)],
                out_specs=row,
                compiler_params=_params(("arbitrary",), 32), name="residual",
            )(x, y, mod, gains))
        return outs, None
    nl, ng, nshift, nscale = nxt
    return pl.pallas_call(
        _residual_adaln_kernel,
        out_shape=(jax.ShapeDtypeStruct((N_TOK, D_MODEL), F32), jax.ShapeDtypeStruct((N_TOK, D_MODEL), BF16)),
        grid=(N_TOK // ROW_TILE,),
        in_specs=[row, row, _mod_spec(layer, gate_idx), _gain_spec(layer, post_idx),
                  _gain_spec(nl, ng), _mod_spec(nl, nshift), _mod_spec(nl, nscale)],
        out_specs=(row, row),
        compiler_params=_params(("arbitrary",), 40), name="residual_adaln",
    )(x, y, mod, gains, gains, mod, mod)


PROJ_TM = 1024


def _proj_kernel(a_ref, w_ref, o_ref, wbf_ref):
    @pl.when(pl.program_id(1) == 0)
    def _():
        wbf_ref[...] = w_ref[...].astype(BF16)

    o_ref[...] = _dot(a_ref[...], wbf_ref[...]).astype(o_ref.dtype)


def _project(a, w, layer, n=None, out_dtype=F32):
    k = w.shape[1]
    n = w.shape[2] if n is None else n
    tm = PROJ_TM
    tn = min(n, 1024)
    return pl.pallas_call(
        _proj_kernel,
        out_shape=jax.ShapeDtypeStruct((N_TOK, n), out_dtype),
        grid=(n // tn, N_TOK // tm),
        in_specs=[pl.BlockSpec((tm, k), lambda j, i: (i, 0)),
                  pl.BlockSpec((None, k, tn), lambda j, i: (layer, 0, j))],
        out_specs=pl.BlockSpec((tm, tn), lambda j, i: (i, j)),
        scratch_shapes=[pltpu.VMEM((k, tn), BF16)],
        compiler_params=_params(("arbitrary", "arbitrary"), 48),
        name="project",
    )(a, w)


def _proj_t_kernel(a_ref, wt_ref, o_ref, wbf_ref):
    @pl.when(pl.program_id(1) == 0)
    def _():
        wbf_ref[...] = wt_ref[...].astype(BF16)

    o_ref[...] = _dot_nt(a_ref[...], wbf_ref[...])


def _project_t(a, w_t, layer, n):
    k = w_t.shape[2]
    tm, tn = PROJ_TM, 1024
    return pl.pallas_call(
        _proj_t_kernel,
        out_shape=jax.ShapeDtypeStruct((N_TOK, n), F32),
        grid=(n // tn, N_TOK // tm),
        in_specs=[pl.BlockSpec((tm, k), lambda j, i: (i, 0)),
                  pl.BlockSpec((None, tn, k), lambda j, i: (layer, j, 0))],
        out_specs=pl.BlockSpec((tm, tn), lambda j, i: (i, j)),
        scratch_shapes=[pltpu.VMEM((tn, k), BF16)],
        compiler_params=_params(("arbitrary", "arbitrary"), 48),
        name="project_t",
    )(a, w_t)


OUT_TM = 512


def _out_proj_kernel(ap_ref, as_ref, w_ref, x_ref, gate_ref, gpost_ref, gpre_ref, shift_ref, scale_ref,
                     xo_ref, h_ref, wbf_ref):
    @pl.when(pl.program_id(0) == 0)
    def _():
        wbf_ref[...] = w_ref[...].astype(BF16)

    def finish(a):
        x = x_ref[...] + gate_ref[...] * _rms(_dot(a, wbf_ref[...]), gpost_ref[...])
        xo_ref[...] = x
        h_ref[...] = (_rms(x, gpre_ref[...]) * (1.0 + scale_ref[...]) + shift_ref[...]).astype(BF16)

    @pl.when(pl.program_id(0) < N_PROMPT // OUT_TM)
    def _():
        finish(ap_ref[...])

    @pl.when(pl.program_id(0) >= N_PROMPT // OUT_TM)
    def _():
        finish(as_ref[...])


def _out_project(o_p, o_s, w_out, x, gains, mod, layer):
    tm = OUT_TM
    p_tiles = N_PROMPT // tm
    row = pl.BlockSpec((tm, D_MODEL), lambda i: (i, 0))

    def mod_spec(which):
        return pl.BlockSpec((None, None, None, 1, D_MODEL), lambda i: (layer, _cond_index(i * tm), which, 0, 0))

    return pl.pallas_call(
        _out_proj_kernel,
        out_shape=(jax.ShapeDtypeStruct((N_TOK, D_MODEL), F32), jax.ShapeDtypeStruct((N_TOK, D_MODEL), BF16)),
        grid=(N_TOK // tm,),
        in_specs=[pl.BlockSpec((tm, D_MODEL), lambda i: (jnp.minimum(i, p_tiles - 1), 0)),
                  pl.BlockSpec((tm, D_MODEL), lambda i: (jnp.maximum(i - p_tiles, 0), 0)),
                  pl.BlockSpec((D_MODEL, D_MODEL), lambda i: (0, 0), pipeline_mode=pl.Buffered(1)),
                  row, mod_spec(2), _gain_spec(layer, 1), _gain_spec(layer, 2), mod_spec(3), mod_spec(4)],
        out_specs=(row, row),
        scratch_shapes=[pltpu.VMEM((D_MODEL, D_MODEL), BF16)],
        compiler_params=_params(("arbitrary",), 58),
        name="out_project",
    )(o_p, o_s, w_out, x, mod, gains, gains, mod, mod)


def _mlp_kernel(h_ref, w1_ref, w2_ref, o_ref):
    @pl.when(pl.program_id(1) == 0)
    def _():
        o_ref[...] = jnp.zeros_like(o_ref)

    u = jnp.maximum(_dot(h_ref[...], w1_ref[...].astype(BF16)), 0.0)
    o_ref[...] += _dot((u * u).astype(BF16), w2_ref[...].astype(BF16))


def _mlp(h, w1, w2, layer):
    tm, tf = 1024, 512
    return pl.pallas_call(
        _mlp_kernel,
        out_shape=jax.ShapeDtypeStruct((N_TOK, D_MODEL), F32),
        grid=(N_TOK // tm, D_FF // tf),
        in_specs=[pl.BlockSpec((tm, D_MODEL), lambda i, f: (i, 0)),
                  pl.BlockSpec((None, D_MODEL, tf), lambda i, f: (layer, 0, f)),
                  pl.BlockSpec((None, tf, D_MODEL), lambda i, f: (layer, f, 0))],
        out_specs=pl.BlockSpec((tm, D_MODEL), lambda i, f: (i, 0)),
        compiler_params=_params(("arbitrary", "arbitrary"), 56),
        name="mlp",
    )(h, w1, w2)


def _rope_tables():
    t = jnp.arange(DEC_SEQ)
    lane = np.arange(LANES)
    f = lane % 32
    first = f < 16
    inv = ROPE_THETA ** (-jnp.arange(16, dtype=F32) / 16)
    pos = jnp.where((lane % 64 < 32)[None, :], (t // GRID_W)[:, None], (t % GRID_W)[:, None]).astype(F32)
    ang = pos * inv[f % 16][None, :]
    cos, sin = jnp.cos(ang), jnp.sin(ang)
    sin_next = jnp.where(first[None, :], -sin, 0.0)
    sin_prev = jnp.where(first[None, :], 0.0, sin)
    return cos, sin_next, sin_prev


def _rope(x, cos, sin_next, sin_prev):
    return (x * cos + pltpu.roll(x, LANES - 16, axis=1) * sin_next
            + pltpu.roll(x, 16, axis=1) * sin_prev)


def _lane_lo(shape):
    return lax.broadcasted_iota(jnp.int32, shape, 1) < HALF


def _with_ones(v):
    return jnp.concatenate([v.astype(BF16), jnp.ones(v.shape, BF16)], axis=1)


def _attend(scores, vals_ones, sink=None):
    m = None
    for s in scores:
        mi = jnp.max(s, axis=-1, keepdims=True)
        m = mi if m is None else jnp.maximum(m, mi)
    if sink is not None:
        m = jnp.maximum(m, sink)
    acc = None
    for s, v in zip(scores, vals_ones):
        part = _dot(jnp.exp(s - m).astype(BF16), v)
        acc = part if acc is None else acc + part
    den = acc[:, LANES:]
    if sink is not None:
        den = den + jnp.exp(sink - m)
    return acc[:, :LANES] / den


CTX_HEADS = 8


A_SCALE = A_HALF_DIM ** -0.5


def _diff_lambda(lp, lam_init):
    a = jnp.sum(lp[0] * lp[1], axis=-1, keepdims=True)
    b = jnp.sum(lp[2] * lp[3], axis=-1, keepdims=True)
    return jnp.exp(a) - jnp.exp(b) + lam_init


def _diff_finish(o, subln_ref, lam_init):
    return (_rms(o, subln_ref[...]) * (1.0 - lam_init)).astype(BF16)


def _head_rows(head, n_tokens, n_heads):
    return pl.ds(head, n_tokens, stride=n_heads)


def _a_ctx_kernel(lam_ref, subln_ref, q_ref, k_ref, v_ref, o_ref, ko_ref, vo_ref, *, lam_init):
    lo = _lane_lo((SEQ, LANES))
    for hh in range(CTX_HEADS):
        cols = slice(hh * LANES, (hh + 1) * LANES)
        q = q_ref[:, cols] * A_SCALE
        k = k_ref[:, cols]
        v = v_ref[:, cols]
        cache_rows = _head_rows(pl.program_id(1) * CTX_HEADS + hh, SEQ, A_HEADS)
        ko_ref[cache_rows, :] = k
        vo_ref[cache_rows, :] = v
        kb, v1 = k.astype(BF16), _with_ones(v)
        o1 = _attend([_dot_nt(jnp.where(lo, q, 0.0).astype(BF16), kb)], [v1])
        o2 = _attend([_dot_nt(jnp.where(lo, 0.0, q).astype(BF16), kb)], [v1])
        lam = _diff_lambda(lam_ref[:, hh], lam_init)
        o_ref[:, cols] = _diff_finish(o1 - lam * o2, subln_ref, lam_init)


A_LAT_TQ = 256


A_LAT_HEADS = 2


def _a_lat_kernel(lam_ref, subln_ref, cos_ref, nxt_ref, prv_ref, q_ref, k_ref, v_ref, pk_ref, pv_ref, o_ref,
                  kb_ref, v1_ref, e_ref, *, lam_init):
    stages = []
    for hh in range(A_LAT_HEADS):
        cols = slice(hh * LANES, (hh + 1) * LANES)
        stages.append(_a_lat_head(pl.program_id(1) * A_LAT_HEADS + hh, lam_ref[:, hh], subln_ref, cos_ref, nxt_ref,
                                  prv_ref, q_ref.at[:, cols], k_ref.at[:, cols], v_ref.at[:, cols], pk_ref, pv_ref,
                                  o_ref.at[:, cols], kb_ref.at[hh], v1_ref.at[hh], e_ref.at[hh], lam_init))
    n_blocks = DEC_SEQ // A_LAT_TQ
    for weights, _ in stages:
        weights(0)
    for i in range(1, n_blocks):
        for _, values in stages:
            values(i - 1)
        for weights, _ in stages:
            weights(i)
    for _, values in stages:
        values(n_blocks - 1)


def _a_lat_head(head, lam_p, subln_ref, cos_ref, nxt_ref, prv_ref, q_ref, k_ref, v_ref, pk_ref, pv_ref, o_ref,
                kb_ref, v1_ref, e_ref, lam_init):
    past_rows = _head_rows(head, PAST_LEN, A_HEADS)
    kb_ref[:PAST_LEN, :] = pk_ref[past_rows, :].astype(BF16)
    kb_ref[PAST_LEN:, :] = _rope(k_ref[...], cos_ref[...], nxt_ref[...], prv_ref[...]).astype(BF16)
    v1_ref[:PAST_LEN, :] = _with_ones(pv_ref[past_rows, :])
    v1_ref[PAST_LEN:, :] = _with_ones(v_ref[...])
    lam = _diff_lambda(lam_p, lam_init)
    lo = _lane_lo((A_LAT_TQ, LANES))

    def weights(i):
        rows = slice(i * A_LAT_TQ, (i + 1) * A_LAT_TQ)
        q = _rope(q_ref[rows, :], cos_ref[rows, :], nxt_ref[rows, :], prv_ref[rows, :]) * A_SCALE
        for half, keep in enumerate((lo, ~lo)):
            s = _dot_nt(jnp.where(keep, q, 0.0).astype(BF16), kb_ref[...])
            e_ref[i % 2, half] = jnp.exp(s - jnp.max(s, axis=-1, keepdims=True)).astype(BF16)

    def values(i):
        rows = slice(i * A_LAT_TQ, (i + 1) * A_LAT_TQ)
        outs = []
        for half in (0, 1):
            acc = _dot(e_ref[i % 2, half], v1_ref[...])
            outs.append(acc[:, :LANES] / acc[:, LANES:])
        o_ref[rows, :] = _diff_finish(outs[0] - lam * outs[1], subln_ref, lam_init)

    return weights, values


def _mixer_a(qkv, cache_k, cache_v, lam_p, subln, lam_init, rope):
    nh = A_HEADS
    lam4 = lam_p.reshape(4, nh, 1, A_HALF_DIM)
    sub2 = subln.reshape(1, A_V_DIM)
    cw = CTX_HEADS * LANES
    nt = nh // CTX_HEADS
    ctx_out = pl.BlockSpec((SEQ, cw), lambda b, h: (b, h))
    kv_out = pl.BlockSpec((SEQ * nh, A_V_DIM), lambda b, h: (b, 0))
    kv_shape = jax.ShapeDtypeStruct((N_PROMPT * nh, A_V_DIM), F32)
    o_p, k_new, v_new = pl.pallas_call(
        functools.partial(_a_ctx_kernel, lam_init=lam_init),
        out_shape=(jax.ShapeDtypeStruct((N_PROMPT, nh * A_V_DIM), BF16), kv_shape, kv_shape),
        grid=(BATCH, nt),
        in_specs=[pl.BlockSpec((4, CTX_HEADS, 1, A_HALF_DIM), lambda b, h: (0, h, 0, 0)),
                  pl.BlockSpec((1, A_V_DIM), lambda b, h: (0, 0)),
                  pl.BlockSpec((SEQ, cw), lambda b, h: (b, h)),
                  pl.BlockSpec((SEQ, cw), lambda b, h: (b, nt + h)),
                  pl.BlockSpec((SEQ, cw), lambda b, h: (b, 2 * nt + h))],
        out_specs=(ctx_out, kv_out, kv_out),
        compiler_params=_params(("arbitrary", "arbitrary"), 40),
        name="diff_attn_ctx",
    )(lam4, sub2, qkv, qkv, qkv)

    table = pl.BlockSpec((DEC_SEQ, LANES), lambda b, h: (0, 0))
    past = pl.BlockSpec((PAST_LEN * nh, A_V_DIM), lambda b, h: (b, 0))
    lh = A_LAT_HEADS
    lw, nl = lh * LANES, nh // lh
    o_s = pl.pallas_call(
        functools.partial(_a_lat_kernel, lam_init=lam_init),
        out_shape=jax.ShapeDtypeStruct((N_SAMPLE, nh * A_V_DIM), BF16),
        grid=(DEC_BATCH, nl),
        in_specs=[pl.BlockSpec((4, lh, 1, A_HALF_DIM), lambda b, h: (0, h, 0, 0)),
                  pl.BlockSpec((1, A_V_DIM), lambda b, h: (0, 0)),
                  table, table, table,
                  pl.BlockSpec((DEC_SEQ, lw), lambda b, h: (SAMPLE_ROW0 + b, h)),
                  pl.BlockSpec((DEC_SEQ, lw), lambda b, h: (SAMPLE_ROW0 + b, nl + h)),
                  pl.BlockSpec((DEC_SEQ, lw), lambda b, h: (SAMPLE_ROW0 + b, 2 * nl + h)),
                  past, past],
        out_specs=pl.BlockSpec((DEC_SEQ, lw), lambda b, h: (b, h)),
        scratch_shapes=[pltpu.VMEM((lh, PAST_LEN + DEC_SEQ, LANES), BF16),
                        pltpu.VMEM((lh, PAST_LEN + DEC_SEQ, 2 * LANES), BF16),
                        pltpu.VMEM((lh, 2, 2, A_LAT_TQ, PAST_LEN + DEC_SEQ), BF16)],
        compiler_params=_params(("arbitrary", "arbitrary"), 56),
        name="diff_attn_lat",
    )(lam4, sub2, *rope, qkv, qkv, qkv,
      cache_k.reshape(DEC_BATCH * PAST_LEN * nh, A_V_DIM), cache_v.reshape(DEC_BATCH * PAST_LEN * nh, A_V_DIM))
    return o_p, o_s, k_new, v_new


B_CHUNK = 256


def _log_sigmoid(x):
    return jnp.minimum(x, 0.0) - jnp.log(1.0 + jnp.exp(-jnp.abs(x)))


def _mlstm_kernel(*refs, heads, **static):
    def head_view(ref, hh, kind):
        if kind == "cols":
            width = ref.shape[1] // heads
            return ref.at[:, hh * width:(hh + 1) * width]
        if kind == "lead":
            return ref.at[hh]
        return ref.at[:, hh]

    kinds = (["cols"] * 4 + ["lead"] * 2 + ["cols"] + (["state"] * 3 if static["has_init"] else [])
             + ["cols"] + (["state"] * 3 if static["emit_state"] else []) + ["lead", "lead"])
    assert len(kinds) == len(refs)
    for hh in range(heads):
        _mlstm_head(*[head_view(r, hh, kd) for r, kd in zip(refs, kinds)], **static)


def _mlstm_head(*refs, n_chunks, has_init, emit_state):
    it = iter(refs)
    q_ref, k_ref, v_ref, og_ref, grow_ref, bcol_ref, nw_ref = (next(it) for _ in range(7))
    if has_init:
        c0_ref, n0_ref, m0_ref = next(it), next(it), next(it)
    h_ref = next(it)
    if emit_state:
        c_out, n_out, m_out = next(it), next(it), next(it)
    hst_ref, vt_ref = next(it), next(it)

    L = B_CHUNK
    si = lax.broadcasted_iota(jnp.int32, (L, L), 0)
    ti = lax.broadcasted_iota(jnp.int32, (L, L), 1)
    grow = grow_ref[...] + bcol_ref[...]
    grow8 = jnp.concatenate([grow, jnp.zeros_like(grow)], axis=0)
    qscale = B_DK ** -0.5
    chunks = [slice(c * L, (c + 1) * L) for c in range(n_chunks)]
    gcols = {}
    for rows in chunks:
        vt_ref[:, rows] = v_ref[rows, :].T
        gcols[rows.start] = grow8[:, rows].T

    for d in (0, 1):
        feeds = (si <= ti) if d == 0 else (si >= ti)
        before = (ti <= si) if d == 0 else (ti >= si)
        last = L - 1 if d == 0 else 0
        if has_init:
            C, n, m = c0_ref[d], n0_ref[d], m0_ref[d][:, :1]
        else:
            C, n, m = jnp.zeros((B_DV, B_DK), F32), jnp.zeros((1, B_DK), F32), jnp.zeros((1, 1), F32)
        for ci in range(n_chunks):
            rows = chunks[ci if d == 0 else n_chunks - 1 - ci]
            qb = (q_ref[rows, :] * qscale).astype(BF16)
            kb = k_ref[rows, :].astype(BF16)
            vt = vt_ref[:, rows]
            ig_row = grow[2 * d:2 * d + 1, rows]
            lf_row = _log_sigmoid(grow[2 * d + 1:2 * d + 2, rows])
            ig_col = gcols[rows.start][:, 2 * d:2 * d + 1]
            lf_col = _log_sigmoid(gcols[rows.start][:, 2 * d + 1:2 * d + 2])
            b_row = jnp.sum(jnp.where(feeds, lf_col, 0.0), axis=0, keepdims=True)
            b_col = jnp.sum(jnp.where(before, lf_row, 0.0), axis=1, keepdims=True)
            dm = jnp.where(feeds, b_row + (ig_col - b_col), NEG)
            inter = b_row + m
            m_t = jnp.maximum(inter, jnp.max(dm, axis=0, keepdims=True))
            w = jnp.exp(dm - m_t)
            a_in = jnp.exp(inter - m_t)
            sw = _dot_nt(kb, qb) * w
            num = _dot(vt.astype(BF16), sw.astype(BF16)) + a_in * _dot_nt(C.astype(BF16), qb)
            nq = _dot_nt(jnp.broadcast_to(n, (8, B_DK)).astype(BF16), qb)[:1]
            den = jnp.sum(sw, axis=0, keepdims=True) + a_in * nq
            h = num * (1.0 / jnp.maximum(jnp.abs(den), jnp.exp(-m_t)))
            if d == 0:
                hst_ref[:, rows] = h
            else:
                hst_ref[:, rows] += h
            if emit_state or ci + 1 < n_chunks:
                m_last = m_t[:, last:last + 1]
                al = a_in[:, last:last + 1]
                wl = jnp.exp(b_row[:, last:last + 1] - b_row + ig_row - m_last)
                C = al * C + _dot((vt * wl).astype(BF16), kb)
                n = al * n + _dot(jnp.broadcast_to(wl, (8, L)).astype(BF16), kb)[:1]
                m = m_last
        if emit_state:
            c_out[d] = C
            n_out[d] = n
            m_out[d] = jnp.broadcast_to(m, (1, LANES))

    for rows in chunks:
        hsum = hst_ref[:, rows].T
        gate = 1.0 / (1.0 + jnp.exp(-og_ref[rows, :]))
        h_ref[rows, :] = (_rms(hsum, nw_ref[...]) * gate).astype(BF16)


def _mlstm_call(z, grow, bias, norm_w, seq, batch, row0, init, heads):
    nh = B_HEADS
    has_init = init is not None
    emit_state = not has_init
    bcol = bias.reshape(2, 2, nh).transpose(2, 0, 1).reshape(nh, 4, 1)
    hp = heads
    qk_tiles = B_QK // (hp * B_DK)
    vd0 = 2 * B_QK // (hp * B_DV)
    nt = nh // hp
    in_specs = [
        pl.BlockSpec((seq, hp * B_DK), lambda b, h: (row0 + b, h)),
        pl.BlockSpec((seq, hp * B_DK), lambda b, h: (row0 + b, qk_tiles + h)),
        pl.BlockSpec((seq, hp * B_DV), lambda b, h: (row0 + b, vd0 + h)),
        pl.BlockSpec((seq, hp * B_DV), lambda b, h: (row0 + b, vd0 + nt + h)),
        pl.BlockSpec((hp, 4, seq), lambda b, h: (h, 0, row0 + b)),
        pl.BlockSpec((hp, 4, 1), lambda b, h: (h, 0, 0)),
        pl.BlockSpec((1, hp * B_DV), lambda b, h: (0, h)),
    ]
    args = [z, z, z, z, grow, bcol, norm_w.reshape(1, B_VD)]
    state_c = pl.BlockSpec((None, 2, hp, B_DV, B_DK), lambda b, h: (b, 0, h, 0, 0))
    state_v = pl.BlockSpec((None, 2, hp, 1, B_DK), lambda b, h: (b, 0, h, 0, 0))
    if has_init:
        c0, n0, m0 = init
        in_specs += [state_c, state_v, state_v]
        args += [c0, n0.reshape(batch, 2, nh, 1, B_DK),
                 jnp.broadcast_to(m0[..., None, None], (batch, 2, nh, 1, B_DK))]
    h_shape = jax.ShapeDtypeStruct((batch * seq, B_VD), BF16)
    h_spec = pl.BlockSpec((seq, hp * B_DV), lambda b, h: (b, h))
    if emit_state:
        out_shape = (h_shape,
                     jax.ShapeDtypeStruct((batch, 2, nh, B_DV, B_DK), F32),
                     jax.ShapeDtypeStruct((batch, 2, nh, 1, B_DK), F32),
                     jax.ShapeDtypeStruct((batch, 2, nh, 1, B_DK), F32))
        out_specs = (h_spec, state_c, state_v, state_v)
    else:
        out_shape, out_specs = h_shape, h_spec
    return pl.pallas_call(
        functools.partial(_mlstm_kernel, heads=hp, n_chunks=seq // B_CHUNK, has_init=has_init,
                          emit_state=emit_state),
        out_shape=out_shape,
        grid=(batch, nt),
        in_specs=in_specs, out_specs=out_specs,
        scratch_shapes=[pltpu.VMEM((hp, B_DV, seq), F32), pltpu.VMEM((hp, B_DV, seq), F32)],
        compiler_params=_params(("arbitrary", "arbitrary"), 48),
        name="mlstm_ctx" if emit_state else "mlstm_lat",
    )(*args)


def _gates_kernel(w_ref, h_ref, o_ref):
    o_ref[...] = _dot_nt(w_ref[...].astype(BF16), h_ref[...])


def _gate_rows(h, w_gate_t):
    nh, tm = B_HEADS, 1024
    w_rows = w_gate_t.reshape(4, nh, D_MODEL).transpose(1, 0, 2).reshape(4 * nh, D_MODEL)
    out = pl.pallas_call(
        _gates_kernel,
        out_shape=jax.ShapeDtypeStruct((4 * nh, N_TOK), F32),
        grid=(N_TOK // tm,),
        in_specs=[pl.BlockSpec((4 * nh, D_MODEL), lambda i: (0, 0)),
                  pl.BlockSpec((tm, D_MODEL), lambda i: (i, 0))],
        out_specs=pl.BlockSpec((4 * nh, tm), lambda i: (0, i)),
        compiler_params=_params(("arbitrary",), 32),
        name="gate_rows",
    )(w_rows, h)
    return out.reshape(nh, 4, N_TOK)


def _mixer_b(z, grow, state_c, state_n, state_m, bias, norm_w):
    nh = B_HEADS
    o_p, c_new, n_new, m_new = _mlstm_call(z, grow, bias, norm_w, SEQ, BATCH, 0, None, heads=4)
    o_s = _mlstm_call(z, grow, bias, norm_w, DEC_SEQ, DEC_BATCH, SAMPLE_ROW0,
                      (state_c, state_n, state_m), heads=1)
    return o_p, o_s, c_new, n_new.reshape(BATCH, 2, nh, B_DK), m_new[:, :, :, 0, 0]


C_SCALE = C_HEAD_DIM ** -0.5


def _c_ctx_kernel(q_ref, k_ref, v_ref, o_ref, ko_ref, vo_ref):
    for hh in range(CTX_HEADS):
        cols = slice(hh * LANES, (hh + 1) * LANES)
        k = k_ref[:, cols]
        v = v_ref[:, cols]
        cache_rows = _head_rows(pl.program_id(1) * CTX_HEADS + hh, SEQ, C_HEADS)
        ko_ref[cache_rows, :] = k
        vo_ref[cache_rows, :] = v
        s = _dot_nt(q_ref[:, cols].astype(BF16), k.astype(BF16)) * C_SCALE
        o_ref[:, cols] = _attend([s], [_with_ones(v)]).astype(BF16)


def _na_row_start(r):
    return min(max(r - NA_ROWS // 2, 0), GRID_H - NA_ROWS)


def _na_row_groups():
    groups = []
    for r in range(GRID_H):
        if groups and _na_row_start(groups[-1][0]) == _na_row_start(r):
            groups[-1].append(r)
        else:
            groups.append([r])
    return groups


C_LAT_HEADS = 2


def _c_lat_kernel(bias_ref, q_ref, k_ref, v_ref, pk_ref, pv_ref, o_ref, *scratch):
    for hh in range(C_LAT_HEADS):
        cols = slice(hh * LANES, (hh + 1) * LANES)
        _c_lat_head(pl.program_id(1) * C_LAT_HEADS + hh, bias_ref.at[hh], q_ref.at[:, cols], k_ref.at[:, cols],
                    v_ref.at[:, cols], pk_ref, pv_ref, o_ref.at[:, cols], *[s.at[hh] for s in scratch])


def _c_lat_head(head, bias_ref, q_ref, k_ref, v_ref, pk_ref, pv_ref, o_ref, kb_ref, v1_ref, sc_ref, ec_ref, acc_ref):
    n_win = NA_ROWS * GRID_W
    kb_ref[...] = k_ref[...].astype(BF16)
    v1_ref[...] = _with_ones(v_ref[...])
    past_rows = _head_rows(head, PAST_LEN, C_HEADS)
    sc_ref[...] = _dot_nt(q_ref[...].astype(BF16), pk_ref[past_rows, :].astype(BF16)) * C_SCALE
    for rows_g in _na_row_groups():
        r0 = _na_row_start(rows_g[0])
        rows = slice(rows_g[0] * GRID_W, (rows_g[-1] + 1) * GRID_W)
        win = slice(r0 * GRID_W, r0 * GRID_W + n_win)
        strips = []
        for r in rows_g:
            strip = NA_ROWS - 1 - (r - r0)
            even = strip - strip % 2
            strips.append(bias_ref[strip % 2, :, even * GRID_W:even * GRID_W + n_win])
        bias = strips[0] if len(strips) == 1 else jnp.concatenate(strips, axis=0)
        s_n = _dot_nt(q_ref[rows, :].astype(BF16), kb_ref[win, :]) * C_SCALE + bias
        s_c = sc_ref[rows, :]
        m = jnp.maximum(jnp.max(s_n, axis=-1, keepdims=True), jnp.max(s_c, axis=-1, keepdims=True))
        ec_ref[rows, :] = jnp.exp(s_c - m).astype(BF16)
        acc_ref[rows, :] = _dot(jnp.exp(s_n - m).astype(BF16), v1_ref[win, :])
    acc = acc_ref[...] + _dot(ec_ref[...], _with_ones(pv_ref[past_rows, :]))
    o_ref[...] = (acc[:, :LANES] / acc[:, LANES:]).astype(BF16)


def _na_bias(rpb):
    n_drow, n_dcol = 2 * NA_ROWS - 1, 2 * NA_COLS - 1
    cq = np.arange(GRID_W)[:, None]
    kc = np.arange(GRID_W)[None, :]
    cstart = np.clip(cq - NA_COLS // 2, 0, GRID_W - NA_COLS)
    col_ok = (kc >= cstart) & (kc < cstart + NA_COLS)
    dcol = np.clip(kc - cq, -(NA_COLS - 1), NA_COLS - 1) + NA_COLS - 1
    onehot = (dcol[:, :, None] == np.arange(n_dcol)).astype(np.float32)
    strips = jnp.einsum('hab,qkb->hqak', rpb.astype(F32), onehot, precision=lax.Precision.HIGHEST)
    strips = jnp.where(col_ok[None, :, None, :], strips, NEG).reshape(C_HEADS, GRID_W, n_drow * GRID_W)
    strips = jnp.pad(strips, ((0, 0), (0, 0), (0, 2 * GRID_W)))
    return jnp.stack([strips[:, :, :(n_drow + 1) * GRID_W], strips[:, :, GRID_W:]], axis=1)


def _mixer_c(qkv, cache_k, cache_v, rpb):
    nh = C_HEADS
    cw = CTX_HEADS * LANES
    nt = nh // CTX_HEADS
    ctx_out = pl.BlockSpec((SEQ, cw), lambda b, h: (b, h))
    kv_out = pl.BlockSpec((SEQ * nh, C_HEAD_DIM), lambda b, h: (b, 0))
    kv_shape = jax.ShapeDtypeStruct((N_PROMPT * nh, C_HEAD_DIM), F32)
    o_p, k_new, v_new = pl.pallas_call(
        _c_ctx_kernel,
        out_shape=(jax.ShapeDtypeStruct((N_PROMPT, D_MODEL), BF16), kv_shape, kv_shape),
        grid=(BATCH, nt),
        in_specs=[pl.BlockSpec((SEQ, cw), lambda b, h: (b, h)),
                  pl.BlockSpec((SEQ, cw), lambda b, h: (b, nt + h)),
                  pl.BlockSpec((SEQ, cw), lambda b, h: (b, 2 * nt + h))],
        out_specs=(ctx_out, kv_out, kv_out),
        compiler_params=_params(("arbitrary", "arbitrary"), 40),
        name="na_ctx",
    )(qkv, qkv, qkv)

    past = pl.BlockSpec((PAST_LEN * nh, C_HEAD_DIM), lambda b, h: (b, 0))
    lh = C_LAT_HEADS
    lw, nl = lh * LANES, nh // lh
    o_s = pl.pallas_call(
        _c_lat_kernel,
        out_shape=jax.ShapeDtypeStruct((N_SAMPLE, D_MODEL), BF16),
        grid=(DEC_BATCH, nl),
        in_specs=[pl.BlockSpec((lh, 2, GRID_W, 2 * NA_ROWS * GRID_W), lambda b, h: (h, 0, 0, 0)),
                  pl.BlockSpec((DEC_SEQ, lw), lambda b, h: (SAMPLE_ROW0 + b, h)),
                  pl.BlockSpec((DEC_SEQ, lw), lambda b, h: (SAMPLE_ROW0 + b, nl + h)),
                  pl.BlockSpec((DEC_SEQ, lw), lambda b, h: (SAMPLE_ROW0 + b, 2 * nl + h)),
                  past, past],
        out_specs=pl.BlockSpec((DEC_SEQ, lw), lambda b, h: (b, h)),
        scratch_shapes=[pltpu.VMEM((lh, DEC_SEQ, LANES), BF16), pltpu.VMEM((lh, DEC_SEQ, 2 * LANES), BF16),
                        pltpu.VMEM((lh, DEC_SEQ, PAST_LEN), F32), pltpu.VMEM((lh, DEC_SEQ, PAST_LEN), BF16),
                        pltpu.VMEM((lh, DEC_SEQ, 2 * LANES), F32)],
        compiler_params=_params(("arbitrary", "arbitrary"), 56),
        name="na_lat",
    )(_na_bias(rpb), qkv, qkv, qkv,
      cache_k.reshape(DEC_BATCH * PAST_LEN * nh, C_HEAD_DIM), cache_v.reshape(DEC_BATCH * PAST_LEN * nh, C_HEAD_DIM))
    return o_p, o_s, k_new, v_new


D_SCALE = D_HEAD_DIM ** -0.5
D_QBLOCK = 128
D_BAND = 3 * D_QBLOCK


D_CTX_PAIRS = 2


def _d_ctx_kernel(sink_ref, q_ref, k_ref, v_ref, o_ref, ko_ref, vo_ref):
    lo = _lane_lo((SEQ, LANES))
    for pp in range(D_CTX_PAIRS):
        pair = pl.program_id(1) * D_CTX_PAIRS + pp
        kv_cols = slice(pp * LANES, (pp + 1) * LANES)
        for half in (0, 1):
            cache_rows = _head_rows(2 * pair + half, SEQ, D_KV_HEADS)
            half_cols = slice(pp * LANES + half * HALF, pp * LANES + (half + 1) * HALF)
            ko_ref[cache_rows, :] = k_ref[:, half_cols]
            vo_ref[cache_rows, :] = v_ref[:, half_cols]
        kb, v1 = k_ref[:, kv_cols].astype(BF16), _with_ones(v_ref[:, kv_cols])
        for tile in range(4):
            kv = tile // 2
            keep = lo if kv == 0 else ~lo
            cols = slice((4 * pp + tile) * LANES, (4 * pp + tile + 1) * LANES)
            qt = q_ref[:, cols] * D_SCALE
            halves = []
            for e in (0, 1):
                qe = qt if e == kv else pltpu.roll(qt, HALF, axis=1)
                s = _dot_nt(jnp.where(keep, qe, 0.0).astype(BF16), kb)
                o = _attend([s], [v1], sink=sink_ref[pair * 2 * D_GROUP + tile * 2 + e])
                halves.append(o if e == kv else pltpu.roll(o, HALF, axis=1))
            o_ref[:, cols] = jnp.where(lo, halves[0], halves[1]).astype(BF16)


def _d_lat_kernel(sink_ref, cos_ref, nxt_ref, prv_ref, q_ref, k_ref, v_ref, pk_ref, pv_ref, o_ref,
                  kb_ref, v1_ref, pkb_ref, pv1_ref, e_ref, es_ref):
    pair = pl.program_id(1)
    kb_ref[...] = _rope(k_ref[...], cos_ref[...], nxt_ref[...], prv_ref[...]).astype(BF16)
    v1_ref[...] = _with_ones(v_ref[...])
    pkb_ref[...] = pk_ref[...].astype(BF16)
    pv1_ref[...] = _with_ones(pv_ref[...])
    tq = D_QBLOCK
    lo = _lane_lo((tq, LANES))
    n_blocks = DEC_SEQ // tq

    def band_of(bi):
        start = min(max((bi - 1) * tq, 0), DEC_SEQ - D_BAND)
        return start, slice(start, start + D_BAND)

    def weights(bi):
        rows = slice(bi * tq, (bi + 1) * tq)
        start, band = band_of(bi)
        cos, nxt, prv = cos_ref[rows, :], nxt_ref[rows, :], prv_ref[rows, :]
        qpos = bi * tq + lax.broadcasted_iota(jnp.int32, (tq, D_BAND), 0)
        kpos = start + lax.broadcasted_iota(jnp.int32, (tq, D_BAND), 1)
        mask = jnp.where(jnp.abs(qpos - kpos) <= WINDOW, 0.0, NEG)
        mask4 = jnp.concatenate([mask] * D_GROUP, axis=0)
        tiles = [_rope(q_ref[rows, t * LANES:(t + 1) * LANES], cos, nxt, prv) * D_SCALE for t in range(4)]
        for kv in (0, 1):
            keep = lo if kv == 0 else ~lo
            stack, sinks = [], []
            for t in (2 * kv, 2 * kv + 1):
                for e in (0, 1):
                    qe = tiles[t] if e == kv else pltpu.roll(tiles[t], HALF, axis=1)
                    stack.append(jnp.where(keep, qe, 0.0))
                    sinks.append(jnp.full((tq, 1), sink_ref[pair * 2 * D_GROUP + t * 2 + e], F32))
            qs = jnp.concatenate(stack, axis=0).astype(BF16)
            sink = jnp.concatenate(sinks, axis=0)
            s_band = _dot_nt(qs, kb_ref[band, :]) + mask4
            s_past = _dot_nt(qs, pkb_ref[...])
            m = jnp.maximum(jnp.maximum(jnp.max(s_band, axis=-1, keepdims=True),
                                        jnp.max(s_past, axis=-1, keepdims=True)), sink)
            e_ref[bi % 2, kv, :, :D_BAND] = jnp.exp(s_band - m).astype(BF16)
            e_ref[bi % 2, kv, :, D_BAND:] = jnp.exp(s_past - m).astype(BF16)
            es_ref[bi % 2, kv] = jnp.exp(sink - m)

    def values(bi):
        rows = slice(bi * tq, (bi + 1) * tq)
        _, band = band_of(bi)
        outs = {}
        for kv in (0, 1):
            acc = (_dot(e_ref[bi % 2, kv, :, :D_BAND], v1_ref[band, :])
                   + _dot(e_ref[bi % 2, kv, :, D_BAND:], pv1_ref[...]))
            o = acc[:, :LANES] / (acc[:, LANES:] + es_ref[bi % 2, kv])
            for i, t in enumerate((2 * kv, 2 * kv + 1)):
                for e in (0, 1):
                    piece = o[(2 * i + e) * tq:(2 * i + e + 1) * tq, :]
                    outs[(t, e)] = piece if e == kv else pltpu.roll(piece, HALF, axis=1)
        for t in range(4):
            o_ref[rows, t * LANES:(t + 1) * LANES] = jnp.where(lo, outs[(t, 0)], outs[(t, 1)]).astype(BF16)

    weights(0)
    for bi in range(1, n_blocks):
        values(bi - 1)
        weights(bi)
    values(n_blocks - 1)


def _mixer_d(qkv, cache_k, cache_v, sink, rope):
    n_pairs = D_KV_HEADS // 2
    qw = 2 * D_GROUP * D_HEAD_DIM
    k0 = D_HEADS * D_HEAD_DIM // LANES
    v0 = k0 + n_pairs
    kvw = D_KV_HEADS * D_HEAD_DIM
    smem = pl.BlockSpec(memory_space=pltpu.SMEM)
    sink = sink.astype(F32)
    cp = D_CTX_PAIRS
    kv_out = pl.BlockSpec((SEQ * D_KV_HEADS, D_HEAD_DIM), lambda b, p: (b, 0))
    kv_shape = jax.ShapeDtypeStruct((N_PROMPT * D_KV_HEADS, D_HEAD_DIM), F32)
    o_p, k_new, v_new = pl.pallas_call(
        _d_ctx_kernel,
        out_shape=(jax.ShapeDtypeStruct((N_PROMPT, D_MODEL), BF16), kv_shape, kv_shape),
        grid=(BATCH, n_pairs // cp),
        in_specs=[smem,
                  pl.BlockSpec((SEQ, cp * qw), lambda b, p: (b, p)),
                  pl.BlockSpec((SEQ, cp * LANES), lambda b, p: (b, k0 // cp + p)),
                  pl.BlockSpec((SEQ, cp * LANES), lambda b, p: (b, v0 // cp + p))],
        out_specs=(pl.BlockSpec((SEQ, cp * qw), lambda b, p: (b, p)), kv_out, kv_out),
        compiler_params=_params(("arbitrary", "arbitrary"), 40),
        name="gqa_ctx",
    )(sink, qkv, qkv, qkv)

    table = pl.BlockSpec((DEC_SEQ, LANES), lambda b, p: (0, 0))
    past = pl.BlockSpec((PAST_LEN, LANES), lambda b, p: (b, p))
    o_s = pl.pallas_call(
        _d_lat_kernel,
        out_shape=jax.ShapeDtypeStruct((N_SAMPLE, D_MODEL), BF16),
        grid=(DEC_BATCH, n_pairs),
        in_specs=[smem, table, table, table,
                  pl.BlockSpec((DEC_SEQ, qw), lambda b, p: (SAMPLE_ROW0 + b, p)),
                  pl.BlockSpec((DEC_SEQ, LANES), lambda b, p: (SAMPLE_ROW0 + b, k0 + p)),
                  pl.BlockSpec((DEC_SEQ, LANES), lambda b, p: (SAMPLE_ROW0 + b, v0 + p)),
                  past, past],
        out_specs=pl.BlockSpec((DEC_SEQ, qw), lambda b, p: (b, p)),
        scratch_shapes=[pltpu.VMEM((DEC_SEQ, LANES), BF16), pltpu.VMEM((DEC_SEQ, 2 * LANES), BF16),
                        pltpu.VMEM((PAST_LEN, LANES), BF16), pltpu.VMEM((PAST_LEN, 2 * LANES), BF16),
                        pltpu.VMEM((2, 2, D_GROUP * D_QBLOCK, D_BAND + PAST_LEN), BF16),
                        pltpu.VMEM((2, 2, D_GROUP * D_QBLOCK, 1), F32)],
        compiler_params=_params(("arbitrary", "arbitrary"), 40),
        name="gqa_lat",
    )(sink, *rope, qkv, qkv, qkv,
      cache_k.reshape(DEC_BATCH * PAST_LEN, kvw), cache_v.reshape(DEC_BATCH * PAST_LEN, kvw))
    return o_p, o_s, k_new, v_new


def kernel(x_prompt, x_sample, cache_a_k, cache_a_v, state_b_C, state_b_n, state_b_m, cache_c_k, cache_c_v,
           cache_d_k, cache_d_v, c, c_ctx, w_mod, b_mod, g_norm, w_ff1, w_ff2, a_w_in, a_w_out, a_lambda,
           a_subln, b_w_in, b_gate_bias, b_w_out, b_norm, c_w_in, c_w_out, c_rpb, d_w_in, d_w_out, d_sink):
    cond = jnp.concatenate([c_ctx[None, :], c, jnp.zeros((N_COND - 1 - DEC_BATCH, D_MODEL), F32)], axis=0)
    mod = _modulation(cond, w_mod, b_mod).reshape(DEPTH, N_COND, N_MOD, 1, D_MODEL)
    gains = g_norm.reshape(DEPTH, 4, 1, D_MODEL)
    rope = _rope_tables()
    new = {name: [] for name in ("a_k", "a_v", "b_C", "b_n", "b_m", "c_k", "c_v", "d_k", "d_v")}

    x, h = _adaln(x_prompt, x_sample, gains, mod, 0)
    for i in range(DEPTH):
        kind, j = i % N_MIXERS, i // N_MIXERS
        if kind == 0:
            lam_init = 0.8 - 0.6 * math.exp(-0.3 * i)
            qkv = _project(h, a_w_in, j)
            o_p, o_s, k_new, v_new = _mixer_a(qkv, cache_a_k[:, j], cache_a_v[:, j], a_lambda[j], a_subln[j],
                                              lam_init, rope)
            w_out = a_w_out[j]
            new["a_k"].append(k_new.reshape(BATCH, SEQ, A_HEADS, A_V_DIM))
            new["a_v"].append(v_new.reshape(BATCH, SEQ, A_HEADS, A_V_DIM))
        elif kind == 1:
            w_in_t = jnp.swapaxes(b_w_in, 1, 2)
            z = _project_t(h, w_in_t, j, B_MAIN)
            grow = _gate_rows(h, w_in_t[j, B_MAIN:])
            o_p, o_s, c_new, n_new, m_new = _mixer_b(z, grow, state_b_C[:, j], state_b_n[:, j], state_b_m[:, j],
                                                     b_gate_bias[j], b_norm[j])
            w_out = b_w_out[j]
            new["b_C"].append(c_new)
            new["b_n"].append(n_new)
            new["b_m"].append(m_new)
        elif kind == 2:
            qkv = _project(h, c_w_in, j)
            o_p, o_s, k_new, v_new = _mixer_c(qkv, cache_c_k[:, j], cache_c_v[:, j], c_rpb[j])
            w_out = c_w_out[j]
            new["c_k"].append(k_new.reshape(BATCH, SEQ, C_HEADS, C_HEAD_DIM))
            new["c_v"].append(v_new.reshape(BATCH, SEQ, C_HEADS, C_HEAD_DIM))
        else:
            qkv = _project(h, d_w_in, j)
            o_p, o_s, k_new, v_new = _mixer_d(qkv, cache_d_k[:, j], cache_d_v[:, j], d_sink[j], rope)
            w_out = d_w_out[j]
            new["d_k"].append(k_new.reshape(BATCH, SEQ, D_KV_HEADS, D_HEAD_DIM))
            new["d_v"].append(v_new.reshape(BATCH, SEQ, D_KV_HEADS, D_HEAD_DIM))
        x, h = _out_project(o_p, o_s, w_out, x, gains, mod, i)
        y = _mlp(h, w_ff1, w_ff2, i)
        x, h = _residual(x, y, gains, mod, i, 5, 3, (i + 1, 0, 0, 1) if i + 1 < DEPTH else None)

    x_p, x_s = x
    stack = lambda name: jnp.stack(new[name], axis=1)
    return (x_p.reshape(BATCH, SEQ, D_MODEL), x_s.reshape(DEC_BATCH, DEC_SEQ, D_MODEL),
            stack("a_k"), stack("a_v"), stack("b_C"), stack("b_n"), stack("b_m"),
            stack("c_k"), stack("c_v"), stack("d_k"), stack("d_v"))
```
